```python
import math
import jax, jax.numpy as jnp
from jax import lax
import numpy as np

D_MODEL = 2048
BATCH = 1
SEQ = 16384
DEPTH = 2

HEAD_DIM = 128
A_HEADS = 8
A_DK = 128
A_DV = 128
A_CHUNK = 64
B_HEADS = 8
B_KV_GROUPS = 2
B_HPG = B_HEADS // B_KV_GROUPS
CMP_LEN = 32
CMP_STRIDE = 16
SLC_LEN = 64
N_SLC = 16
WIN = 512
B_QBLK = 64
C_HEADS = 8
MOBA_BLOCK = 256
MOBA_TOPK = 3
C_QBLK = 32
D_HEADS = 4
D_DK = 128
D_DV = 2 * D_DK
D_QBLK = 128
D_FF = ((8 * D_MODEL + 3 * 256 - 1) // (3 * 256)) * 256

DEEPNORM_ALPHA = (2 * DEPTH) ** 0.25
DEEPNORM_BETA = (8 * DEPTH) ** -0.25
N_EVEN = (DEPTH + 1) // 2
N_ODD = DEPTH // 2
NEG_INF = -1e30
FORCE_SCORE = 1e9
LN_EPS = 1e-5
RMS_EPS = 1e-6

AB_SPLITS = (A_HEADS * A_DK, A_HEADS * A_DK, A_HEADS * A_DV, A_HEADS * A_DV, B_HEADS * HEAD_DIM) + (B_KV_GROUPS * HEAD_DIM,) * 6 + (3 * B_HEADS,)
AB_IN = sum(AB_SPLITS)
AB_MIX = A_HEADS * A_DV + B_HEADS * HEAD_DIM
CD_SPLITS = (C_HEADS * HEAD_DIM,) * 3 + (D_HEADS * 2 * D_DK, D_HEADS * 2 * D_DK, D_HEADS * D_DV)
CD_IN = sum(CD_SPLITS)
CD_MIX = C_HEADS * HEAD_DIM + D_HEADS * D_DV

kernel_name = 'hybrid_hgrn2_nsa_moba_diffattn_deepnorm'


def _split(z, sizes):
    offs = np.cumsum(sizes)[:-1].tolist()
    return jnp.split(z, offs, axis=-1)


def _layer_norm(x, w, b):
    xf = x.astype(jnp.float32)
    mu = xf.mean(-1, keepdims=True)
    var = jnp.square(xf - mu).mean(-1, keepdims=True)
    return ((xf - mu) * lax.rsqrt(var + LN_EPS) * w.astype(jnp.float32) + b.astype(jnp.float32)).astype(x.dtype)


def _rms_norm(x, w):
    xf = x.astype(jnp.float32)
    return xf * lax.rsqrt(jnp.mean(xf * xf, -1, keepdims=True) + RMS_EPS) * w.astype(jnp.float32)


def _masked_softmax(s, mask):
    p = jax.nn.softmax(jnp.where(mask, s.astype(jnp.float32), NEG_INF), axis=-1)
    return p * mask


def _gather_blocks(blocks, idx):
    return jax.vmap(jax.vmap(lambda bl, ix: bl[ix]))(blocks, idx)


def _swiglu(h, w_in, w_out):
    g, u = jnp.split(h @ w_in, 2, axis=-1)
    return (jax.nn.silu(g) * u) @ w_out


def _hgrn2(q, f_raw, i, g, lb, norm_w):
    bsz, t_len, _ = q.shape
    dt = q.dtype
    nc = t_len // A_CHUNK
    f = lb + (1.0 - lb) * jax.nn.sigmoid(f_raw.astype(jnp.float32))
    logf = jnp.log(f)
    k = 1.0 - f
    qf = jax.nn.silu(q.astype(jnp.float32))

    def heads(z, d):
        return z.reshape(bsz, nc, A_CHUNK, A_HEADS, d).transpose(1, 0, 3, 2, 4)

    qc, kc, lc = heads(qf, A_DK), heads(k, A_DK), heads(logf, A_DK)
    vc = heads(i.astype(jnp.float32), A_DV)
    b = jnp.cumsum(lc, axis=3)
    b_last = b[..., -1:, :]
    q_t = qc * jnp.exp(b)
    k_t = kc * jnp.exp(-b)
    causal = jnp.tril(jnp.ones((A_CHUNK, A_CHUNK), dtype=bool))
    att = jnp.where(causal, jnp.einsum('nbhcd,nbhsd->nbhcs', q_t, k_t), 0.0)
    o_intra = jnp.einsum('nbhcs,nbhsv->nbhcv', att, vc)
    u = jnp.einsum('nbhcd,nbhcv->nbhdv', kc * jnp.exp(b_last - b), vc)
    decay = jnp.exp(b_last[..., 0, :])

    def step(state, inp):
        qt, dec, uu = inp
        o = jnp.einsum('bhcd,bhdv->bhcv', qt, state)
        return dec[..., None] * state + uu, o

    s0 = jnp.zeros((bsz, A_HEADS, A_DK, A_DV), jnp.float32)
    _, o_inter = lax.scan(step, s0, (q_t, decay, u))
    o = (o_intra + o_inter).transpose(1, 0, 3, 2, 4).reshape(bsz, t_len, A_HEADS, A_DV)
    gate = jax.nn.silu(g.astype(jnp.float32)).reshape(bsz, t_len, A_HEADS, A_DV)
    o = _rms_norm(o, norm_w) * gate
    return o.reshape(bsz, t_len, A_HEADS * A_DV).astype(dt)


def _cmp_to_slc(p, nb):
    ratio = SLC_LEN // CMP_STRIDE
    n_over = CMP_LEN // CMP_STRIDE
    ax = p.ndim - 1
    right = ratio * nb - p.shape[-1]
    pp = jnp.pad(p, [(0, 0)] * ax + [(n_over - 1, right)])
    terms = []
    for m in range(ratio):
        for n in range(n_over):
            st = m - n + n_over - 1
            terms.append(lax.slice_in_dim(pp, st, st + ratio * (nb - 1) + 1, stride=ratio, axis=ax))
    return sum(terms)


def _nsa(q, k_c, v_c, k_s, v_s, k_w, v_w, gate, pos_k, w1k, w2k, pos_v, w1v, w2v):
    bsz, t_len, _ = q.shape
    dt = q.dtype
    scale = HEAD_DIM ** -0.5
    qh = q.reshape(bsz, t_len, B_KV_GROUPS, B_HPG, HEAD_DIM).transpose(0, 2, 3, 1, 4)

    def kv(z):
        return z.reshape(bsz, t_len, B_KV_GROUPS, HEAD_DIM).transpose(0, 2, 1, 3)

    n_cmp = (t_len - CMP_LEN) // CMP_STRIDE + 1
    cidx = np.arange(n_cmp)[:, None] * CMP_STRIDE + np.arange(CMP_LEN)[None, :]

    def compress(z, pos, w1, w2):
        blk = kv(z)[:, :, cidx] + pos
        hid = jax.nn.silu(blk.reshape(bsz, B_KV_GROUPS, n_cmp, CMP_LEN * HEAD_DIM) @ w1)
        return hid @ w2

    k_cmp = compress(k_c, pos_k, w1k, w2k)
    v_cmp = compress(v_c, pos_v, w1v, w2v)
    cmp_end = jnp.arange(n_cmp) * CMP_STRIDE + (CMP_LEN - 1)
    nb = t_len // SLC_LEN
    n_sel = min(N_SLC, nb)
    k_slc = kv(k_s).reshape(bsz, B_KV_GROUPS, nb, SLC_LEN, HEAD_DIM)
    v_slc = kv(v_s).reshape(bsz, B_KV_GROUPS, nb, SLC_LEN, HEAD_DIM)
    wpad = ((0, 0), (0, 0), (WIN, 0), (0, 0))
    k_win = jnp.pad(kv(k_w), wpad)
    v_win = jnp.pad(kv(v_w), wpad)
    gates = jax.nn.sigmoid(gate.astype(jnp.float32)).reshape(bsz, t_len, B_KV_GROUPS, B_HPG, 3).transpose(0, 2, 3, 1, 4)

    nqb = t_len // B_QBLK
    q_blocks = qh.reshape(bsz, B_KV_GROUPS, B_HPG, nqb, B_QBLK, HEAD_DIM).transpose(3, 0, 1, 2, 4, 5)
    g_blocks = gates.reshape(bsz, B_KV_GROUPS, B_HPG, nqb, B_QBLK, 3).transpose(3, 0, 1, 2, 4, 5)
    starts = jnp.arange(nqb, dtype=jnp.int32) * B_QBLK
    jb = jnp.arange(nb)
    lpos = jnp.arange(SLC_LEN)

    def block(args):
        qb, gb, q0 = args
        t = q0 + jnp.arange(B_QBLK)
        s = jnp.einsum('bghqd,bgnd->bghqn', qb, k_cmp) * scale
        p_c = _masked_softmax(s, cmp_end[None, :] <= t[:, None])
        o_c = jnp.einsum('bghqn,bgnd->bghqd', p_c.astype(dt), v_cmp)
        p_slc = _cmp_to_slc(p_c.sum(axis=2), nb)
        cur = t // SLC_LEN
        valid = jb[None, :] <= cur[:, None]
        forced = (jb[None, :] == 0) | (jb[None, :] == cur[:, None]) | (jb[None, :] == cur[:, None] - 1)
        score = jnp.where(valid, jnp.where(forced, FORCE_SCORE, p_slc), NEG_INF)
        _, idx = lax.top_k(score, n_sel)
        ks = _gather_blocks(k_slc, idx)
        vs = _gather_blocks(v_slc, idx)
        s = jnp.einsum('bghqd,bgqnld->bghqnl', qb, ks) * scale
        kpos = idx[..., None] * SLC_LEN + lpos
        m_s = (kpos <= t[:, None, None])[:, :, None]
        s = s.reshape(s.shape[:4] + (n_sel * SLC_LEN,))
        m_s = m_s.reshape(m_s.shape[:4] + (n_sel * SLC_LEN,))
        p_s = _masked_softmax(s, m_s)
        o_s = jnp.einsum('bghqk,bgqkd->bghqd', p_s.astype(dt), vs.reshape(vs.shape[:3] + (n_sel * SLC_LEN, HEAD_DIM)))
        kw = lax.dynamic_slice_in_dim(k_win, q0, WIN + B_QBLK, axis=2)
        vw = lax.dynamic_slice_in_dim(v_win, q0, WIN + B_QBLK, axis=2)
        wpos = q0 - WIN + jnp.arange(WIN + B_QBLK)
        m_w = (wpos[None, :] <= t[:, None]) & (wpos[None, :] > t[:, None] - WIN) & (wpos[None, :] >= 0)
        s = jnp.einsum('bghqd,bgkd->bghqk', qb, kw) * scale
        p_w = _masked_softmax(s, m_w)
        o_w = jnp.einsum('bghqk,bgkd->bghqd', p_w.astype(dt), vw)
        return (gb[..., 0:1] * o_c + gb[..., 1:2] * o_s + gb[..., 2:3] * o_w).astype(dt)

    o = lax.map(block, (q_blocks, g_blocks, starts))
    return o.transpose(1, 0, 4, 2, 3, 5).reshape(bsz, t_len, B_HEADS * HEAD_DIM)


def _moba(q, k, v):
    bsz, t_len, _ = q.shape
    dt = q.dtype
    scale = HEAD_DIM ** -0.5
    nbm = -(-t_len // MOBA_BLOCK)
    t_pad = nbm * MOBA_BLOCK

    def heads(z):
        return z.reshape(bsz, t_len, C_HEADS, HEAD_DIM).transpose(0, 2, 1, 3)

    qh = heads(q)
    pad = ((0, 0), (0, 0), (0, t_pad - t_len), (0, 0))
    kp = jnp.pad(heads(k), pad)
    vp = jnp.pad(heads(v), pad)
    k_blk = kp.reshape(bsz, C_HEADS, nbm, MOBA_BLOCK, HEAD_DIM)
    v_blk = vp.reshape(bsz, C_HEADS, nbm, MOBA_BLOCK, HEAD_DIM)
    k_mean = k_blk.astype(jnp.float32).mean(axis=3).astype(dt)
    n_top = min(MOBA_TOPK, max(nbm - 1, 1))
    nqb = t_len // C_QBLK
    q_blocks = qh.reshape(bsz, C_HEADS, nqb, C_QBLK, HEAD_DIM).transpose(2, 0, 1, 3, 4)
    starts = jnp.arange(nqb, dtype=jnp.int32) * C_QBLK
    jb = jnp.arange(nbm)

    def block(args):
        qb, q0 = args
        t = q0 + jnp.arange(C_QBLK)
        cur = q0 // MOBA_BLOCK
        gate = jnp.einsum('bhqd,bhnd->bhqn', qb, k_mean).astype(jnp.float32)
        gate = jnp.where(jb < cur, gate, NEG_INF)
        _, idx = lax.top_k(gate, n_top)
        sel_ok = idx < cur
        ks = _gather_blocks(k_blk, idx)
        vs = _gather_blocks(v_blk, idx)
        s_sel = jnp.einsum('bhqd,bhqnld->bhqnl', qb, ks).reshape(bsz, C_HEADS, C_QBLK, n_top * MOBA_BLOCK)
        m_sel = jnp.broadcast_to(sel_ok[..., None], sel_ok.shape + (MOBA_BLOCK,)).reshape(s_sel.shape)
        k_own = lax.dynamic_slice_in_dim(kp, cur * MOBA_BLOCK, MOBA_BLOCK, axis=2)
        v_own = lax.dynamic_slice_in_dim(vp, cur * MOBA_BLOCK, MOBA_BLOCK, axis=2)
        s_own = jnp.einsum('bhqd,bhkd->bhqk', qb, k_own)
        opos = cur * MOBA_BLOCK + jnp.arange(MOBA_BLOCK)
        m_own = jnp.broadcast_to(opos[None, :] <= t[:, None], s_own.shape)
        s = jnp.concatenate([s_sel, s_own], axis=-1) * scale
        m = jnp.concatenate([m_sel, m_own], axis=-1)
        p = _masked_softmax(s, m).astype(dt)
        n_sel_keys = n_top * MOBA_BLOCK
        o = jnp.einsum('bhqk,bhqkd->bhqd', p[..., :n_sel_keys], vs.reshape(bsz, C_HEADS, C_QBLK, n_sel_keys, HEAD_DIM))
        return o + jnp.einsum('bhqk,bhkd->bhqd', p[..., n_sel_keys:], v_own)

    o = lax.map(block, (q_blocks, starts))
    return o.transpose(1, 0, 3, 2, 4).reshape(bsz, t_len, C_HEADS * HEAD_DIM)


def _diff_attn(q, k, v, lq1, lk1, lq2, lk2, subln_w, layer_idx):
    bsz, t_len, _ = q.shape
    dt = q.dtype
    scale = D_DK ** -0.5
    lam_init = 0.8 - 0.6 * math.exp(-0.3 * layer_idx)
    f32 = jnp.float32
    lam = (jnp.exp(jnp.sum(lq1.astype(f32) * lk1.astype(f32))) - jnp.exp(jnp.sum(lq2.astype(f32) * lk2.astype(f32))) + lam_init)
    qh = q.reshape(bsz, t_len, D_HEADS, 2, D_DK).transpose(0, 2, 3, 1, 4)
    kh = k.reshape(bsz, t_len, D_HEADS, 2, D_DK).transpose(0, 2, 3, 1, 4)
    vh = v.reshape(bsz, t_len, D_HEADS, D_DV).transpose(0, 2, 1, 3)
    nqb = t_len // D_QBLK
    q_blocks = qh.reshape(bsz, D_HEADS, 2, nqb, D_QBLK, D_DK).transpose(3, 0, 1, 2, 4, 5)
    starts = jnp.arange(nqb, dtype=jnp.int32) * D_QBLK
    kpos = jnp.arange(t_len)

    def block(args):
        qb, q0 = args
        t = q0 + jnp.arange(D_QBLK)
        s = jnp.einsum('bhmqd,bhmkd->bhmqk', qb, kh) * scale
        p = _masked_softmax(s, kpos[None, :] <= t[:, None])
        w = p[:, :, 0] - lam * p[:, :, 1]
        return jnp.einsum('bhqk,bhkv->bhqv', w.astype(dt), vh)

    o = lax.map(block, (q_blocks, starts))
    o = o.transpose(1, 0, 3, 2, 4).reshape(bsz, t_len, D_HEADS, D_DV)
    o = _rms_norm(o, subln_w) * (1.0 - lam_init)
    return o.reshape(bsz, t_len, D_HEADS * D_DV).astype(dt)


def setup_inputs(seed: int = 0) -> dict:
    key = jax.random.key(seed)
    ks = jax.random.split(key, 22)

    def nrm(k, shape, s):
        return jax.random.normal(k, shape, jnp.float32) * s

    return {
        'x': nrm(ks[0], (BATCH, SEQ, D_MODEL), 1.0),
        'ln_w': 1.0 + nrm(ks[1], (DEPTH, 2, D_MODEL), 0.05),
        'ln_b': nrm(ks[2], (DEPTH, 2, D_MODEL), 0.02),
        'ffn_w_in': nrm(ks[3], (DEPTH, D_MODEL, 2 * D_FF), D_MODEL ** -0.5),
        'ffn_w_out': nrm(ks[4], (DEPTH, D_FF, D_MODEL), DEEPNORM_BETA * D_FF ** -0.5),
        'ab_w_in': nrm(ks[5], (N_EVEN, D_MODEL, AB_IN), D_MODEL ** -0.5),
        'ab_w_out': nrm(ks[6], (N_EVEN, AB_MIX, D_MODEL), DEEPNORM_BETA * AB_MIX ** -0.5),
        'hgrn_lower_bounds': nrm(ks[7], (DEPTH + 1, A_HEADS * A_DK), 0.1),
        'hgrn_norm_w': 1.0 + nrm(ks[8], (N_EVEN, A_DV), 0.05),
        'nsa_cmp_pos_k': nrm(ks[9], (N_EVEN, CMP_LEN, HEAD_DIM), 0.02),
        'nsa_cmp_k_w1': nrm(ks[10], (N_EVEN, CMP_LEN * HEAD_DIM, HEAD_DIM), (CMP_LEN * HEAD_DIM) ** -0.5),
        'nsa_cmp_k_w2': nrm(ks[11], (N_EVEN, HEAD_DIM, HEAD_DIM), HEAD_DIM ** -0.5),
        'nsa_cmp_pos_v': nrm(ks[12], (N_EVEN, CMP_LEN, HEAD_DIM), 0.02),
        'nsa_cmp_v_w1': nrm(ks[13], (N_EVEN, CMP_LEN * HEAD_DIM, HEAD_DIM), (CMP_LEN * HEAD_DIM) ** -0.5),
        'nsa_cmp_v_w2': nrm(ks[14], (N_EVEN, HEAD_DIM, HEAD_DIM), HEAD_DIM ** -0.5),
        'cd_w_in': nrm(ks[15], (N_ODD, D_MODEL, CD_IN), D_MODEL ** -0.5),
        'cd_w_out': nrm(ks[16], (N_ODD, CD_MIX, D_MODEL), DEEPNORM_BETA * CD_MIX ** -0.5),
        'diff_lambda_q1': nrm(ks[17], (N_ODD, D_DK), 0.1),
        'diff_lambda_k1': nrm(ks[18], (N_ODD, D_DK), 0.1),
        'diff_lambda_q2': nrm(ks[19], (N_ODD, D_DK), 0.1),
        'diff_lambda_k2': nrm(ks[20], (N_ODD, D_DK), 0.1),
        'diff_subln_w': 1.0 + nrm(ks[21], (N_ODD, D_DV), 0.05),
    }


def reference(x, ln_w, ln_b, ffn_w_in, ffn_w_out, ab_w_in, ab_w_out, hgrn_lower_bounds, hgrn_norm_w,
              nsa_cmp_pos_k, nsa_cmp_k_w1, nsa_cmp_k_w2, nsa_cmp_pos_v, nsa_cmp_v_w1, nsa_cmp_v_w2,
              cd_w_in, cd_w_out, diff_lambda_q1, diff_lambda_k1, diff_lambda_q2, diff_lambda_k2, diff_subln_w):
    lb_all = jnp.cumsum(jax.nn.softmax(hgrn_lower_bounds.astype(jnp.float32), axis=0), axis=0)
    h = x
    for layer in range(DEPTH):
        if layer % 2 == 0:
            e = layer // 2
            (a_q, a_f, a_i, a_g, b_q, b_kc, b_vc, b_ks, b_vs, b_kw, b_vw, b_g) = _split(h @ ab_w_in[e], AB_SPLITS)
            o_a = _hgrn2(a_q, a_f, a_i, a_g, lb_all[layer], hgrn_norm_w[e])
            o_b = _nsa(b_q, b_kc, b_vc, b_ks, b_vs, b_kw, b_vw, b_g,
                       nsa_cmp_pos_k[e], nsa_cmp_k_w1[e], nsa_cmp_k_w2[e],
                       nsa_cmp_pos_v[e], nsa_cmp_v_w1[e], nsa_cmp_v_w2[e])
            mix = jnp.concatenate([o_a, o_b], axis=-1) @ ab_w_out[e]
        else:
            o_idx = layer // 2
            (c_q, c_k, c_v, d_q, d_k, d_v) = _split(h @ cd_w_in[o_idx], CD_SPLITS)
            o_c = _moba(c_q, c_k, c_v)
            o_d = _diff_attn(d_q, d_k, d_v, diff_lambda_q1[o_idx], diff_lambda_k1[o_idx],
                             diff_lambda_q2[o_idx], diff_lambda_k2[o_idx], diff_subln_w[o_idx], layer)
            mix = jnp.concatenate([o_c, o_d], axis=-1) @ cd_w_out[o_idx]
        h = _layer_norm(DEEPNORM_ALPHA * h + mix, ln_w[layer, 0], ln_b[layer, 0])
        h = _layer_norm(DEEPNORM_ALPHA * h + _swiglu(h, ffn_w_in[layer], ffn_w_out[layer]), ln_w[layer, 1], ln_b[layer, 1])
    return h
```

```python
import functools
import math

import numpy as np
import jax
import jax.numpy as jnp
from jax import lax
from jax.experimental import pallas as pl
from jax.experimental.pallas import tpu as pltpu

F32 = jnp.float32
BF16 = jnp.bfloat16

D_MODEL = 2048
DEPTH = 2
HEAD_DIM = 128
A_HEADS = 8
A_CHUNK = 64
B_HEADS = 8
B_KV_GROUPS = 2
B_HPG = B_HEADS // B_KV_GROUPS
CMP_LEN = 32
CMP_STRIDE = 16
SLC_LEN = 64
N_SLC = 16
WIN = 512
C_HEADS = 8
MOBA_BLOCK = 256
MOBA_TOPK = 3
D_HEADS = 4
D_DK = 128
D_DV = 2 * D_DK
D_FF = ((8 * D_MODEL + 3 * 256 - 1) // (3 * 256)) * 256

DEEPNORM_ALPHA = (2 * DEPTH) ** 0.25
NEG_INF = -1e30
FORCE_SCORE = 1e9
LN_EPS = 1e-5
RMS_EPS = 1e-6
ATT_SCALE = HEAD_DIM ** -0.5

A_W = A_HEADS * HEAD_DIM
B_QW = B_HEADS * HEAD_DIM
B_KVW = B_KV_GROUPS * HEAD_DIM
C_W = C_HEADS * HEAD_DIM
D_QW = D_HEADS * 2 * D_DK
D_VW = D_HEADS * D_DV

VMEM_LIMIT = 56 * 1024 * 1024


def _cparams(sem):
    return pltpu.CompilerParams(dimension_semantics=sem, vmem_limit_bytes=VMEM_LIMIT)


def _dot(a, b):
    return jnp.dot(a, b, preferred_element_type=F32)


def _dot_nt(a, b):
    return lax.dot_general(a, b, (((1,), (1,)), ((), ())), preferred_element_type=F32)


def _dot_tn(a, b):
    return lax.dot_general(a, b, (((0,), (0,)), ((), ())), preferred_element_type=F32)


def _sigmoid(x):
    return 1.0 / (1.0 + jnp.exp(-x))


def _silu(x):
    return x * _sigmoid(x)


def _mm_kernel(x_ref, w_ref, o_ref):
    o_ref[...] = _dot(x_ref[...], w_ref[...]).astype(o_ref.dtype)


def _matmul(x, w, out_dtype, tm=1024, tn=512):
    m, k = x.shape
    n = w.shape[1]
    tn = min(tn, n)
    return pl.pallas_call(
        _mm_kernel,
        grid=(m // tm, n // tn),
        in_specs=[pl.BlockSpec((tm, k), lambda i, j: (i, 0)),
                  pl.BlockSpec((k, tn), lambda i, j: (0, j))],
        out_specs=pl.BlockSpec((tm, tn), lambda i, j: (i, j)),
        out_shape=jax.ShapeDtypeStruct((m, n), out_dtype),
        compiler_params=_cparams(("parallel", "parallel")),
    )(x, w)


def _ffn_in_kernel(x_ref, wg_ref, wu_ref, o_ref):
    x = x_ref[...]
    g = _dot(x, wg_ref[...])
    u = _dot(x, wu_ref[...])
    o_ref[...] = (_silu(g) * u).astype(o_ref.dtype)


def _ffn_in(x, w_in, tm=1024, tn=512):
    m, k = x.shape
    dff = w_in.shape[1] // 2
    nj = dff // tn
    return pl.pallas_call(
        _ffn_in_kernel,
        grid=(m // tm, nj),
        in_specs=[pl.BlockSpec((tm, k), lambda i, j: (i, 0)),
                  pl.BlockSpec((k, tn), lambda i, j: (0, j)),
                  pl.BlockSpec((k, tn), lambda i, j: (0, j + nj))],
        out_specs=pl.BlockSpec((tm, tn), lambda i, j: (i, j)),
        out_shape=jax.ShapeDtypeStruct((m, dff), BF16),
        compiler_params=_cparams(("parallel", "parallel")),
    )(x, w_in, w_in)


def _mm_res_ln_kernel(a_ref, w_ref, h_ref, lnw_ref, lnb_ref, o_ref, obf_ref, acc_ref, *, nk):
    k = pl.program_id(1)

    @pl.when(k == 0)
    def _():
        acc_ref[...] = jnp.zeros_like(acc_ref)

    acc_ref[...] += _dot(a_ref[...], w_ref[...])

    @pl.when(k == nk - 1)
    def _():
        y = DEEPNORM_ALPHA * h_ref[...] + acc_ref[...]
        mu = jnp.mean(y, axis=-1, keepdims=True)
        d = y - mu
        var = jnp.mean(d * d, axis=-1, keepdims=True)
        out = d * lax.rsqrt(var + LN_EPS) * lnw_ref[...] + lnb_ref[...]
        o_ref[...] = out
        obf_ref[...] = out.astype(BF16)


def _mm_res_ln(a, w, h, ln_w, ln_b, tm=512, tk=512):
    m, kdim = a.shape
    n = w.shape[1]
    nk = kdim // tk
    return pl.pallas_call(
        functools.partial(_mm_res_ln_kernel, nk=nk),
        grid=(m // tm, nk),
        in_specs=[pl.BlockSpec((tm, tk), lambda i, k: (i, k)),
                  pl.BlockSpec((tk, n), lambda i, k: (k, 0)),
                  pl.BlockSpec((tm, n), lambda i, k: (i, 0)),
                  pl.BlockSpec((1, n), lambda i, k: (0, 0)),
                  pl.BlockSpec((1, n), lambda i, k: (0, 0))],
        out_specs=[pl.BlockSpec((tm, n), lambda i, k: (i, 0)),
                   pl.BlockSpec((tm, n), lambda i, k: (i, 0))],
        out_shape=[jax.ShapeDtypeStruct((m, n), F32), jax.ShapeDtypeStruct((m, n), BF16)],
        scratch_shapes=[pltpu.VMEM((tm, n), F32)],
        compiler_params=_cparams(("parallel", "arbitrary")),
    )(a, w, h, ln_w.reshape(1, n), ln_b.reshape(1, n))


def _flash_init(m_ref, l_ref, acc_ref):
    m_ref[...] = jnp.full(m_ref.shape, -jnp.inf, F32)
    l_ref[...] = jnp.zeros(l_ref.shape, F32)
    acc_ref[...] = jnp.zeros(acc_ref.shape, F32)


def _flash_update(s, v, m_ref, l_ref, acc_ref):
    m_prev = m_ref[...]
    m_new = jnp.maximum(m_prev, jnp.max(s, axis=1, keepdims=True))
    alpha = jnp.exp(m_prev - m_new)
    p = jnp.exp(s - m_new)
    l_ref[...] = alpha * l_ref[...] + jnp.sum(p, axis=1, keepdims=True)
    acc_ref[...] = alpha * acc_ref[...] + _dot(p.astype(BF16), v)
    m_ref[...] = m_new


HG_TILE = 512


def _hgrn_kernel(q_ref, f_ref, i_ref, g_ref, lb_ref, nw_ref, o_ref, st_ref):
    @pl.when(pl.program_id(1) == 0)
    def _():
        st_ref[...] = jnp.zeros_like(st_ref)

    lb = lb_ref[...]
    f = lb + (1.0 - lb) * _sigmoid(f_ref[...])
    logf = jnp.log(f)
    kk = 1.0 - f
    qf = _silu(q_ref[...])
    r64 = lax.broadcasted_iota(jnp.int32, logf.shape, 0) & (A_CHUNK - 1)
    b = logf
    step = 1
    while step < A_CHUNK:
        b = b + jnp.where(r64 >= step, pltpu.roll(b, step, 0), 0.0)
        step *= 2
    causal = (lax.broadcasted_iota(jnp.int32, (A_CHUNK, A_CHUNK), 0)
              >= lax.broadcasted_iota(jnp.int32, (A_CHUNK, A_CHUNK), 1))
    nw = nw_ref[...]
    for c in range(HG_TILE // A_CHUNK):
        sl = slice(c * A_CHUNK, (c + 1) * A_CHUNK)
        bc = b[sl]
        b_last = bc[A_CHUNK - 1:A_CHUNK, :]
        q_t = (qf[sl] * jnp.exp(bc)).astype(BF16)
        k_t = (kk[sl] * jnp.exp(-bc)).astype(BF16)
        vc = i_ref[sl, :].astype(BF16)
        att = jnp.where(causal, _dot_nt(q_t, k_t), 0.0)
        st = st_ref[...]
        o = _dot(att.astype(BF16), vc) + _dot_nt(q_t, st.astype(BF16))
        kdec = (kk[sl] * jnp.exp(b_last - bc)).astype(BF16)
        st_ref[...] = st * jnp.exp(b_last) + _dot_tn(vc, kdec)
        o = o * lax.rsqrt(jnp.mean(o * o, axis=-1, keepdims=True) + RMS_EPS) * nw
        o_ref[sl, :] = (o * _silu(g_ref[sl, :])).astype(o_ref.dtype)


def _hgrn2(z_a, lb, norm_w):
    t = z_a.shape[0]
    nh = A_HEADS
    col = lambda base: (lambda h, i: (i, base + h))
    return pl.pallas_call(
        _hgrn_kernel,
        grid=(nh, t // HG_TILE),
        in_specs=[pl.BlockSpec((HG_TILE, HEAD_DIM), col(0)),
                  pl.BlockSpec((HG_TILE, HEAD_DIM), col(nh)),
                  pl.BlockSpec((HG_TILE, HEAD_DIM), col(2 * nh)),
                  pl.BlockSpec((HG_TILE, HEAD_DIM), col(3 * nh)),
                  pl.BlockSpec((1, HEAD_DIM), lambda h, i: (0, h)),
                  pl.BlockSpec((1, HEAD_DIM), lambda h, i: (0, 0))],
        out_specs=pl.BlockSpec((HG_TILE, HEAD_DIM), lambda h, i: (i, h)),
        out_shape=jax.ShapeDtypeStruct((t, A_W), BF16),
        scratch_shapes=[pltpu.VMEM((HEAD_DIM, HEAD_DIM), F32)],
        compiler_params=_cparams(("parallel", "arbitrary")),
    )(z_a, z_a, z_a, z_a, lb.reshape(1, A_W), norm_w.reshape(1, HEAD_DIM))


SEG_W = CMP_STRIDE * HEAD_DIM


def _nsa_compress_kernel(seg_ref, pos_ref, w1_ref, w2_ref, o_ref):
    seg = seg_ref[0, 0]
    nseg = seg.shape[0]
    a = _dot(seg, w1_ref[0, :SEG_W, :])
    b = _dot(seg, w1_ref[0, SEG_W:, :])
    b_next = pltpu.roll(b, nseg - 1, 0)
    pos = jnp.broadcast_to(pos_ref[0], (8, CMP_LEN * HEAD_DIM))
    c = _dot(pos, w1_ref[0])[0:1, :]
    hid = _silu(a + b_next + c)
    o_ref[0, 0] = _dot(hid.astype(BF16), w2_ref[0]).astype(o_ref.dtype)


def _nsa_compress(seg, pos, w1, w2):
    _, ng, nseg, _ = seg.shape
    return pl.pallas_call(
        _nsa_compress_kernel,
        grid=(2, ng),
        in_specs=[pl.BlockSpec((1, 1, nseg, SEG_W), lambda a, g: (a, g, 0, 0)),
                  pl.BlockSpec((1, 1, CMP_LEN * HEAD_DIM), lambda a, g: (a, 0, 0)),
                  pl.BlockSpec((1, CMP_LEN * HEAD_DIM, HEAD_DIM), lambda a, g: (a, 0, 0)),
                  pl.BlockSpec((1, HEAD_DIM, HEAD_DIM), lambda a, g: (a, 0, 0))],
        out_specs=pl.BlockSpec((1, 1, nseg, HEAD_DIM), lambda a, g: (a, g, 0, 0)),
        out_shape=jax.ShapeDtypeStruct((2, ng, nseg, HEAD_DIM), BF16),
        compiler_params=_cparams(("parallel", "parallel")),
    )(seg, pos, w1, w2)


CS_TQ = 256


def _split3(x):
    hi = x.astype(BF16)
    r = x - hi.astype(F32)
    mid = r.astype(BF16)
    lo = (r - mid.astype(F32)).astype(BF16)
    return hi, mid, lo


def _topk_mask(score, lane, k):
    sel = jnp.zeros(score.shape, F32)
    width = float(score.shape[1])
    for _ in range(k):
        mx = jnp.max(score, axis=1, keepdims=True)
        idx = jnp.min(jnp.where(score == mx, lane, width), axis=1, keepdims=True)
        hit = lane == idx
        sel = jnp.where(hit, 1.0, sel)
        score = jnp.where(hit, -jnp.inf, score)
    return sel


def _nsa_cmp_kernel(q_ref, kc_ref, vc_ref, map_ref, o_ref, sel_ref):
    i = pl.program_id(1)
    tq = q_ref.shape[0]
    ncmp = kc_ref.shape[2]
    nb = map_ref.shape[1]
    t_pos = i * tq + lax.broadcasted_iota(jnp.int32, (tq, 1), 0)
    cmp_end = lax.broadcasted_iota(jnp.int32, (1, ncmp), 1) * CMP_STRIDE + (CMP_LEN - 1)
    mask = cmp_end <= t_pos
    kc = kc_ref[0, 0]
    vc = vc_ref[0, 0]
    psum = jnp.zeros((tq, ncmp), F32)
    for hh in range(B_HPG):
        hs = slice(hh * HEAD_DIM, (hh + 1) * HEAD_DIM)
        s = _dot_nt(q_ref[:, hs], kc) * ATT_SCALE
        s = jnp.where(mask, s, NEG_INF)
        e = jnp.exp(s - jnp.max(s, axis=1, keepdims=True))
        p = jnp.where(mask, e / jnp.sum(e, axis=1, keepdims=True), 0.0)
        o_ref[:, hs] = _dot(p.astype(BF16), vc).astype(o_ref.dtype)
        psum = psum + p
    hi, mid, lo = _split3(psum)
    cmap = map_ref[...]
    p_slc = _dot(hi, cmap) + _dot(mid, cmap) + _dot(lo, cmap)
    jb = lax.broadcasted_iota(jnp.int32, (1, nb), 1)
    cur = t_pos // SLC_LEN
    valid = jb <= cur
    forced = (jb == 0) | (jb == cur) | (jb == cur - 1)
    score = jnp.where(valid, jnp.where(forced, FORCE_SCORE, p_slc), NEG_INF)
    sel = _topk_mask(score, jb.astype(F32), min(N_SLC, nb))
    sel_ref[0] = jnp.where(valid, sel, 0.0).astype(sel_ref.dtype)


def _cmp_to_slc_matrix(ncmp_pad, n_cmp, nb):
    ratio = SLC_LEN // CMP_STRIDE
    n_over = CMP_LEN // CMP_STRIDE
    mat = np.zeros((ncmp_pad, nb), np.float32)
    for j in range(nb):
        for m in range(ratio):
            for n in range(n_over):
                c = ratio * j + m - n
                if 0 <= c < n_cmp:
                    mat[c, j] += 1.0
    return mat


def _nsa_cmp_select(z_b, cmp_kv):
    t = z_b.shape[0]
    ng = B_KV_GROUPS
    nseg = cmp_kv.shape[2]
    nb = t // SLC_LEN
    n_cmp = (t - CMP_LEN) // CMP_STRIDE + 1
    cmap = jnp.asarray(_cmp_to_slc_matrix(nseg, n_cmp, nb), BF16)
    gw = B_HPG * HEAD_DIM
    return pl.pallas_call(
        _nsa_cmp_kernel,
        grid=(ng, t // CS_TQ),
        in_specs=[pl.BlockSpec((CS_TQ, gw), lambda g, i: (i, g)),
                  pl.BlockSpec((1, 1, nseg, HEAD_DIM), lambda g, i: (0, g, 0, 0)),
                  pl.BlockSpec((1, 1, nseg, HEAD_DIM), lambda g, i: (1, g, 0, 0)),
                  pl.BlockSpec((nseg, nb), lambda g, i: (0, 0))],
        out_specs=[pl.BlockSpec((CS_TQ, gw), lambda g, i: (i, g)),
                   pl.BlockSpec((1, CS_TQ, nb), lambda g, i: (g, i, 0))],
        out_shape=[jax.ShapeDtypeStruct((t, B_QW), BF16),
                   jax.ShapeDtypeStruct((ng, t, nb), BF16)],
        compiler_params=_cparams(("parallel", "parallel")),
    )(z_b, cmp_kv, cmp_kv, cmap)


SL_TQ = 128
SL_TK = 256


def _nsa_slc_kernel(q_ref, sel_ref, k_ref, v_ref, o_ref, qa_ref, m_ref, l_ref, acc_ref):
    i = pl.program_id(1)
    tq = SL_TQ
    bias = ((sel_ref[0].astype(F32) - 1.0) * (-NEG_INF)).astype(BF16)
    for hh in range(B_HPG):
        rows = slice(hh * tq, (hh + 1) * tq)
        q = q_ref[:, hh * HEAD_DIM:(hh + 1) * HEAD_DIM].astype(F32) * ATT_SCALE
        qa_ref[rows, 0:HEAD_DIM] = q.astype(BF16)
        qa_ref[rows, HEAD_DIM:] = bias
    _flash_init(m_ref, l_ref, acc_ref)
    jd = (i * tq) // SL_TK

    def body(j, carry):
        off = pl.multiple_of(j * SL_TK, SL_TK)
        s = _dot_nt(qa_ref[...], k_ref[0, pl.ds(off, SL_TK), :])
        _flash_update(s, v_ref[pl.ds(off, SL_TK), :], m_ref, l_ref, acc_ref)
        return carry

    lax.fori_loop(0, jd, body, 0)
    off = pl.multiple_of(jd * SL_TK, SL_TK)
    s = _dot_nt(qa_ref[...], k_ref[0, pl.ds(off, SL_TK), :])
    kpos = off + lax.broadcasted_iota(jnp.int32, (1, SL_TK), 1)
    t_pos = i * tq + (lax.broadcasted_iota(jnp.int32, (B_HPG * tq, 1), 0) & (tq - 1))
    s = jnp.where(kpos <= t_pos, s, NEG_INF)
    _flash_update(s, v_ref[pl.ds(off, SL_TK), :], m_ref, l_ref, acc_ref)
    o = acc_ref[...] / l_ref[...]
    for hh in range(B_HPG):
        o_ref[:, hh * HEAD_DIM:(hh + 1) * HEAD_DIM] = o[hh * tq:(hh + 1) * tq].astype(o_ref.dtype)


def _nsa_slc(z_b, sel, k_aug):
    t = z_b.shape[0]
    nb = t // SLC_LEN
    gw = B_HPG * HEAD_DIM
    v_col0 = (B_QW + 3 * B_KVW) // HEAD_DIM
    return pl.pallas_call(
        _nsa_slc_kernel,
        grid=(B_KV_GROUPS, t // SL_TQ),
        in_specs=[pl.BlockSpec((SL_TQ, gw), lambda g, i: (i, g)),
                  pl.BlockSpec((1, SL_TQ, nb), lambda g, i: (g, i, 0)),
                  pl.BlockSpec((1, t, HEAD_DIM + nb), lambda g, i: (g, 0, 0)),
                  pl.BlockSpec((t, HEAD_DIM), lambda g, i: (0, v_col0 + g))],
        out_specs=pl.BlockSpec((SL_TQ, gw), lambda g, i: (i, g)),
        out_shape=jax.ShapeDtypeStruct((t, B_QW), BF16),
        scratch_shapes=[pltpu.VMEM((B_HPG * SL_TQ, HEAD_DIM + nb), BF16),
                        pltpu.VMEM((B_HPG * SL_TQ, 1), F32),
                        pltpu.VMEM((B_HPG * SL_TQ, 1), F32),
                        pltpu.VMEM((B_HPG * SL_TQ, HEAD_DIM), F32)],
        compiler_params=_cparams(("parallel", "arbitrary")),
    )(z_b, sel, k_aug, z_b)


WN_TQ = 128
WN_TK = 128


def _nsa_win_kernel(q_ref, k_ref, v_ref, oc_ref, os_ref, gate_ref, o_ref, qs_ref, m_ref, l_ref, acc_ref):
    i = pl.program_id(1)
    tq = WN_TQ
    for hh in range(B_HPG):
        q = q_ref[:, hh * HEAD_DIM:(hh + 1) * HEAD_DIM].astype(F32) * ATT_SCALE
        qs_ref[hh * tq:(hh + 1) * tq, :] = q.astype(BF16)
    _flash_init(m_ref, l_ref, acc_ref)
    t_pos = i * tq + (lax.broadcasted_iota(jnp.int32, (B_HPG * tq, 1), 0) & (tq - 1))
    j0 = jnp.maximum(i - WIN // WN_TK, 0)

    def body(j, carry):
        off = pl.multiple_of(j * WN_TK, WN_TK)
        s = _dot_nt(qs_ref[...], k_ref[pl.ds(off, WN_TK), :])
        kpos = off + lax.broadcasted_iota(jnp.int32, (1, WN_TK), 1)
        s = jnp.where((kpos <= t_pos) & (kpos > t_pos - WIN), s, NEG_INF)
        _flash_update(s, v_ref[pl.ds(off, WN_TK), :], m_ref, l_ref, acc_ref)
        return carry

    body(i, 0)
    lax.fori_loop(j0, i, body, 0)
    o_w = acc_ref[...] / l_ref[...]
    gates = _sigmoid(gate_ref[...])
    for hh in range(B_HPG):
        hs = slice(hh * HEAD_DIM, (hh + 1) * HEAD_DIM)
        g_c, g_s, g_w = (gates[:, 3 * hh + c:3 * hh + c + 1] for c in range(3))
        o = (g_c * oc_ref[:, hs].astype(F32) + g_s * os_ref[:, hs].astype(F32)
             + g_w * o_w[hh * tq:(hh + 1) * tq])
        o_ref[:, hs] = o.astype(o_ref.dtype)


def _nsa_win_combine(z_b, o_c, o_s, z_gate):
    t = z_b.shape[0]
    gw = B_HPG * HEAD_DIM
    k_col0 = (B_QW + 4 * B_KVW) // HEAD_DIM
    v_col0 = (B_QW + 5 * B_KVW) // HEAD_DIM
    rows = B_HPG * WN_TQ
    return pl.pallas_call(
        _nsa_win_kernel,
        grid=(B_KV_GROUPS, t // WN_TQ),
        in_specs=[pl.BlockSpec((WN_TQ, gw), lambda g, i: (i, g)),
                  pl.BlockSpec((t, HEAD_DIM), lambda g, i: (0, k_col0 + g)),
                  pl.BlockSpec((t, HEAD_DIM), lambda g, i: (0, v_col0 + g)),
                  pl.BlockSpec((WN_TQ, gw), lambda g, i: (i, g)),
                  pl.BlockSpec((WN_TQ, gw), lambda g, i: (i, g)),
                  pl.BlockSpec((WN_TQ, HEAD_DIM), lambda g, i: (i, g))],
        out_specs=pl.BlockSpec((WN_TQ, gw), lambda g, i: (i, g)),
        out_shape=jax.ShapeDtypeStruct((t, B_QW), BF16),
        scratch_shapes=[pltpu.VMEM((rows, HEAD_DIM), BF16),
                        pltpu.VMEM((rows, 1), F32),
                        pltpu.VMEM((rows, 1), F32),
                        pltpu.VMEM((rows, HEAD_DIM), F32)],
        compiler_params=_cparams(("parallel", "arbitrary")),
    )(z_b, z_b, z_b, o_c, o_s, z_gate)


def _nsa(z_b, z_gate, pos_k, w1k, w2k, pos_v, w1v, w2v):
    t = z_b.shape[0]
    ng = B_KV_GROUPS
    nb = t // SLC_LEN

    def segs(col0):
        z = z_b[:, col0:col0 + B_KVW].reshape(t, ng, HEAD_DIM).transpose(1, 0, 2)
        return z.reshape(ng, t // CMP_STRIDE, SEG_W)

    seg = jnp.stack([segs(B_QW), segs(B_QW + B_KVW)])
    pos = jnp.stack([pos_k, pos_v]).reshape(2, 1, CMP_LEN * HEAD_DIM).astype(BF16)
    w1 = jnp.stack([w1k, w1v]).astype(BF16)
    w2 = jnp.stack([w2k, w2v]).astype(BF16)
    cmp_kv = _nsa_compress(seg, pos, w1, w2)
    o_c, sel = _nsa_cmp_select(z_b, cmp_kv)
    k_s = z_b[:, B_QW + 2 * B_KVW:B_QW + 3 * B_KVW].reshape(t, ng, HEAD_DIM).transpose(1, 0, 2)
    onehot = (np.arange(t)[:, None] // SLC_LEN == np.arange(nb)[None, :]).astype(np.float32)
    k_aug = jnp.concatenate([k_s, jnp.broadcast_to(jnp.asarray(onehot, BF16), (ng, t, nb))], axis=-1)
    o_s = _nsa_slc(z_b, sel, k_aug)
    return _nsa_win_combine(z_b, o_c, o_s, z_gate)


KM_ROWS = 8


def _kmean_kernel(k_ref, o_ref):
    x = k_ref[...].astype(F32)
    o_ref[...] = jnp.mean(x.reshape(KM_ROWS, MOBA_BLOCK, x.shape[1]), axis=1)


def _moba_kmean(z_cd):
    t = z_cd.shape[0]
    nbm = t // MOBA_BLOCK
    return pl.pallas_call(
        _kmean_kernel,
        grid=(nbm // KM_ROWS,),
        in_specs=[pl.BlockSpec((KM_ROWS * MOBA_BLOCK, C_W), lambda i: (i, 1))],
        out_specs=pl.BlockSpec((KM_ROWS, C_W), lambda i: (i, 0)),
        out_shape=jax.ShapeDtypeStruct((nbm, C_W), F32),
        compiler_params=_cparams(("parallel",)),
    )(z_cd)


MB_T = MOBA_BLOCK
MB_NBP = 128


def _moba_kernel(q_ref, km_ref, k_ref, v_ref, o_ref, qa_ref, m_ref, l_ref, acc_ref):
    cur = pl.program_id(1)
    q = q_ref[...]
    km = km_ref[...]
    km_hi = km.astype(BF16)
    km_lo = (km - km_hi.astype(F32)).astype(BF16)
    gate = _dot_nt(q, km_hi) + _dot_nt(q, km_lo)
    jb = lax.broadcasted_iota(jnp.int32, (1, MB_NBP), 1)
    earlier = jb < cur
    sel = _topk_mask(jnp.where(earlier, gate, NEG_INF), jb.astype(F32), MOBA_TOPK)
    sel = jnp.where(earlier, sel, 0.0)
    qa_ref[:, 0:HEAD_DIM] = (q.astype(F32) * ATT_SCALE).astype(BF16)
    qa_ref[:, HEAD_DIM:] = ((sel - 1.0) * (-NEG_INF)).astype(BF16)
    _flash_init(m_ref, l_ref, acc_ref)

    def body(j, carry):
        off = pl.multiple_of(j * MB_T, MB_T)
        s = _dot_nt(qa_ref[...], k_ref[0, pl.ds(off, MB_T), :])
        _flash_update(s, v_ref[pl.ds(off, MB_T), :], m_ref, l_ref, acc_ref)
        return carry

    lax.fori_loop(0, cur, body, 0)
    off = pl.multiple_of(cur * MB_T, MB_T)
    s = _dot_nt(qa_ref[:, 0:HEAD_DIM], k_ref[0, pl.ds(off, MB_T), 0:HEAD_DIM])
    causal = (lax.broadcasted_iota(jnp.int32, (1, MB_T), 1)
              <= lax.broadcasted_iota(jnp.int32, (MB_T, 1), 0))
    s = jnp.where(causal, s, NEG_INF)
    _flash_update(s, v_ref[pl.ds(off, MB_T), :], m_ref, l_ref, acc_ref)
    o_ref[...] = (acc_ref[...] / l_ref[...]).astype(o_ref.dtype)


def _moba(z_cd):
    t = z_cd.shape[0]
    nbm = t // MOBA_BLOCK
    nh = C_HEADS
    k_mean = _moba_kmean(z_cd)
    k_mean = jnp.pad(k_mean, ((0, MB_NBP - nbm), (0, 0)))
    k_h = z_cd[:, C_W:2 * C_W].reshape(t, nh, HEAD_DIM).transpose(1, 0, 2)
    onehot = (np.arange(t)[:, None] // MOBA_BLOCK == np.arange(MB_NBP)[None, :]).astype(np.float32)
    k_aug = jnp.concatenate([k_h, jnp.broadcast_to(jnp.asarray(onehot, BF16), (nh, t, MB_NBP))], axis=-1)
    return pl.pallas_call(
        _moba_kernel,
        grid=(nh, t // MB_T),
        in_specs=[pl.BlockSpec((MB_T, HEAD_DIM), lambda h, i: (i, h)),
                  pl.BlockSpec((MB_NBP, HEAD_DIM), lambda h, i: (0, h)),
                  pl.BlockSpec((1, t, HEAD_DIM + MB_NBP), lambda h, i: (h, 0, 0)),
                  pl.BlockSpec((t, HEAD_DIM), lambda h, i: (0, 2 * nh + h))],
        out_specs=pl.BlockSpec((MB_T, HEAD_DIM), lambda h, i: (i, h)),
        out_shape=jax.ShapeDtypeStruct((t, C_W), BF16),
        scratch_shapes=[pltpu.VMEM((MB_T, HEAD_DIM + MB_NBP), BF16),
                        pltpu.VMEM((MB_T, 1), F32),
                        pltpu.VMEM((MB_T, 1), F32),
                        pltpu.VMEM((MB_T, HEAD_DIM), F32)],
        compiler_params=_cparams(("parallel", "arbitrary")),
    )(z_cd, k_mean, k_aug, z_cd)


DF_T = 256


def _diff_kernel(q_ref, k_ref, v_ref, lq1_ref, lk1_ref, lq2_ref, lk2_ref, nw_ref, o_ref,
                 qs_ref, m_ref, l_ref, acc_ref, *, lam_init):
    i = pl.program_id(1)
    for mp in range(2):
        q = q_ref[:, mp * D_DK:(mp + 1) * D_DK].astype(F32) * (D_DK ** -0.5)
        qs_ref[mp] = q.astype(BF16)
        _flash_init(m_ref.at[mp], l_ref.at[mp], acc_ref.at[mp])

    def step(off, causal):
        v = v_ref[pl.ds(off, DF_T), :]
        for mp in range(2):
            s = _dot_nt(qs_ref[mp], k_ref[pl.ds(off, DF_T), mp * D_DK:(mp + 1) * D_DK])
            if causal is not None:
                s = jnp.where(causal, s, NEG_INF)
            _flash_update(s, v, m_ref.at[mp], l_ref.at[mp], acc_ref.at[mp])

    def body(j, carry):
        step(pl.multiple_of(j * DF_T, DF_T), None)
        return carry

    lax.fori_loop(0, i, body, 0)
    causal = (lax.broadcasted_iota(jnp.int32, (1, DF_T), 1)
              <= lax.broadcasted_iota(jnp.int32, (DF_T, 1), 0))
    step(pl.multiple_of(i * DF_T, DF_T), causal)
    lam = (jnp.exp(jnp.sum(lq1_ref[...] * lk1_ref[...], axis=1, keepdims=True))
           - jnp.exp(jnp.sum(lq2_ref[...] * lk2_ref[...], axis=1, keepdims=True)) + lam_init)
    o = acc_ref[0] / l_ref[0] - lam * (acc_ref[1] / l_ref[1])
    o = o * lax.rsqrt(jnp.mean(o * o, axis=-1, keepdims=True) + RMS_EPS) * nw_ref[...]
    o_ref[...] = (o * (1.0 - lam_init)).astype(o_ref.dtype)


def _diff_attn(z_cd, lq1, lk1, lq2, lk2, subln_w, layer_idx):
    t = z_cd.shape[0]
    lam_init = 0.8 - 0.6 * math.exp(-0.3 * layer_idx)
    q_col0 = 3 * C_W // D_DV
    k_col0 = (3 * C_W + D_QW) // D_DV
    v_col0 = (3 * C_W + 2 * D_QW) // D_DV
    vec = pl.BlockSpec((1, D_DK), lambda h, i: (0, 0))
    return pl.pallas_call(
        functools.partial(_diff_kernel, lam_init=lam_init),
        grid=(D_HEADS, t // DF_T),
        in_specs=[pl.BlockSpec((DF_T, 2 * D_DK), lambda h, i: (i, q_col0 + h)),
                  pl.BlockSpec((t, 2 * D_DK), lambda h, i: (0, k_col0 + h)),
                  pl.BlockSpec((t, D_DV), lambda h, i: (0, v_col0 + h)),
                  vec, vec, vec, vec,
                  pl.BlockSpec((1, D_DV), lambda h, i: (0, 0))],
        out_specs=pl.BlockSpec((DF_T, D_DV), lambda h, i: (i, h)),
        out_shape=jax.ShapeDtypeStruct((t, D_VW), BF16),
        scratch_shapes=[pltpu.VMEM((2, DF_T, D_DK), BF16),
                        pltpu.VMEM((2, DF_T, 1), F32),
                        pltpu.VMEM((2, DF_T, 1), F32),
                        pltpu.VMEM((2, DF_T, D_DV), F32)],
        compiler_params=_cparams(("parallel", "arbitrary")),
    )(z_cd, z_cd, z_cd, lq1.reshape(1, D_DK), lk1.reshape(1, D_DK), lq2.reshape(1, D_DK),
      lk2.reshape(1, D_DK), subln_w.reshape(1, D_DV))


def _gate_weight(w_gate):
    d = w_gate.shape[0]
    per = 3 * B_HPG
    wg = w_gate.reshape(d, B_KV_GROUPS, per)
    wg = jnp.pad(wg, ((0, 0), (0, 0), (0, HEAD_DIM - per)))
    return wg.reshape(d, B_KV_GROUPS * HEAD_DIM)


def kernel(x, ln_w, ln_b, ffn_w_in, ffn_w_out, ab_w_in, ab_w_out, hgrn_lower_bounds, hgrn_norm_w,
           nsa_cmp_pos_k, nsa_cmp_k_w1, nsa_cmp_k_w2, nsa_cmp_pos_v, nsa_cmp_v_w1, nsa_cmp_v_w2,
           cd_w_in, cd_w_out, diff_lambda_q1, diff_lambda_k1, diff_lambda_q2, diff_lambda_k2, diff_subln_w):
    bsz, t, d = x.shape
    lb_all = jnp.cumsum(jax.nn.softmax(hgrn_lower_bounds.astype(F32), axis=0), axis=0)
    outs = []
    for bi in range(bsz):
        h = x[bi]
        h_bf = h.astype(BF16)
        for layer in range(DEPTH):
            if layer % 2 == 0:
                e = layer // 2
                w = ab_w_in[e]
                n_a = 4 * A_W
                n_b = B_QW + 6 * B_KVW
                z_a = _matmul(h_bf, w[:, :n_a].astype(BF16), F32)
                z_b = _matmul(h_bf, w[:, n_a:n_a + n_b].astype(BF16), BF16)
                z_g = _matmul(h_bf, _gate_weight(w[:, n_a + n_b:]).astype(BF16), F32)
                o_a = _hgrn2(z_a, lb_all[layer], hgrn_norm_w[e])
                o_b = _nsa(z_b, z_g, nsa_cmp_pos_k[e], nsa_cmp_k_w1[e], nsa_cmp_k_w2[e],
                           nsa_cmp_pos_v[e], nsa_cmp_v_w1[e], nsa_cmp_v_w2[e])
                mix_in = jnp.concatenate([o_a, o_b], axis=-1)
                w_out = ab_w_out[e]
            else:
                oi = layer // 2
                z_cd = _matmul(h_bf, cd_w_in[oi].astype(BF16), BF16)
                o_c = _moba(z_cd)
                o_d = _diff_attn(z_cd, diff_lambda_q1[oi], diff_lambda_k1[oi], diff_lambda_q2[oi],
                                 diff_lambda_k2[oi], diff_subln_w[oi], layer)
                mix_in = jnp.concatenate([o_c, o_d], axis=-1)
                w_out = cd_w_out[oi]
            h, h_bf = _mm_res_ln(mix_in, w_out.astype(BF16), h, ln_w[layer, 0], ln_b[layer, 0])
            act = _ffn_in(h_bf, ffn_w_in[layer].astype(BF16))
            h, h_bf = _mm_res_ln(act, ffn_w_out[layer].astype(BF16), h, ln_w[layer, 1], ln_b[layer, 1])
        outs.append(h)
    return jnp.stack(outs)
```

```python
import functools
import math

import numpy as np
import jax
import jax.numpy as jnp
from jax import lax
from jax.experimental import pallas as pl
from jax.experimental.pallas import tpu as pltpu

F32 = jnp.float32
BF16 = jnp.bfloat16

D_MODEL = 2048
DEPTH = 2
HEAD_DIM = 128
A_HEADS = 8
A_CHUNK = 64
B_HEADS = 8
B_KV_GROUPS = 2
B_HPG = B_HEADS // B_KV_GROUPS
CMP_LEN = 32
CMP_STRIDE = 16
SLC_LEN = 64
N_SLC = 16
WIN = 512
C_HEADS = 8
MOBA_BLOCK = 256
MOBA_TOPK = 3
D_HEADS = 4
D_DK = 128
D_DV = 2 * D_DK
D_FF = ((8 * D_MODEL + 3 * 256 - 1) // (3 * 256)) * 256

DEEPNORM_ALPHA = (2 * DEPTH) ** 0.25
NEG_INF = -1e30
FORCE_SCORE = 1e9
LN_EPS = 1e-5
RMS_EPS = 1e-6
ATT_SCALE = HEAD_DIM ** -0.5

A_W = A_HEADS * HEAD_DIM
B_QW = B_HEADS * HEAD_DIM
B_KVW = B_KV_GROUPS * HEAD_DIM
C_W = C_HEADS * HEAD_DIM
D_QW = D_HEADS * 2 * D_DK
D_VW = D_HEADS * D_DV

VMEM_LIMIT = 56 * 1024 * 1024


def _cparams(sem):
    return pltpu.CompilerParams(dimension_semantics=sem, vmem_limit_bytes=VMEM_LIMIT)


def _dot(a, b):
    return jnp.dot(a, b, preferred_element_type=F32)


def _dot_nt(a, b):
    return lax.dot_general(a, b, (((1,), (1,)), ((), ())), preferred_element_type=F32)


def _dot_tn(a, b):
    return lax.dot_general(a, b, (((0,), (0,)), ((), ())), preferred_element_type=F32)


def _sigmoid(x):
    return 1.0 / (1.0 + jnp.exp(-x))


def _silu(x):
    return x * _sigmoid(x)


def _mm_kernel(x_ref, w_ref, o_ref):
    o_ref[...] = _dot(x_ref[...], w_ref[...]).astype(o_ref.dtype)


def _matmul(x, w, out_dtype, tm=1024, tn=512):
    m, k = x.shape
    n = w.shape[1]
    tn = min(tn, n)
    return pl.pallas_call(
        _mm_kernel,
        grid=(m // tm, n // tn),
        in_specs=[pl.BlockSpec((tm, k), lambda i, j: (i, 0)),
                  pl.BlockSpec((k, tn), lambda i, j: (0, j))],
        out_specs=pl.BlockSpec((tm, tn), lambda i, j: (i, j)),
        out_shape=jax.ShapeDtypeStruct((m, n), out_dtype),
        compiler_params=_cparams(("parallel", "parallel")),
    )(x, w)


def _ffn_in_kernel(x_ref, wg_ref, wu_ref, o_ref):
    x = x_ref[...]
    g = _dot(x, wg_ref[...])
    u = _dot(x, wu_ref[...])
    o_ref[...] = (_silu(g) * u).astype(o_ref.dtype)


def _ffn_in(x, w_in, tm=1024, tn=512):
    m, k = x.shape
    dff = w_in.shape[1] // 2
    nj = dff // tn
    return pl.pallas_call(
        _ffn_in_kernel,
        grid=(m // tm, nj),
        in_specs=[pl.BlockSpec((tm, k), lambda i, j: (i, 0)),
                  pl.BlockSpec((k, tn), lambda i, j: (0, j)),
                  pl.BlockSpec((k, tn), lambda i, j: (0, j + nj))],
        out_specs=pl.BlockSpec((tm, tn), lambda i, j: (i, j)),
        out_shape=jax.ShapeDtypeStruct((m, dff), BF16),
        compiler_params=_cparams(("parallel", "parallel")),
    )(x, w_in, w_in)


def _mm_res_ln_kernel(a_ref, w_ref, h_ref, lnw_ref, lnb_ref, o_ref, obf_ref, acc_ref, *, nk):
    k = pl.program_id(1)

    @pl.when(k == 0)
    def _():
        acc_ref[...] = jnp.zeros_like(acc_ref)

    acc_ref[...] += _dot(a_ref[...], w_ref[...])

    @pl.when(k == nk - 1)
    def _():
        y = DEEPNORM_ALPHA * h_ref[...] + acc_ref[...]
        mu = jnp.mean(y, axis=-1, keepdims=True)
        d = y - mu
        var = jnp.mean(d * d, axis=-1, keepdims=True)
        out = d * lax.rsqrt(var + LN_EPS) * lnw_ref[...] + lnb_ref[...]
        o_ref[...] = out
        obf_ref[...] = out.astype(BF16)


def _mm_res_ln(a, w, h, ln_w, ln_b, tm=512, tk=512):
    m, kdim = a.shape
    n = w.shape[1]
    nk = kdim // tk
    return pl.pallas_call(
        functools.partial(_mm_res_ln_kernel, nk=nk),
        grid=(m // tm, nk),
        in_specs=[pl.BlockSpec((tm, tk), lambda i, k: (i, k)),
                  pl.BlockSpec((tk, n), lambda i, k: (k, 0)),
                  pl.BlockSpec((tm, n), lambda i, k: (i, 0)),
                  pl.BlockSpec((1, n), lambda i, k: (0, 0)),
                  pl.BlockSpec((1, n), lambda i, k: (0, 0))],
        out_specs=[pl.BlockSpec((tm, n), lambda i, k: (i, 0)),
                   pl.BlockSpec((tm, n), lambda i, k: (i, 0))],
        out_shape=[jax.ShapeDtypeStruct((m, n), F32), jax.ShapeDtypeStruct((m, n), BF16)],
        scratch_shapes=[pltpu.VMEM((tm, n), F32)],
        compiler_params=_cparams(("parallel", "arbitrary")),
    )(a, w, h, ln_w.reshape(1, n), ln_b.reshape(1, n))


LOG2E = math.log2(math.e)
Q_SCALE = ATT_SCALE * LOG2E


def _flash_init(m_ref, l_ref, acc_ref):
    m_ref[...] = jnp.full(m_ref.shape, -jnp.inf, F32)
    if l_ref is not None:
        l_ref[...] = jnp.zeros(l_ref.shape, F32)
    acc_ref[...] = jnp.zeros(acc_ref.shape, F32)


def _flash_update(s, v, m_ref, l_ref, acc_ref):
    m_prev = m_ref[...]
    m_new = jnp.maximum(m_prev, jnp.max(s, axis=1, keepdims=True))
    alpha = jnp.exp2(m_prev - m_new)
    p = jnp.exp2(s - m_new)
    if l_ref is not None:
        l_ref[...] = alpha * l_ref[...] + jnp.sum(p, axis=1, keepdims=True)
    acc_ref[...] = alpha * acc_ref[...] + _dot(p.astype(BF16), v)
    m_ref[...] = m_new


def _flash_sweep(n_full, qk, process):
    qk(0, 0)

    def pair(jj, carry):
        j = 2 * jj
        qk(1, j + 1)
        process(0, j, False)
        qk(0, j + 2)
        process(1, j + 1, False)
        return carry

    lax.fori_loop(0, n_full // 2, pair, 0)
    qk(1, n_full + 1)
    process(0, n_full, True)
    process(1, n_full + 1, True)


def _ones_column(v):
    pad = jnp.zeros(v.shape[:-1] + (v.shape[-1] - 1,), v.dtype)
    return jnp.concatenate([v, jnp.ones(v.shape[:-1] + (1,), v.dtype), pad], axis=-1)


def _resident(block_shape, index_map):
    return pl.BlockSpec(block_shape, index_map, pipeline_mode=pl.Buffered(1))


HG_TILE = 512


def _hgrn_kernel(q_ref, f_ref, i_ref, g_ref, lb_ref, nw_ref, o_ref, st_ref):
    @pl.when(pl.program_id(1) == 0)
    def _():
        st_ref[...] = jnp.zeros_like(st_ref)

    lb = lb_ref[...]
    f = lb + (1.0 - lb) * _sigmoid(f_ref[...])
    logf = jnp.log(f)
    kk = 1.0 - f
    qf = _silu(q_ref[...])
    r64 = lax.broadcasted_iota(jnp.int32, logf.shape, 0) & (A_CHUNK - 1)
    b = logf
    step = 1
    while step < A_CHUNK:
        b = b + jnp.where(r64 >= step, pltpu.roll(b, step, 0), 0.0)
        step *= 2
    causal = (lax.broadcasted_iota(jnp.int32, (A_CHUNK, A_CHUNK), 0)
              >= lax.broadcasted_iota(jnp.int32, (A_CHUNK, A_CHUNK), 1))
    nw = nw_ref[...]
    for c in range(HG_TILE // A_CHUNK):
        sl = slice(c * A_CHUNK, (c + 1) * A_CHUNK)
        bc = b[sl]
        b_last = bc[A_CHUNK - 1:A_CHUNK, :]
        q_t = (qf[sl] * jnp.exp(bc)).astype(BF16)
        k_t = (kk[sl] * jnp.exp(-bc)).astype(BF16)
        vc = i_ref[sl, :].astype(BF16)
        att = jnp.where(causal, _dot_nt(q_t, k_t), 0.0)
        st = st_ref[...]
        o = _dot(att.astype(BF16), vc) + _dot_nt(q_t, st.astype(BF16))
        kdec = (kk[sl] * jnp.exp(b_last - bc)).astype(BF16)
        st_ref[...] = st * jnp.exp(b_last) + _dot_tn(vc, kdec)
        o = o * lax.rsqrt(jnp.mean(o * o, axis=-1, keepdims=True) + RMS_EPS) * nw
        o_ref[sl, :] = (o * _silu(g_ref[sl, :])).astype(o_ref.dtype)


def _hgrn2(z_a, lb, norm_w):
    t = z_a.shape[0]
    nh = A_HEADS
    col = lambda base: (lambda h, i: (i, base + h))
    return pl.pallas_call(
        _hgrn_kernel,
        grid=(nh, t // HG_TILE),
        in_specs=[pl.BlockSpec((HG_TILE, HEAD_DIM), col(0)),
                  pl.BlockSpec((HG_TILE, HEAD_DIM), col(nh)),
                  pl.BlockSpec((HG_TILE, HEAD_DIM), col(2 * nh)),
                  pl.BlockSpec((HG_TILE, HEAD_DIM), col(3 * nh)),
                  pl.BlockSpec((1, HEAD_DIM), lambda h, i: (0, h)),
                  pl.BlockSpec((1, HEAD_DIM), lambda h, i: (0, 0))],
        out_specs=pl.BlockSpec((HG_TILE, HEAD_DIM), lambda h, i: (i, h)),
        out_shape=jax.ShapeDtypeStruct((t, A_W), BF16),
        scratch_shapes=[pltpu.VMEM((HEAD_DIM, HEAD_DIM), F32)],
        compiler_params=_cparams(("parallel", "arbitrary")),
    )(z_a, z_a, z_a, z_a, lb.reshape(1, A_W), norm_w.reshape(1, HEAD_DIM))


SEG_W = CMP_STRIDE * HEAD_DIM


def _nsa_compress_kernel(seg_ref, pos_ref, w1_ref, w2_ref, o_ref):
    seg = seg_ref[0, 0]
    nseg = seg.shape[0]
    a = _dot(seg, w1_ref[0, :SEG_W, :])
    b = _dot(seg, w1_ref[0, SEG_W:, :])
    b_next = pltpu.roll(b, nseg - 1, 0)
    pos = jnp.broadcast_to(pos_ref[0], (8, CMP_LEN * HEAD_DIM))
    c = _dot(pos, w1_ref[0])[0:1, :]
    hid = _silu(a + b_next + c)
    o_ref[0, 0] = _dot(hid.astype(BF16), w2_ref[0]).astype(o_ref.dtype)


def _nsa_compress(seg, pos, w1, w2):
    _, ng, nseg, _ = seg.shape
    return pl.pallas_call(
        _nsa_compress_kernel,
        grid=(2, ng),
        in_specs=[pl.BlockSpec((1, 1, nseg, SEG_W), lambda a, g: (a, g, 0, 0)),
                  pl.BlockSpec((1, 1, CMP_LEN * HEAD_DIM), lambda a, g: (a, 0, 0)),
                  pl.BlockSpec((1, CMP_LEN * HEAD_DIM, HEAD_DIM), lambda a, g: (a, 0, 0)),
                  pl.BlockSpec((1, HEAD_DIM, HEAD_DIM), lambda a, g: (a, 0, 0))],
        out_specs=pl.BlockSpec((1, 1, nseg, HEAD_DIM), lambda a, g: (a, g, 0, 0)),
        out_shape=jax.ShapeDtypeStruct((2, ng, nseg, HEAD_DIM), BF16),
        compiler_params=_cparams(("parallel", "parallel")),
    )(seg, pos, w1, w2)


CS_TQ = 256
SL_NBP = 128


def _split3(x):
    hi = x.astype(BF16)
    r = x - hi.astype(F32)
    mid = r.astype(BF16)
    lo = (r - mid.astype(F32)).astype(BF16)
    return hi, mid, lo


def _topk_mask(score, lane, k):
    sel = jnp.zeros(score.shape, F32)
    width = float(score.shape[1])
    for _ in range(k):
        mx = jnp.max(score, axis=1, keepdims=True)
        idx = jnp.min(jnp.where(score == mx, lane, width), axis=1, keepdims=True)
        hit = lane == idx
        sel = jnp.where(hit, 1.0, sel)
        score = jnp.where(hit, -jnp.inf, score)
    return sel


def _nsa_cmp_kernel(q_ref, kc_ref, vc_ref, map_ref, o_ref, sel_ref, *, n_sel):
    i = pl.program_id(1)
    tq = q_ref.shape[0]
    ncmp = kc_ref.shape[2]
    nbp = map_ref.shape[1]
    t_pos = i * tq + lax.broadcasted_iota(jnp.int32, (tq, 1), 0)
    cmp_end = lax.broadcasted_iota(jnp.int32, (1, ncmp), 1) * CMP_STRIDE + (CMP_LEN - 1)
    mask = cmp_end <= t_pos
    kc = kc_ref[0, 0]
    vc = vc_ref[0, 0]
    psum = jnp.zeros((tq, ncmp), F32)
    for hh in range(B_HPG):
        hs = slice(hh * HEAD_DIM, (hh + 1) * HEAD_DIM)
        s = _dot_nt(q_ref[:, hs], kc) * ATT_SCALE
        s = jnp.where(mask, s, NEG_INF)
        e = jnp.exp(s - jnp.max(s, axis=1, keepdims=True))
        p = jnp.where(mask, e / jnp.sum(e, axis=1, keepdims=True), 0.0)
        o_ref[:, hs] = _dot(p.astype(BF16), vc).astype(o_ref.dtype)
        psum = psum + p
    hi, mid, lo = _split3(psum)
    cmap = map_ref[...]
    p_slc = _dot(hi, cmap) + _dot(mid, cmap) + _dot(lo, cmap)
    jb = lax.broadcasted_iota(jnp.int32, (1, nbp), 1)
    cur = t_pos // SLC_LEN
    valid = jb <= cur
    forced = (jb == 0) | (jb == cur) | (jb == cur - 1)
    score = jnp.where(valid, jnp.where(forced, FORCE_SCORE, p_slc), NEG_INF)
    sel = _topk_mask(score, jb.astype(F32), n_sel)
    sel_ref[0] = jnp.where(valid, sel, 0.0).astype(sel_ref.dtype)


def _cmp_to_slc_matrix(ncmp_pad, n_cmp, nb, nbp):
    ratio = SLC_LEN // CMP_STRIDE
    n_over = CMP_LEN // CMP_STRIDE
    mat = np.zeros((ncmp_pad, nbp), np.float32)
    for j in range(nb):
        for m in range(ratio):
            for n in range(n_over):
                c = ratio * j + m - n
                if 0 <= c < n_cmp:
                    mat[c, j] += 1.0
    return mat


def _nsa_cmp_select(z_b, cmp_kv):
    t = z_b.shape[0]
    ng = B_KV_GROUPS
    nseg = cmp_kv.shape[2]
    nb = t // SLC_LEN
    n_cmp = (t - CMP_LEN) // CMP_STRIDE + 1
    nbp = -(-nb // SL_NBP) * SL_NBP
    cmap = jnp.asarray(_cmp_to_slc_matrix(nseg, n_cmp, nb, nbp), BF16)
    gw = B_HPG * HEAD_DIM
    return pl.pallas_call(
        functools.partial(_nsa_cmp_kernel, n_sel=min(N_SLC, nb)),
        grid=(ng, t // CS_TQ),
        in_specs=[pl.BlockSpec((CS_TQ, gw), lambda g, i: (i, g)),
                  pl.BlockSpec((1, 1, nseg, HEAD_DIM), lambda g, i: (0, g, 0, 0)),
                  pl.BlockSpec((1, 1, nseg, HEAD_DIM), lambda g, i: (1, g, 0, 0)),
                  pl.BlockSpec((nseg, nbp), lambda g, i: (0, 0))],
        out_specs=[pl.BlockSpec((CS_TQ, gw), lambda g, i: (i, g)),
                   pl.BlockSpec((1, CS_TQ, nbp), lambda g, i: (g, i, 0))],
        out_shape=[jax.ShapeDtypeStruct((t, B_QW), BF16),
                   jax.ShapeDtypeStruct((ng, t, nbp), BF16)],
        compiler_params=_cparams(("parallel", "parallel")),
    )(z_b, cmp_kv, cmp_kv, cmap)


SL_TQ = 512
SL_TK = SL_TQ // 2
SL_PHASE_TILES = SL_NBP * SLC_LEN // SL_TK


def _nsa_slc_kernel(q_ref, sel_ref, k_ref, v_ref, o_ref, qa_ref, s_ref, m_ref, acc_ref, *, n_phase):
    i = pl.program_id(1)
    tq = SL_TQ
    bias = ((sel_ref[0].astype(F32) - 1.0) * (-NEG_INF)).astype(BF16)
    for hh in range(B_HPG):
        rows = slice(hh * tq, (hh + 1) * tq)
        q = (q_ref[:, hh * HEAD_DIM:(hh + 1) * HEAD_DIM].astype(F32) * Q_SCALE).astype(BF16)
        for ph in range(n_phase):
            qa_ref[ph, rows, 0:HEAD_DIM] = q
            qa_ref[ph, rows, HEAD_DIM:] = bias[:, ph * SL_NBP:(ph + 1) * SL_NBP]
    _flash_init(m_ref, None, acc_ref)
    t_pos = i * tq + (lax.broadcasted_iota(jnp.int32, (B_HPG * tq, 1), 0) & (tq - 1))

    def qk(slot, j):
        off = pl.multiple_of(j * SL_TK, SL_TK)
        qa = qa_ref[0] if n_phase == 1 else qa_ref[j // SL_PHASE_TILES]
        s_ref[slot] = _dot_nt(qa, k_ref[0, pl.ds(off, SL_TK), :])

    def process(slot, j, causal):
        off = pl.multiple_of(j * SL_TK, SL_TK)
        s = s_ref[slot]
        if causal:
            kpos = off + lax.broadcasted_iota(jnp.int32, (1, SL_TK), 1)
            s = jnp.where(kpos <= t_pos, s, NEG_INF)
        _flash_update(s, v_ref[0, pl.ds(off, SL_TK), :], m_ref, None, acc_ref)

    _flash_sweep(i * (tq // SL_TK), qk, process)
    acc = acc_ref[...]
    o = acc[:, 0:HEAD_DIM] / acc[:, HEAD_DIM:HEAD_DIM + 1]
    for hh in range(B_HPG):
        o_ref[:, hh * HEAD_DIM:(hh + 1) * HEAD_DIM] = o[hh * tq:(hh + 1) * tq].astype(o_ref.dtype)


def _nsa_slc(z_b, sel, k_aug, v_aug):
    t = z_b.shape[0]
    nbp = sel.shape[2]
    n_phase = nbp // SL_NBP
    gw = B_HPG * HEAD_DIM
    rows = B_HPG * SL_TQ
    return pl.pallas_call(
        functools.partial(_nsa_slc_kernel, n_phase=n_phase),
        grid=(B_KV_GROUPS, t // SL_TQ),
        in_specs=[pl.BlockSpec((SL_TQ, gw), lambda g, i: (i, g)),
                  pl.BlockSpec((1, SL_TQ, nbp), lambda g, i: (g, i, 0)),
                  _resident((1, t, HEAD_DIM + SL_NBP), lambda g, i: (g, 0, 0)),
                  _resident((1, t, 2 * HEAD_DIM), lambda g, i: (g, 0, 0))],
        out_specs=pl.BlockSpec((SL_TQ, gw), lambda g, i: (i, g)),
        out_shape=jax.ShapeDtypeStruct((t, B_QW), BF16),
        scratch_shapes=[pltpu.VMEM((n_phase, rows, HEAD_DIM + SL_NBP), BF16),
                        pltpu.VMEM((2, rows, SL_TK), F32),
                        pltpu.VMEM((rows, 1), F32),
                        pltpu.VMEM((rows, 2 * HEAD_DIM), F32)],
        compiler_params=_cparams(("parallel", "arbitrary")),
    )(z_b, sel, k_aug, v_aug)


WN_TQ = 128
WN_TK = 128


def _nsa_win_kernel(q_ref, k_ref, v_ref, oc_ref, os_ref, gate_ref, o_ref, qs_ref, m_ref, l_ref, acc_ref):
    i = pl.program_id(1)
    tq = WN_TQ
    for hh in range(B_HPG):
        q = q_ref[:, hh * HEAD_DIM:(hh + 1) * HEAD_DIM].astype(F32) * Q_SCALE
        qs_ref[hh * tq:(hh + 1) * tq, :] = q.astype(BF16)
    _flash_init(m_ref, l_ref, acc_ref)
    t_pos = i * tq + (lax.broadcasted_iota(jnp.int32, (B_HPG * tq, 1), 0) & (tq - 1))
    j0 = jnp.maximum(i - WIN // WN_TK, 0)

    def body(j, carry):
        off = pl.multiple_of(j * WN_TK, WN_TK)
        s = _dot_nt(qs_ref[...], k_ref[pl.ds(off, WN_TK), :])
        kpos = off + lax.broadcasted_iota(jnp.int32, (1, WN_TK), 1)
        s = jnp.where((kpos <= t_pos) & (kpos > t_pos - WIN), s, NEG_INF)
        _flash_update(s, v_ref[pl.ds(off, WN_TK), :], m_ref, l_ref, acc_ref)
        return carry

    body(i, 0)
    lax.fori_loop(j0, i, body, 0)
    o_w = acc_ref[...] / l_ref[...]
    gates = _sigmoid(gate_ref[...])
    for hh in range(B_HPG):
        hs = slice(hh * HEAD_DIM, (hh + 1) * HEAD_DIM)
        g_c, g_s, g_w = (gates[:, 3 * hh + c:3 * hh + c + 1] for c in range(3))
        o = (g_c * oc_ref[:, hs].astype(F32) + g_s * os_ref[:, hs].astype(F32)
             + g_w * o_w[hh * tq:(hh + 1) * tq])
        o_ref[:, hs] = o.astype(o_ref.dtype)


def _nsa_win_combine(z_b, o_c, o_s, z_gate):
    t = z_b.shape[0]
    gw = B_HPG * HEAD_DIM
    k_col0 = (B_QW + 4 * B_KVW) // HEAD_DIM
    v_col0 = (B_QW + 5 * B_KVW) // HEAD_DIM
    rows = B_HPG * WN_TQ
    return pl.pallas_call(
        _nsa_win_kernel,
        grid=(B_KV_GROUPS, t // WN_TQ),
        in_specs=[pl.BlockSpec((WN_TQ, gw), lambda g, i: (i, g)),
                  pl.BlockSpec((t, HEAD_DIM), lambda g, i: (0, k_col0 + g)),
                  pl.BlockSpec((t, HEAD_DIM), lambda g, i: (0, v_col0 + g)),
                  pl.BlockSpec((WN_TQ, gw), lambda g, i: (i, g)),
                  pl.BlockSpec((WN_TQ, gw), lambda g, i: (i, g)),
                  pl.BlockSpec((WN_TQ, HEAD_DIM), lambda g, i: (i, g))],
        out_specs=pl.BlockSpec((WN_TQ, gw), lambda g, i: (i, g)),
        out_shape=jax.ShapeDtypeStruct((t, B_QW), BF16),
        scratch_shapes=[pltpu.VMEM((rows, HEAD_DIM), BF16),
                        pltpu.VMEM((rows, 1), F32),
                        pltpu.VMEM((rows, 1), F32),
                        pltpu.VMEM((rows, HEAD_DIM), F32)],
        compiler_params=_cparams(("parallel", "arbitrary")),
    )(z_b, z_b, z_b, o_c, o_s, z_gate)


def _nsa(z_b, z_gate, pos_k, w1k, w2k, pos_v, w1v, w2v):
    t = z_b.shape[0]
    ng = B_KV_GROUPS

    def segs(col0):
        z = z_b[:, col0:col0 + B_KVW].reshape(t, ng, HEAD_DIM).transpose(1, 0, 2)
        return z.reshape(ng, t // CMP_STRIDE, SEG_W)

    seg = jnp.stack([segs(B_QW), segs(B_QW + B_KVW)])
    pos = jnp.stack([pos_k, pos_v]).reshape(2, 1, CMP_LEN * HEAD_DIM).astype(BF16)
    w1 = jnp.stack([w1k, w1v]).astype(BF16)
    w2 = jnp.stack([w2k, w2v]).astype(BF16)
    cmp_kv = _nsa_compress(seg, pos, w1, w2)
    o_c, sel = _nsa_cmp_select(z_b, cmp_kv)
    k_s = z_b[:, B_QW + 2 * B_KVW:B_QW + 3 * B_KVW].reshape(t, ng, HEAD_DIM).transpose(1, 0, 2)
    onehot = ((np.arange(t)[:, None] // SLC_LEN) % SL_NBP == np.arange(SL_NBP)[None, :]).astype(np.float32)
    k_aug = jnp.concatenate([k_s, jnp.broadcast_to(jnp.asarray(onehot, BF16), (ng, t, SL_NBP))], axis=-1)
    v_s = z_b[:, B_QW + 3 * B_KVW:B_QW + 4 * B_KVW].reshape(t, ng, HEAD_DIM).transpose(1, 0, 2)
    o_s = _nsa_slc(z_b, sel, k_aug, _ones_column(v_s))
    return _nsa_win_combine(z_b, o_c, o_s, z_gate)


KM_ROWS = 8


def _kmean_kernel(k_ref, o_ref):
    x = k_ref[...].astype(F32)
    o_ref[...] = jnp.mean(x.reshape(KM_ROWS, MOBA_BLOCK, x.shape[1]), axis=1)


def _moba_kmean(z_cd):
    t = z_cd.shape[0]
    nbm = t // MOBA_BLOCK
    return pl.pallas_call(
        _kmean_kernel,
        grid=(nbm // KM_ROWS,),
        in_specs=[pl.BlockSpec((KM_ROWS * MOBA_BLOCK, C_W), lambda i: (i, 1))],
        out_specs=pl.BlockSpec((KM_ROWS, C_W), lambda i: (i, 0)),
        out_shape=jax.ShapeDtypeStruct((nbm, C_W), F32),
        compiler_params=_cparams(("parallel",)),
    )(z_cd)


MB_TQ = 1024
MB_TK = MB_TQ // 2
MB_NBP = 128


def _moba_kernel(q_ref, km_ref, k_ref, v_ref, o_ref, qa_ref, s_ref, m_ref, acc_ref):
    i = pl.program_id(1)
    tq = MB_TQ
    q = q_ref[...]
    km = km_ref[...]
    km_hi = km.astype(BF16)
    km_lo = (km - km_hi.astype(F32)).astype(BF16)
    gate = _dot_nt(q, km_hi) + _dot_nt(q, km_lo)
    jb = lax.broadcasted_iota(jnp.int32, (1, MB_NBP), 1)
    t_pos = i * tq + lax.broadcasted_iota(jnp.int32, (tq, 1), 0)
    cur = t_pos // MOBA_BLOCK
    earlier = jb < cur
    sel = _topk_mask(jnp.where(earlier, gate, NEG_INF), jb.astype(F32), MOBA_TOPK)
    sel = jnp.where(jb == cur, 1.0, jnp.where(earlier, sel, 0.0))
    qa_ref[:, 0:HEAD_DIM] = (q.astype(F32) * Q_SCALE).astype(BF16)
    qa_ref[:, HEAD_DIM:] = ((sel - 1.0) * (-NEG_INF)).astype(BF16)
    _flash_init(m_ref, None, acc_ref)

    def qk(slot, j):
        off = pl.multiple_of(j * MB_TK, MB_TK)
        s_ref[slot] = _dot_nt(qa_ref[...], k_ref[0, pl.ds(off, MB_TK), :])

    def process(slot, j, causal):
        off = pl.multiple_of(j * MB_TK, MB_TK)
        s = s_ref[slot]
        if causal:
            kpos = off + lax.broadcasted_iota(jnp.int32, (1, MB_TK), 1)
            s = jnp.where(kpos <= t_pos, s, NEG_INF)
        _flash_update(s, v_ref[0, pl.ds(off, MB_TK), :], m_ref, None, acc_ref)

    _flash_sweep(i * (tq // MB_TK), qk, process)
    acc = acc_ref[...]
    o_ref[...] = (acc[:, 0:HEAD_DIM] / acc[:, HEAD_DIM:HEAD_DIM + 1]).astype(o_ref.dtype)


def _moba(z_cd):
    t = z_cd.shape[0]
    nbm = t // MOBA_BLOCK
    nh = C_HEADS
    k_mean = _moba_kmean(z_cd)
    k_mean = jnp.pad(k_mean, ((0, MB_NBP - nbm), (0, 0)))
    k_h = z_cd[:, C_W:2 * C_W].reshape(t, nh, HEAD_DIM).transpose(1, 0, 2)
    v_h = z_cd[:, 2 * C_W:3 * C_W].reshape(t, nh, HEAD_DIM).transpose(1, 0, 2)
    onehot = (np.arange(t)[:, None] // MOBA_BLOCK == np.arange(MB_NBP)[None, :]).astype(np.float32)
    k_aug = jnp.concatenate([k_h, jnp.broadcast_to(jnp.asarray(onehot, BF16), (nh, t, MB_NBP))], axis=-1)
    return pl.pallas_call(
        _moba_kernel,
        grid=(nh, t // MB_TQ),
        in_specs=[pl.BlockSpec((MB_TQ, HEAD_DIM), lambda h, i: (i, h)),
                  pl.BlockSpec((MB_NBP, HEAD_DIM), lambda h, i: (0, h)),
                  _resident((1, t, HEAD_DIM + MB_NBP), lambda h, i: (h, 0, 0)),
                  _resident((1, t, 2 * HEAD_DIM), lambda h, i: (h, 0, 0))],
        out_specs=pl.BlockSpec((MB_TQ, HEAD_DIM), lambda h, i: (i, h)),
        out_shape=jax.ShapeDtypeStruct((t, C_W), BF16),
        scratch_shapes=[pltpu.VMEM((MB_TQ, HEAD_DIM + MB_NBP), BF16),
                        pltpu.VMEM((2, MB_TQ, MB_TK), F32),
                        pltpu.VMEM((MB_TQ, 1), F32),
                        pltpu.VMEM((MB_TQ, 2 * HEAD_DIM), F32)],
        compiler_params=_cparams(("parallel", "arbitrary")),
    )(z_cd, k_mean, k_aug, _ones_column(v_h))


DF_TQ = 1024
DF_TK = DF_TQ // 2


def _diff_kernel(q_ref, k_ref, v_ref, lq1_ref, lk1_ref, lq2_ref, lk2_ref, nw_ref, o_ref,
                 qs_ref, s_ref, m_ref, l_ref, acc_ref, *, lam_init):
    i = pl.program_id(1)
    tq = DF_TQ
    for mp in range(2):
        q = q_ref[:, mp * D_DK:(mp + 1) * D_DK].astype(F32) * (D_DK ** -0.5 * LOG2E)
        qs_ref[mp] = q.astype(BF16)
        _flash_init(m_ref.at[mp], l_ref.at[mp], acc_ref.at[mp])
    t_pos = i * tq + lax.broadcasted_iota(jnp.int32, (tq, 1), 0)

    def qk(slot, j):
        off = pl.multiple_of(j * DF_TK, DF_TK)
        for mp in range(2):
            s_ref[slot, mp] = _dot_nt(qs_ref[mp], k_ref[pl.ds(off, DF_TK), mp * D_DK:(mp + 1) * D_DK])

    def process(slot, j, causal):
        off = pl.multiple_of(j * DF_TK, DF_TK)
        v = v_ref[pl.ds(off, DF_TK), :]
        for mp in range(2):
            s = s_ref[slot, mp]
            if causal:
                kpos = off + lax.broadcasted_iota(jnp.int32, (1, DF_TK), 1)
                s = jnp.where(kpos <= t_pos, s, NEG_INF)
            _flash_update(s, v, m_ref.at[mp], l_ref.at[mp], acc_ref.at[mp])

    _flash_sweep(i * (tq // DF_TK), qk, process)
    lam = (jnp.exp(jnp.sum(lq1_ref[...] * lk1_ref[...], axis=1, keepdims=True))
           - jnp.exp(jnp.sum(lq2_ref[...] * lk2_ref[...], axis=1, keepdims=True)) + lam_init)
    o = acc_ref[0] / l_ref[0] - lam * (acc_ref[1] / l_ref[1])
    o = o * lax.rsqrt(jnp.mean(o * o, axis=-1, keepdims=True) + RMS_EPS) * nw_ref[...]
    o_ref[...] = (o * (1.0 - lam_init)).astype(o_ref.dtype)


def _diff_attn(z_cd, lq1, lk1, lq2, lk2, subln_w, layer_idx):
    t = z_cd.shape[0]
    lam_init = 0.8 - 0.6 * math.exp(-0.3 * layer_idx)
    q_col0 = 3 * C_W // D_DV
    k_col0 = (3 * C_W + D_QW) // D_DV
    v_col0 = (3 * C_W + 2 * D_QW) // D_DV
    vec = pl.BlockSpec((1, D_DK), lambda h, i: (0, 0))
    return pl.pallas_call(
        functools.partial(_diff_kernel, lam_init=lam_init),
        grid=(D_HEADS, t // DF_TQ),
        in_specs=[pl.BlockSpec((DF_TQ, 2 * D_DK), lambda h, i: (i, q_col0 + h)),
                  _resident((t, 2 * D_DK), lambda h, i: (0, k_col0 + h)),
                  _resident((t, D_DV), lambda h, i: (0, v_col0 + h)),
                  vec, vec, vec, vec,
                  pl.BlockSpec((1, D_DV), lambda h, i: (0, 0))],
        out_specs=pl.BlockSpec((DF_TQ, D_DV), lambda h, i: (i, h)),
        out_shape=jax.ShapeDtypeStruct((t, D_VW), BF16),
        scratch_shapes=[pltpu.VMEM((2, DF_TQ, D_DK), BF16),
                        pltpu.VMEM((2, 2, DF_TQ, DF_TK), F32),
                        pltpu.VMEM((2, DF_TQ, 1), F32),
                        pltpu.VMEM((2, DF_TQ, 1), F32),
                        pltpu.VMEM((2, DF_TQ, D_DV), F32)],
        compiler_params=_cparams(("parallel", "arbitrary")),
    )(z_cd, z_cd, z_cd, lq1.reshape(1, D_DK), lk1.reshape(1, D_DK), lq2.reshape(1, D_DK),
      lk2.reshape(1, D_DK), subln_w.reshape(1, D_DV))


def _gate_weight(w_gate):
    d = w_gate.shape[0]
    per = 3 * B_HPG
    wg = w_gate.reshape(d, B_KV_GROUPS, per)
    wg = jnp.pad(wg, ((0, 0), (0, 0), (0, HEAD_DIM - per)))
    return wg.reshape(d, B_KV_GROUPS * HEAD_DIM)


def kernel(x, ln_w, ln_b, ffn_w_in, ffn_w_out, ab_w_in, ab_w_out, hgrn_lower_bounds, hgrn_norm_w,
           nsa_cmp_pos_k, nsa_cmp_k_w1, nsa_cmp_k_w2, nsa_cmp_pos_v, nsa_cmp_v_w1, nsa_cmp_v_w2,
           cd_w_in, cd_w_out, diff_lambda_q1, diff_lambda_k1, diff_lambda_q2, diff_lambda_k2, diff_subln_w):
    bsz, t, d = x.shape
    lb_all = jnp.cumsum(jax.nn.softmax(hgrn_lower_bounds.astype(F32), axis=0), axis=0)
    outs = []
    for bi in range(bsz):
        h = x[bi]
        h_bf = h.astype(BF16)
        for layer in range(DEPTH):
            if layer % 2 == 0:
                e = layer // 2
                w = ab_w_in[e]
                n_a = 4 * A_W
                n_b = B_QW + 6 * B_KVW
                z_a = _matmul(h_bf, w[:, :n_a].astype(BF16), F32)
                z_b = _matmul(h_bf, w[:, n_a:n_a + n_b].astype(BF16), BF16)
                z_g = _matmul(h_bf, _gate_weight(w[:, n_a + n_b:]).astype(BF16), F32)
                o_a = _hgrn2(z_a, lb_all[layer], hgrn_norm_w[e])
                o_b = _nsa(z_b, z_g, nsa_cmp_pos_k[e], nsa_cmp_k_w1[e], nsa_cmp_k_w2[e],
                           nsa_cmp_pos_v[e], nsa_cmp_v_w1[e], nsa_cmp_v_w2[e])
                mix_in = jnp.concatenate([o_a, o_b], axis=-1)
                w_out = ab_w_out[e]
            else:
                oi = layer // 2
                z_cd = _matmul(h_bf, cd_w_in[oi].astype(BF16), BF16)
                o_c = _moba(z_cd)
                o_d = _diff_attn(z_cd, diff_lambda_q1[oi], diff_lambda_k1[oi], diff_lambda_q2[oi],
                                 diff_lambda_k2[oi], diff_subln_w[oi], layer)
                mix_in = jnp.concatenate([o_c, o_d], axis=-1)
                w_out = cd_w_out[oi]
            h, h_bf = _mm_res_ln(mix_in, w_out.astype(BF16), h, ln_w[layer, 0], ln_b[layer, 0])
            act = _ffn_in(h_bf, ffn_w_in[layer].astype(BF16))
            h, h_bf = _mm_res_ln(act, ffn_w_out[layer].astype(BF16), h, ln_w[layer, 1], ln_b[layer, 1])
        outs.append(h)
    return jnp.stack(outs)
```

```python
import functools
import math

import numpy as np
import jax
import jax.numpy as jnp
from jax import lax
from jax.experimental import pallas as pl
from jax.experimental.pallas import tpu as pltpu

F32 = jnp.float32
BF16 = jnp.bfloat16

D_MODEL = 2048
DEPTH = 2
HEAD_DIM = 128
A_HEADS = 8
A_CHUNK = 64
B_HEADS = 8
B_KV_GROUPS = 2
B_HPG = B_HEADS // B_KV_GROUPS
CMP_LEN = 32
CMP_STRIDE = 16
SLC_LEN = 64
N_SLC = 16
WIN = 512
C_HEADS = 8
MOBA_BLOCK = 256
MOBA_TOPK = 3
D_HEADS = 4
D_DK = 128
D_DV = 2 * D_DK
D_FF = ((8 * D_MODEL + 3 * 256 - 1) // (3 * 256)) * 256

DEEPNORM_ALPHA = (2 * DEPTH) ** 0.25
NEG_INF = -1e30
FORCE_SCORE = 1e9
LN_EPS = 1e-5
RMS_EPS = 1e-6
ATT_SCALE = HEAD_DIM ** -0.5

A_W = A_HEADS * HEAD_DIM
B_QW = B_HEADS * HEAD_DIM
B_KVW = B_KV_GROUPS * HEAD_DIM
C_W = C_HEADS * HEAD_DIM
D_QW = D_HEADS * 2 * D_DK
D_VW = D_HEADS * D_DV

VMEM_LIMIT = 56 * 1024 * 1024
MM_SMALL_WEIGHT_BYTES = 8 * 1024 * 1024


def _cparams(sem):
    return pltpu.CompilerParams(dimension_semantics=sem, vmem_limit_bytes=VMEM_LIMIT)


def _dot(a, b):
    return jnp.dot(a, b, preferred_element_type=F32)


def _dot_nt(a, b):
    return lax.dot_general(a, b, (((1,), (1,)), ((), ())), preferred_element_type=F32)


def _dot_tn(a, b):
    return lax.dot_general(a, b, (((0,), (0,)), ((), ())), preferred_element_type=F32)


def _sigmoid(x):
    return 1.0 / (1.0 + jnp.exp(-x))


def _silu(x):
    return x * _sigmoid(x)


def _mm_kernel(x_ref, w_ref, o_ref):
    o_ref[...] = _dot(x_ref[...], w_ref[...]).astype(o_ref.dtype)


def _matmul(x, w, out_dtype, tm=1024, tn=512):
    m, k = x.shape
    n = w.shape[1]
    tn = min(tn, n)
    return pl.pallas_call(
        _mm_kernel,
        grid=(m // tm, n // tn),
        in_specs=[pl.BlockSpec((tm, k), lambda i, j: (i, 0)),
                  pl.BlockSpec((k, tn), lambda i, j: (0, j))],
        out_specs=pl.BlockSpec((tm, tn), lambda i, j: (i, j)),
        out_shape=jax.ShapeDtypeStruct((m, n), out_dtype),
        compiler_params=_cparams(("parallel", "parallel")),
    )(x, w)


def _ffn_in_kernel(x_ref, wg_ref, wu_ref, o_ref):
    x = x_ref[...]
    g = _dot(x, wg_ref[...])
    u = _dot(x, wu_ref[...])
    o_ref[...] = (_silu(g) * u).astype(o_ref.dtype)


def _ffn_in(x, w_in, tm=1024, tn=512):
    m, k = x.shape
    dff = w_in.shape[1] // 2
    nj = dff // tn
    return pl.pallas_call(
        _ffn_in_kernel,
        grid=(m // tm, nj),
        in_specs=[pl.BlockSpec((tm, k), lambda i, j: (i, 0)),
                  pl.BlockSpec((k, tn), lambda i, j: (0, j)),
                  pl.BlockSpec((k, tn), lambda i, j: (0, j + nj))],
        out_specs=pl.BlockSpec((tm, tn), lambda i, j: (i, j)),
        out_shape=jax.ShapeDtypeStruct((m, dff), BF16),
        compiler_params=_cparams(("parallel", "parallel")),
    )(x, w_in, w_in)


def _layer_norm(y, w, b):
    mu = jnp.mean(y, axis=-1, keepdims=True)
    d = y - mu
    var = jnp.mean(d * d, axis=-1, keepdims=True)
    return d * lax.rsqrt(var + LN_EPS) * w + b


def _mm_res_ln_kernel(a_ref, w_ref, h_ref, lnw_ref, lnb_ref, o_ref, obf_ref):
    y = DEEPNORM_ALPHA * h_ref[...] + _dot(a_ref[...], w_ref[...])
    out = _layer_norm(y, lnw_ref[...], lnb_ref[...])
    o_ref[...] = out
    obf_ref[...] = out.astype(BF16)


def _mm_res_ln(a, w, h, ln_w, ln_b):
    m, kdim = a.shape
    n = w.shape[1]
    tm = 512 if kdim * n * 2 <= MM_SMALL_WEIGHT_BYTES else 256
    return pl.pallas_call(
        _mm_res_ln_kernel,
        grid=(m // tm,),
        in_specs=[pl.BlockSpec((tm, kdim), lambda i: (i, 0)),
                  _resident((kdim, n), lambda i: (0, 0)),
                  pl.BlockSpec((tm, n), lambda i: (i, 0)),
                  pl.BlockSpec((1, n), lambda i: (0, 0)),
                  pl.BlockSpec((1, n), lambda i: (0, 0))],
        out_specs=[pl.BlockSpec((tm, n), lambda i: (i, 0)),
                   pl.BlockSpec((tm, n), lambda i: (i, 0))],
        out_shape=[jax.ShapeDtypeStruct((m, n), F32), jax.ShapeDtypeStruct((m, n), BF16)],
        compiler_params=_cparams(("parallel",)),
    )(a, w, h, ln_w.reshape(1, n), ln_b.reshape(1, n))


LOG2E = math.log2(math.e)
LANES = 128
Q_SCALE = ATT_SCALE * LOG2E


def _flash_init(m_ref, l_ref, acc_ref):
    m_ref[...] = jnp.full(m_ref.shape, -jnp.inf, F32)
    if l_ref is not None:
        l_ref[...] = jnp.zeros(l_ref.shape, F32)
    acc_ref[...] = jnp.zeros(acc_ref.shape, F32)


def _flash_update(s_ref, v, m_ref, l_ref, acc_ref):
    m_prev = m_ref[...]
    m_new = jnp.maximum(m_prev, jnp.max(s_ref[...], axis=1, keepdims=True))
    m_ref[...] = m_new
    alpha = jnp.exp2(m_prev - m_new)
    p = jnp.exp2(s_ref[...] - _lane_tile(m_new, s_ref.shape[1]))
    if l_ref is not None:
        l_ref[...] = alpha * l_ref[...] + jnp.sum(p, axis=1, keepdims=True)
    acc_ref[...] = _lane_tile(alpha, acc_ref.shape[1]) * acc_ref[...] + _dot(p.astype(BF16), v)


def _mask_scores(s_ref, visible):
    s_ref[...] = jnp.where(visible, s_ref[...], NEG_INF)


def _lane_tile(x, width):
    reps = width // x.shape[1]
    return x if reps == 1 else jnp.concatenate([x] * reps, axis=1)


def _flash_sweep(n_full, qk, process):
    qk(0, 0)

    def pair(jj, carry):
        j = 2 * jj
        qk(1, j + 1)
        process(0, j, False)
        qk(0, j + 2)
        process(1, j + 1, False)
        return carry

    lax.fori_loop(0, n_full // 2, pair, 0)
    qk(1, n_full + 1)
    process(0, n_full, True)
    process(1, n_full + 1, True)


def _ones_column(v):
    pad = jnp.zeros(v.shape[:-1] + (v.shape[-1] - 1,), v.dtype)
    return jnp.concatenate([v, jnp.ones(v.shape[:-1] + (1,), v.dtype), pad], axis=-1)


def _resident(block_shape, index_map):
    return pl.BlockSpec(block_shape, index_map, pipeline_mode=pl.Buffered(1))


HG_TILE = 512


def _hgrn_kernel(q_ref, f_ref, i_ref, g_ref, lb_ref, nw_ref, o_ref, st_ref):
    @pl.when(pl.program_id(1) == 0)
    def _():
        st_ref[...] = jnp.zeros_like(st_ref)

    lb = lb_ref[...]
    f = lb + (1.0 - lb) * _sigmoid(f_ref[...])
    logf = jnp.log(f)
    kk = 1.0 - f
    qf = _silu(q_ref[...])
    r64 = lax.broadcasted_iota(jnp.int32, logf.shape, 0) & (A_CHUNK - 1)
    b = logf
    step = 1
    while step < A_CHUNK:
        b = b + jnp.where(r64 >= step, pltpu.roll(b, step, 0), 0.0)
        step *= 2
    causal = (lax.broadcasted_iota(jnp.int32, (A_CHUNK, A_CHUNK), 0)
              >= lax.broadcasted_iota(jnp.int32, (A_CHUNK, A_CHUNK), 1))
    nw = nw_ref[...]
    for c in range(HG_TILE // A_CHUNK):
        sl = slice(c * A_CHUNK, (c + 1) * A_CHUNK)
        bc = b[sl]
        b_last = bc[A_CHUNK - 1:A_CHUNK, :]
        q_t = (qf[sl] * jnp.exp(bc)).astype(BF16)
        k_t = (kk[sl] * jnp.exp(-bc)).astype(BF16)
        vc = i_ref[sl, :].astype(BF16)
        att = jnp.where(causal, _dot_nt(q_t, k_t), 0.0)
        st = st_ref[...]
        o = _dot(att.astype(BF16), vc) + _dot_nt(q_t, st.astype(BF16))
        kdec = (kk[sl] * jnp.exp(b_last - bc)).astype(BF16)
        st_ref[...] = st * jnp.exp(b_last) + _dot_tn(vc, kdec)
        o = o * lax.rsqrt(jnp.mean(o * o, axis=-1, keepdims=True) + RMS_EPS) * nw
        o_ref[sl, :] = (o * _silu(g_ref[sl, :])).astype(o_ref.dtype)


def _hgrn2(z_a, lb, norm_w):
    t = z_a.shape[0]
    nh = A_HEADS
    col = lambda base: (lambda h, i: (i, base + h))
    return pl.pallas_call(
        _hgrn_kernel,
        grid=(nh, t // HG_TILE),
        in_specs=[pl.BlockSpec((HG_TILE, HEAD_DIM), col(0)),
                  pl.BlockSpec((HG_TILE, HEAD_DIM), col(nh)),
                  pl.BlockSpec((HG_TILE, HEAD_DIM), col(2 * nh)),
                  pl.BlockSpec((HG_TILE, HEAD_DIM), col(3 * nh)),
                  pl.BlockSpec((1, HEAD_DIM), lambda h, i: (0, h)),
                  pl.BlockSpec((1, HEAD_DIM), lambda h, i: (0, 0))],
        out_specs=pl.BlockSpec((HG_TILE, HEAD_DIM), lambda h, i: (i, h)),
        out_shape=jax.ShapeDtypeStruct((t, A_W), BF16),
        scratch_shapes=[pltpu.VMEM((HEAD_DIM, HEAD_DIM), F32)],
        compiler_params=_cparams(("parallel", "arbitrary")),
    )(z_a, z_a, z_a, z_a, lb.reshape(1, A_W), norm_w.reshape(1, HEAD_DIM))


SEG_W = CMP_STRIDE * HEAD_DIM


def _nsa_compress_kernel(seg_ref, pos_ref, w1_ref, w2_ref, o_ref):
    seg = seg_ref[0, 0]
    nseg = seg.shape[0]
    a = _dot(seg, w1_ref[0, :SEG_W, :])
    b = _dot(seg, w1_ref[0, SEG_W:, :])
    b_next = pltpu.roll(b, nseg - 1, 0)
    pos = jnp.broadcast_to(pos_ref[0], (8, CMP_LEN * HEAD_DIM))
    c = _dot(pos, w1_ref[0])[0:1, :]
    hid = _silu(a + b_next + c)
    o_ref[0, 0] = _dot(hid.astype(BF16), w2_ref[0]).astype(o_ref.dtype)


def _nsa_compress(seg, pos, w1, w2):
    _, ng, nseg, _ = seg.shape
    return pl.pallas_call(
        _nsa_compress_kernel,
        grid=(2, ng),
        in_specs=[pl.BlockSpec((1, 1, nseg, SEG_W), lambda a, g: (a, g, 0, 0)),
                  pl.BlockSpec((1, 1, CMP_LEN * HEAD_DIM), lambda a, g: (a, 0, 0)),
                  pl.BlockSpec((1, CMP_LEN * HEAD_DIM, HEAD_DIM), lambda a, g: (a, 0, 0)),
                  pl.BlockSpec((1, HEAD_DIM, HEAD_DIM), lambda a, g: (a, 0, 0))],
        out_specs=pl.BlockSpec((1, 1, nseg, HEAD_DIM), lambda a, g: (a, g, 0, 0)),
        out_shape=jax.ShapeDtypeStruct((2, ng, nseg, HEAD_DIM), BF16),
        compiler_params=_cparams(("parallel", "parallel")),
    )(seg, pos, w1, w2)


CS_TQ = 256
SL_NBP = 128


def _split3(x):
    hi = x.astype(BF16)
    r = x - hi.astype(F32)
    mid = r.astype(BF16)
    lo = (r - mid.astype(F32)).astype(BF16)
    return hi, mid, lo


def _topk_mask(score, lane, k):
    sel = jnp.zeros(score.shape, F32)
    width = float(score.shape[1])
    for _ in range(k):
        mx = jnp.max(score, axis=1, keepdims=True)
        idx = jnp.min(jnp.where(score == mx, lane, width), axis=1, keepdims=True)
        hit = lane == idx
        sel = jnp.where(hit, 1.0, sel)
        score = jnp.where(hit, -jnp.inf, score)
    return sel


def _nsa_cmp_kernel(q_ref, kc_ref, vc_ref, map_ref, o_ref, sel_ref, *, n_sel):
    i = pl.program_id(1)
    tq = q_ref.shape[0]
    ncmp = kc_ref.shape[2]
    nbp = map_ref.shape[1]
    t_pos = i * tq + lax.broadcasted_iota(jnp.int32, (tq, 1), 0)
    cmp_end = lax.broadcasted_iota(jnp.int32, (1, ncmp), 1) * CMP_STRIDE + (CMP_LEN - 1)
    mask = cmp_end <= t_pos
    kc = kc_ref[0, 0]
    vc = vc_ref[0, 0]
    psum = jnp.zeros((tq, ncmp), F32)
    for hh in range(B_HPG):
        hs = slice(hh * HEAD_DIM, (hh + 1) * HEAD_DIM)
        s = _dot_nt(q_ref[:, hs], kc) * ATT_SCALE
        s = jnp.where(mask, s, NEG_INF)
        e = jnp.exp(s - jnp.max(s, axis=1, keepdims=True))
        p = jnp.where(mask, e / jnp.sum(e, axis=1, keepdims=True), 0.0)
        o_ref[:, hs] = _dot(p.astype(BF16), vc).astype(o_ref.dtype)
        psum = psum + p
    hi, mid, lo = _split3(psum)
    cmap = map_ref[...]
    p_slc = _dot(hi, cmap) + _dot(mid, cmap) + _dot(lo, cmap)
    jb = lax.broadcasted_iota(jnp.int32, (1, nbp), 1)
    cur = t_pos // SLC_LEN
    valid = jb <= cur
    forced = (jb == 0) | (jb == cur) | (jb == cur - 1)
    score = jnp.where(valid, jnp.where(forced, FORCE_SCORE, p_slc), NEG_INF)
    sel = _topk_mask(score, jb.astype(F32), n_sel)
    sel_ref[0] = jnp.where(valid, sel, 0.0).astype(sel_ref.dtype)


def _cmp_to_slc_matrix(ncmp_pad, n_cmp, nb, nbp):
    ratio = SLC_LEN // CMP_STRIDE
    n_over = CMP_LEN // CMP_STRIDE
    mat = np.zeros((ncmp_pad, nbp), np.float32)
    for j in range(nb):
        for m in range(ratio):
            for n in range(n_over):
                c = ratio * j + m - n
                if 0 <= c < n_cmp:
                    mat[c, j] += 1.0
    return mat


def _nsa_cmp_select(z_b, cmp_kv):
    t = z_b.shape[0]
    ng = B_KV_GROUPS
    nseg = cmp_kv.shape[2]
    nb = t // SLC_LEN
    n_cmp = (t - CMP_LEN) // CMP_STRIDE + 1
    nbp = -(-nb // SL_NBP) * SL_NBP
    cmap = jnp.asarray(_cmp_to_slc_matrix(nseg, n_cmp, nb, nbp), BF16)
    gw = B_HPG * HEAD_DIM
    return pl.pallas_call(
        functools.partial(_nsa_cmp_kernel, n_sel=min(N_SLC, nb)),
        grid=(ng, t // CS_TQ),
        in_specs=[pl.BlockSpec((CS_TQ, gw), lambda g, i: (i, g)),
                  pl.BlockSpec((1, 1, nseg, HEAD_DIM), lambda g, i: (0, g, 0, 0)),
                  pl.BlockSpec((1, 1, nseg, HEAD_DIM), lambda g, i: (1, g, 0, 0)),
                  pl.BlockSpec((nseg, nbp), lambda g, i: (0, 0))],
        out_specs=[pl.BlockSpec((CS_TQ, gw), lambda g, i: (i, g)),
                   pl.BlockSpec((1, CS_TQ, nbp), lambda g, i: (g, i, 0))],
        out_shape=[jax.ShapeDtypeStruct((t, B_QW), BF16),
                   jax.ShapeDtypeStruct((ng, t, nbp), BF16)],
        compiler_params=_cparams(("parallel", "parallel")),
    )(z_b, cmp_kv, cmp_kv, cmap)


SL_TQ = 512
SL_TK = SL_TQ // 2
SL_PHASE_TILES = SL_NBP * SLC_LEN // SL_TK


def _nsa_slc_kernel(q_ref, sel_ref, k_ref, v_ref, o_ref, qa_ref, s_ref, m_ref, acc_ref, *, n_phase):
    i = pl.program_id(1)
    tq = SL_TQ
    bias = ((sel_ref[0].astype(F32) - 1.0) * (-NEG_INF)).astype(BF16)
    for hh in range(B_HPG):
        rows = slice(hh * tq, (hh + 1) * tq)
        q = (q_ref[:, hh * HEAD_DIM:(hh + 1) * HEAD_DIM].astype(F32) * Q_SCALE).astype(BF16)
        for ph in range(n_phase):
            qa_ref[ph, rows, 0:HEAD_DIM] = q
            qa_ref[ph, rows, HEAD_DIM:] = bias[:, ph * SL_NBP:(ph + 1) * SL_NBP]
    _flash_init(m_ref, None, acc_ref)
    t_pos = i * tq + (lax.broadcasted_iota(jnp.int32, (B_HPG * tq, 1), 0) & (tq - 1))

    def qk(slot, j):
        off = pl.multiple_of(j * SL_TK, SL_TK)
        qa = qa_ref[0] if n_phase == 1 else qa_ref[j // SL_PHASE_TILES]
        s_ref[slot] = _dot_nt(qa, k_ref[0, pl.ds(off, SL_TK), :])

    def process(slot, j, causal):
        off = pl.multiple_of(j * SL_TK, SL_TK)
        if causal:
            kpos = off + lax.broadcasted_iota(jnp.int32, (1, SL_TK), 1)
            _mask_scores(s_ref.at[slot], kpos <= t_pos)
        _flash_update(s_ref.at[slot], v_ref[0, pl.ds(off, SL_TK), :], m_ref, None, acc_ref)

    _flash_sweep(i * (tq // SL_TK), qk, process)
    acc = acc_ref[...]
    o = acc[:, 0:HEAD_DIM] / acc[:, HEAD_DIM:HEAD_DIM + 1]
    for hh in range(B_HPG):
        o_ref[:, hh * HEAD_DIM:(hh + 1) * HEAD_DIM] = o[hh * tq:(hh + 1) * tq].astype(o_ref.dtype)


def _nsa_slc(z_b, sel, k_aug, v_aug):
    t = z_b.shape[0]
    nbp = sel.shape[2]
    n_phase = nbp // SL_NBP
    gw = B_HPG * HEAD_DIM
    rows = B_HPG * SL_TQ
    return pl.pallas_call(
        functools.partial(_nsa_slc_kernel, n_phase=n_phase),
        grid=(B_KV_GROUPS, t // SL_TQ),
        in_specs=[pl.BlockSpec((SL_TQ, gw), lambda g, i: (i, g)),
                  pl.BlockSpec((1, SL_TQ, nbp), lambda g, i: (g, i, 0)),
                  _resident((1, t, HEAD_DIM + SL_NBP), lambda g, i: (g, 0, 0)),
                  _resident((1, t, 2 * HEAD_DIM), lambda g, i: (g, 0, 0))],
        out_specs=pl.BlockSpec((SL_TQ, gw), lambda g, i: (i, g)),
        out_shape=jax.ShapeDtypeStruct((t, B_QW), BF16),
        scratch_shapes=[pltpu.VMEM((n_phase, rows, HEAD_DIM + SL_NBP), BF16),
                        pltpu.VMEM((2, rows, SL_TK), F32),
                        pltpu.VMEM((rows, LANES), F32),
                        pltpu.VMEM((rows, 2 * HEAD_DIM), F32)],
        compiler_params=_cparams(("parallel", "arbitrary")),
    )(z_b, sel, k_aug, v_aug)


WN_TQ = WIN


def _nsa_win_kernel(q_ref, ka_ref, kb_ref, va_ref, vb_ref, oc_ref, os_ref, gate_ref, o_ref, qs_ref):
    i = pl.program_id(1)
    tq = WN_TQ
    for hh in range(B_HPG):
        q = q_ref[:, hh * HEAD_DIM:(hh + 1) * HEAD_DIM].astype(F32) * Q_SCALE
        qs_ref[hh * tq:(hh + 1) * tq, :] = q.astype(BF16)
    qs = qs_ref[...]
    t_loc = lax.broadcasted_iota(jnp.int32, (B_HPG * tq, 1), 0) & (tq - 1)
    c_loc = lax.broadcasted_iota(jnp.int32, (1, tq), 1)
    t_prev = t_loc + jnp.where(i > 0, 0, tq)
    s_a = jnp.where(c_loc > t_prev, _dot_nt(qs, ka_ref[...]), NEG_INF)
    s_b = jnp.where(c_loc <= t_loc, _dot_nt(qs, kb_ref[...]), NEG_INF)
    m = jnp.maximum(jnp.max(s_a, axis=1, keepdims=True), jnp.max(s_b, axis=1, keepdims=True))
    p_a = jnp.exp2(s_a - m)
    p_b = jnp.exp2(s_b - m)
    l = jnp.sum(p_a, axis=1, keepdims=True) + jnp.sum(p_b, axis=1, keepdims=True)
    o_w = (_dot(p_a.astype(BF16), va_ref[...]) + _dot(p_b.astype(BF16), vb_ref[...])) / l
    gates = _sigmoid(gate_ref[...])
    for hh in range(B_HPG):
        hs = slice(hh * HEAD_DIM, (hh + 1) * HEAD_DIM)
        g_c, g_s, g_w = (gates[:, 3 * hh + c:3 * hh + c + 1] for c in range(3))
        o = (g_c * oc_ref[:, hs].astype(F32) + g_s * os_ref[:, hs].astype(F32)
             + g_w * o_w[hh * tq:(hh + 1) * tq])
        o_ref[:, hs] = o.astype(o_ref.dtype)


def _nsa_win_combine(z_b, o_c, o_s, z_gate):
    t = z_b.shape[0]
    gw = B_HPG * HEAD_DIM
    k_col0 = (B_QW + 4 * B_KVW) // HEAD_DIM
    v_col0 = (B_QW + 5 * B_KVW) // HEAD_DIM
    prev_tile = lambda col0: (lambda g, i: (jnp.maximum(i - 1, 0), col0 + g))
    this_tile = lambda col0: (lambda g, i: (i, col0 + g))
    kv_block = (WN_TQ, HEAD_DIM)
    return pl.pallas_call(
        _nsa_win_kernel,
        grid=(B_KV_GROUPS, t // WN_TQ),
        in_specs=[pl.BlockSpec((WN_TQ, gw), lambda g, i: (i, g)),
                  pl.BlockSpec(kv_block, prev_tile(k_col0)),
                  pl.BlockSpec(kv_block, this_tile(k_col0)),
                  pl.BlockSpec(kv_block, prev_tile(v_col0)),
                  pl.BlockSpec(kv_block, this_tile(v_col0)),
                  pl.BlockSpec((WN_TQ, gw), lambda g, i: (i, g)),
                  pl.BlockSpec((WN_TQ, gw), lambda g, i: (i, g)),
                  pl.BlockSpec((WN_TQ, HEAD_DIM), lambda g, i: (i, g))],
        out_specs=pl.BlockSpec((WN_TQ, gw), lambda g, i: (i, g)),
        out_shape=jax.ShapeDtypeStruct((t, B_QW), BF16),
        scratch_shapes=[pltpu.VMEM((B_HPG * WN_TQ, HEAD_DIM), BF16)],
        compiler_params=_cparams(("parallel", "parallel")),
    )(z_b, z_b, z_b, z_b, z_b, o_c, o_s, z_gate)


def _nsa(z_b, z_gate, pos_k, w1k, w2k, pos_v, w1v, w2v):
    t = z_b.shape[0]
    ng = B_KV_GROUPS

    def segs(col0):
        z = z_b[:, col0:col0 + B_KVW].reshape(t, ng, HEAD_DIM).transpose(1, 0, 2)
        return z.reshape(ng, t // CMP_STRIDE, SEG_W)

    seg = jnp.stack([segs(B_QW), segs(B_QW + B_KVW)])
    pos = jnp.stack([pos_k, pos_v]).reshape(2, 1, CMP_LEN * HEAD_DIM).astype(BF16)
    w1 = jnp.stack([w1k, w1v]).astype(BF16)
    w2 = jnp.stack([w2k, w2v]).astype(BF16)
    cmp_kv = _nsa_compress(seg, pos, w1, w2)
    o_c, sel = _nsa_cmp_select(z_b, cmp_kv)
    k_s = z_b[:, B_QW + 2 * B_KVW:B_QW + 3 * B_KVW].reshape(t, ng, HEAD_DIM).transpose(1, 0, 2)
    onehot = ((np.arange(t)[:, None] // SLC_LEN) % SL_NBP == np.arange(SL_NBP)[None, :]).astype(np.float32)
    k_aug = jnp.concatenate([k_s, jnp.broadcast_to(jnp.asarray(onehot, BF16), (ng, t, SL_NBP))], axis=-1)
    v_s = z_b[:, B_QW + 3 * B_KVW:B_QW + 4 * B_KVW].reshape(t, ng, HEAD_DIM).transpose(1, 0, 2)
    o_s = _nsa_slc(z_b, sel, k_aug, _ones_column(v_s))
    return _nsa_win_combine(z_b, o_c, o_s, z_gate)


KM_ROWS = 8


def _kmean_kernel(k_ref, o_ref):
    x = k_ref[...].astype(F32)
    o_ref[...] = jnp.mean(x.reshape(KM_ROWS, MOBA_BLOCK, x.shape[1]), axis=1)


def _moba_kmean(z_cd):
    t = z_cd.shape[0]
    nbm = t // MOBA_BLOCK
    return pl.pallas_call(
        _kmean_kernel,
        grid=(nbm // KM_ROWS,),
        in_specs=[pl.BlockSpec((KM_ROWS * MOBA_BLOCK, C_W), lambda i: (i, 1))],
        out_specs=pl.BlockSpec((KM_ROWS, C_W), lambda i: (i, 0)),
        out_shape=jax.ShapeDtypeStruct((nbm, C_W), F32),
        compiler_params=_cparams(("parallel",)),
    )(z_cd)


MB_TQ = 1024
MB_TK = MB_TQ // 2
MB_NBP = 128


def _moba_kernel(q_ref, km_ref, k_ref, v_ref, o_ref, qa_ref, s_ref, m_ref, acc_ref):
    i = pl.program_id(1)
    tq = MB_TQ
    q = q_ref[...]
    km = km_ref[...]
    km_hi = km.astype(BF16)
    km_lo = (km - km_hi.astype(F32)).astype(BF16)
    gate = _dot_nt(q, km_hi) + _dot_nt(q, km_lo)
    jb = lax.broadcasted_iota(jnp.int32, (1, MB_NBP), 1)
    t_pos = i * tq + lax.broadcasted_iota(jnp.int32, (tq, 1), 0)
    cur = t_pos // MOBA_BLOCK
    earlier = jb < cur
    sel = _topk_mask(jnp.where(earlier, gate, NEG_INF), jb.astype(F32), MOBA_TOPK)
    sel = jnp.where(jb == cur, 1.0, jnp.where(earlier, sel, 0.0))
    qa_ref[:, 0:HEAD_DIM] = (q.astype(F32) * Q_SCALE).astype(BF16)
    qa_ref[:, HEAD_DIM:] = ((sel - 1.0) * (-NEG_INF)).astype(BF16)
    _flash_init(m_ref, None, acc_ref)

    def qk(slot, j):
        off = pl.multiple_of(j * MB_TK, MB_TK)
        s_ref[slot] = _dot_nt(qa_ref[...], k_ref[0, pl.ds(off, MB_TK), :])

    def process(slot, j, causal):
        off = pl.multiple_of(j * MB_TK, MB_TK)
        if causal:
            kpos = off + lax.broadcasted_iota(jnp.int32, (1, MB_TK), 1)
            _mask_scores(s_ref.at[slot], kpos <= t_pos)
        _flash_update(s_ref.at[slot], v_ref[0, pl.ds(off, MB_TK), :], m_ref, None, acc_ref)

    _flash_sweep(i * (tq // MB_TK), qk, process)
    acc = acc_ref[...]
    o_ref[...] = (acc[:, 0:HEAD_DIM] / acc[:, HEAD_DIM:HEAD_DIM + 1]).astype(o_ref.dtype)


def _moba(z_cd):
    t = z_cd.shape[0]
    nbm = t // MOBA_BLOCK
    nh = C_HEADS
    k_mean = _moba_kmean(z_cd)
    k_mean = jnp.pad(k_mean, ((0, MB_NBP - nbm), (0, 0)))
    k_h = z_cd[:, C_W:2 * C_W].reshape(t, nh, HEAD_DIM).transpose(1, 0, 2)
    v_h = z_cd[:, 2 * C_W:3 * C_W].reshape(t, nh, HEAD_DIM).transpose(1, 0, 2)
    onehot = (np.arange(t)[:, None] // MOBA_BLOCK == np.arange(MB_NBP)[None, :]).astype(np.float32)
    k_aug = jnp.concatenate([k_h, jnp.broadcast_to(jnp.asarray(onehot, BF16), (nh, t, MB_NBP))], axis=-1)
    return pl.pallas_call(
        _moba_kernel,
        grid=(nh, t // MB_TQ),
        in_specs=[pl.BlockSpec((MB_TQ, HEAD_DIM), lambda h, i: (i, h)),
                  pl.BlockSpec((MB_NBP, HEAD_DIM), lambda h, i: (0, h)),
                  _resident((1, t, HEAD_DIM + MB_NBP), lambda h, i: (h, 0, 0)),
                  _resident((1, t, 2 * HEAD_DIM), lambda h, i: (h, 0, 0))],
        out_specs=pl.BlockSpec((MB_TQ, HEAD_DIM), lambda h, i: (i, h)),
        out_shape=jax.ShapeDtypeStruct((t, C_W), BF16),
        scratch_shapes=[pltpu.VMEM((MB_TQ, HEAD_DIM + MB_NBP), BF16),
                        pltpu.VMEM((2, MB_TQ, MB_TK), F32),
                        pltpu.VMEM((MB_TQ, LANES), F32),
                        pltpu.VMEM((MB_TQ, 2 * HEAD_DIM), F32)],
        compiler_params=_cparams(("parallel", "arbitrary")),
    )(z_cd, k_mean, k_aug, _ones_column(v_h))


DF_TQ = 1024
DF_TK = DF_TQ // 2


def _diff_kernel(q_ref, k_ref, v_ref, lq1_ref, lk1_ref, lq2_ref, lk2_ref, nw_ref, o_ref,
                 qs_ref, s_ref, m_ref, l_ref, acc_ref, *, lam_init):
    i = pl.program_id(1)
    tq = DF_TQ
    for mp in range(2):
        q = q_ref[:, mp * D_DK:(mp + 1) * D_DK].astype(F32) * (D_DK ** -0.5 * LOG2E)
        qs_ref[mp] = q.astype(BF16)
        _flash_init(m_ref.at[mp], l_ref.at[mp], acc_ref.at[mp])
    t_pos = i * tq + lax.broadcasted_iota(jnp.int32, (tq, 1), 0)

    def qk(slot, j):
        off = pl.multiple_of(j * DF_TK, DF_TK)
        for mp in range(2):
            s_ref[slot, mp] = _dot_nt(qs_ref[mp], k_ref[pl.ds(off, DF_TK), mp * D_DK:(mp + 1) * D_DK])

    def process(slot, j, causal):
        off = pl.multiple_of(j * DF_TK, DF_TK)
        v = v_ref[pl.ds(off, DF_TK), :]
        for mp in range(2):
            if causal:
                kpos = off + lax.broadcasted_iota(jnp.int32, (1, DF_TK), 1)
                _mask_scores(s_ref.at[slot, mp], kpos <= t_pos)
            _flash_update(s_ref.at[slot, mp], v, m_ref.at[mp], l_ref.at[mp], acc_ref.at[mp])

    _flash_sweep(i * (tq // DF_TK), qk, process)
    lam = (jnp.exp(jnp.sum(lq1_ref[...] * lk1_ref[...], axis=1, keepdims=True))
           - jnp.exp(jnp.sum(lq2_ref[...] * lk2_ref[...], axis=1, keepdims=True)) + lam_init)
    o = (acc_ref[0] / _lane_tile(l_ref[0], D_DV)
         - lam * (acc_ref[1] / _lane_tile(l_ref[1], D_DV)))
    o = o * lax.rsqrt(jnp.mean(o * o, axis=-1, keepdims=True) + RMS_EPS) * nw_ref[...]
    o_ref[...] = (o * (1.0 - lam_init)).astype(o_ref.dtype)


def _diff_attn(z_cd, lq1, lk1, lq2, lk2, subln_w, layer_idx):
    t = z_cd.shape[0]
    lam_init = 0.8 - 0.6 * math.exp(-0.3 * layer_idx)
    q_col0 = 3 * C_W // D_DV
    k_col0 = (3 * C_W + D_QW) // D_DV
    v_col0 = (3 * C_W + 2 * D_QW) // D_DV
    vec = pl.BlockSpec((1, D_DK), lambda h, i: (0, 0))
    return pl.pallas_call(
        functools.partial(_diff_kernel, lam_init=lam_init),
        grid=(D_HEADS, t // DF_TQ),
        in_specs=[pl.BlockSpec((DF_TQ, 2 * D_DK), lambda h, i: (i, q_col0 + h)),
                  _resident((t, 2 * D_DK), lambda h, i: (0, k_col0 + h)),
                  _resident((t, D_DV), lambda h, i: (0, v_col0 + h)),
                  vec, vec, vec, vec,
                  pl.BlockSpec((1, D_DV), lambda h, i: (0, 0))],
        out_specs=pl.BlockSpec((DF_TQ, D_DV), lambda h, i: (i, h)),
        out_shape=jax.ShapeDtypeStruct((t, D_VW), BF16),
        scratch_shapes=[pltpu.VMEM((2, DF_TQ, D_DK), BF16),
                        pltpu.VMEM((2, 2, DF_TQ, DF_TK), F32),
                        pltpu.VMEM((2, DF_TQ, LANES), F32),
                        pltpu.VMEM((2, DF_TQ, LANES), F32),
                        pltpu.VMEM((2, DF_TQ, D_DV), F32)],
        compiler_params=_cparams(("parallel", "arbitrary")),
    )(z_cd, z_cd, z_cd, lq1.reshape(1, D_DK), lk1.reshape(1, D_DK), lq2.reshape(1, D_DK),
      lk2.reshape(1, D_DK), subln_w.reshape(1, D_DV))


def _gate_weight(w_gate):
    d = w_gate.shape[0]
    per = 3 * B_HPG
    wg = w_gate.reshape(d, B_KV_GROUPS, per)
    wg = jnp.pad(wg, ((0, 0), (0, 0), (0, HEAD_DIM - per)))
    return wg.reshape(d, B_KV_GROUPS * HEAD_DIM)


def kernel(x, ln_w, ln_b, ffn_w_in, ffn_w_out, ab_w_in, ab_w_out, hgrn_lower_bounds, hgrn_norm_w,
           nsa_cmp_pos_k, nsa_cmp_k_w1, nsa_cmp_k_w2, nsa_cmp_pos_v, nsa_cmp_v_w1, nsa_cmp_v_w2,
           cd_w_in, cd_w_out, diff_lambda_q1, diff_lambda_k1, diff_lambda_q2, diff_lambda_k2, diff_subln_w):
    bsz, t, d = x.shape
    lb_all = jnp.cumsum(jax.nn.softmax(hgrn_lower_bounds.astype(F32), axis=0), axis=0)
    outs = []
    for bi in range(bsz):
        h = x[bi]
        h_bf = h.astype(BF16)
        for layer in range(DEPTH):
            if layer % 2 == 0:
                e = layer // 2
                w = ab_w_in[e]
                n_a = 4 * A_W
                n_b = B_QW + 6 * B_KVW
                z_a = _matmul(h_bf, w[:, :n_a].astype(BF16), F32)
                z_b = _matmul(h_bf, w[:, n_a:n_a + n_b].astype(BF16), BF16)
                z_g = _matmul(h_bf, _gate_weight(w[:, n_a + n_b:]).astype(BF16), F32)
                o_a = _hgrn2(z_a, lb_all[layer], hgrn_norm_w[e])
                o_b = _nsa(z_b, z_g, nsa_cmp_pos_k[e], nsa_cmp_k_w1[e], nsa_cmp_k_w2[e],
                           nsa_cmp_pos_v[e], nsa_cmp_v_w1[e], nsa_cmp_v_w2[e])
                mix_in = jnp.concatenate([o_a, o_b], axis=-1)
                w_out = ab_w_out[e]
            else:
                oi = layer // 2
                z_cd = _matmul(h_bf, cd_w_in[oi].astype(BF16), BF16)
                o_c = _moba(z_cd)
                o_d = _diff_attn(z_cd, diff_lambda_q1[oi], diff_lambda_k1[oi], diff_lambda_q2[oi],
                                 diff_lambda_k2[oi], diff_subln_w[oi], layer)
                mix_in = jnp.concatenate([o_c, o_d], axis=-1)
                w_out = cd_w_out[oi]
            h, h_bf = _mm_res_ln(mix_in, w_out.astype(BF16), h, ln_w[layer, 0], ln_b[layer, 0])
            act = _ffn_in(h_bf, ffn_w_in[layer].astype(BF16))
            h, h_bf = _mm_res_ln(act, ffn_w_out[layer].astype(BF16), h, ln_w[layer, 1], ln_b[layer, 1])
        outs.append(h)
    return jnp.stack(outs)
```

```python
import functools
import math

import numpy as np
import jax
import jax.numpy as jnp
from jax import lax
from jax.experimental import pallas as pl
from jax.experimental.pallas import tpu as pltpu

F32 = jnp.float32
BF16 = jnp.bfloat16

D_MODEL = 2048
DEPTH = 2
HEAD_DIM = 128
A_HEADS = 8
A_CHUNK = 64
B_HEADS = 8
B_KV_GROUPS = 2
B_HPG = B_HEADS // B_KV_GROUPS
CMP_LEN = 32
CMP_STRIDE = 16
SLC_LEN = 64
N_SLC = 16
WIN = 512
C_HEADS = 8
MOBA_BLOCK = 256
MOBA_TOPK = 3
D_HEADS = 4
D_DK = 128
D_DV = 2 * D_DK
D_FF = ((8 * D_MODEL + 3 * 256 - 1) // (3 * 256)) * 256

DEEPNORM_ALPHA = (2 * DEPTH) ** 0.25
NEG_INF = -1e30
FORCE_SCORE = 1e9
LN_EPS = 1e-5
RMS_EPS = 1e-6
ATT_SCALE = HEAD_DIM ** -0.5

A_W = A_HEADS * HEAD_DIM
B_QW = B_HEADS * HEAD_DIM
B_KVW = B_KV_GROUPS * HEAD_DIM
C_W = C_HEADS * HEAD_DIM
D_QW = D_HEADS * 2 * D_DK
D_VW = D_HEADS * D_DV

VMEM_LIMIT = 56 * 1024 * 1024
MM_SMALL_WEIGHT_BYTES = 8 * 1024 * 1024


def _cparams(sem):
    return pltpu.CompilerParams(dimension_semantics=sem, vmem_limit_bytes=VMEM_LIMIT)


def _dot(a, b):
    return jnp.dot(a, b, preferred_element_type=F32)


def _dot_nt(a, b):
    return lax.dot_general(a, b, (((1,), (1,)), ((), ())), preferred_element_type=F32)


def _dot_tn(a, b):
    return lax.dot_general(a, b, (((0,), (0,)), ((), ())), preferred_element_type=F32)


def _sigmoid(x):
    return 1.0 / (1.0 + jnp.exp(-x))


def _silu(x):
    return x * _sigmoid(x)


def _mm_kernel(x_ref, w_ref, o_ref):
    o_ref[...] = _dot(x_ref[...], w_ref[...]).astype(o_ref.dtype)


def _matmul(x, w, out_dtype, tm=1024, tn=512):
    m, k = x.shape
    n = w.shape[1]
    tn = min(tn, n)
    return pl.pallas_call(
        _mm_kernel,
        grid=(m // tm, n // tn),
        in_specs=[pl.BlockSpec((tm, k), lambda i, j: (i, 0)),
                  pl.BlockSpec((k, tn), lambda i, j: (0, j))],
        out_specs=pl.BlockSpec((tm, tn), lambda i, j: (i, j)),
        out_shape=jax.ShapeDtypeStruct((m, n), out_dtype),
        compiler_params=_cparams(("parallel", "parallel")),
    )(x, w)


def _ffn_in_kernel(x_ref, wg_ref, wu_ref, o_ref):
    x = x_ref[...]
    g = _dot(x, wg_ref[...])
    u = _dot(x, wu_ref[...])
    o_ref[...] = (_silu(g) * u).astype(o_ref.dtype)


def _ffn_in(x, w_in, tm=1024, tn=512):
    m, k = x.shape
    dff = w_in.shape[1] // 2
    nj = dff // tn
    return pl.pallas_call(
        _ffn_in_kernel,
        grid=(m // tm, nj),
        in_specs=[pl.BlockSpec((tm, k), lambda i, j: (i, 0)),
                  pl.BlockSpec((k, tn), lambda i, j: (0, j)),
                  pl.BlockSpec((k, tn), lambda i, j: (0, j + nj))],
        out_specs=pl.BlockSpec((tm, tn), lambda i, j: (i, j)),
        out_shape=jax.ShapeDtypeStruct((m, dff), BF16),
        compiler_params=_cparams(("parallel", "parallel")),
    )(x, w_in, w_in)


def _layer_norm(y, w, b):
    mu = jnp.mean(y, axis=-1, keepdims=True)
    d = y - mu
    var = jnp.mean(d * d, axis=-1, keepdims=True)
    return d * lax.rsqrt(var + LN_EPS) * w + b


def _mm_res_ln_kernel(a_ref, w_ref, h_ref, lnw_ref, lnb_ref, o_ref, obf_ref):
    y = DEEPNORM_ALPHA * h_ref[...] + _dot(a_ref[...], w_ref[...])
    out = _layer_norm(y, lnw_ref[...], lnb_ref[...])
    o_ref[...] = out
    obf_ref[...] = out.astype(BF16)


def _mm_res_ln(a, w, h, ln_w, ln_b):
    m, kdim = a.shape
    n = w.shape[1]
    tm = 512 if kdim * n * 2 <= MM_SMALL_WEIGHT_BYTES else 256
    return pl.pallas_call(
        _mm_res_ln_kernel,
        grid=(m // tm,),
        in_specs=[pl.BlockSpec((tm, kdim), lambda i: (i, 0)),
                  _resident((kdim, n), lambda i: (0, 0)),
                  pl.BlockSpec((tm, n), lambda i: (i, 0)),
                  pl.BlockSpec((1, n), lambda i: (0, 0)),
                  pl.BlockSpec((1, n), lambda i: (0, 0))],
        out_specs=[pl.BlockSpec((tm, n), lambda i: (i, 0)),
                   pl.BlockSpec((tm, n), lambda i: (i, 0))],
        out_shape=[jax.ShapeDtypeStruct((m, n), F32), jax.ShapeDtypeStruct((m, n), BF16)],
        compiler_params=_cparams(("parallel",)),
    )(a, w, h, ln_w.reshape(1, n), ln_b.reshape(1, n))


LOG2E = math.log2(math.e)
LANES = 128
Q_SCALE = ATT_SCALE * LOG2E


def _flash_init(m_ref, l_ref, acc_ref):
    m_ref[...] = jnp.full(m_ref.shape, -jnp.inf, F32)
    if l_ref is not None:
        l_ref[...] = jnp.zeros(l_ref.shape, F32)
    acc_ref[...] = jnp.zeros(acc_ref.shape, F32)


def _flash_update(s_ref, v, m_ref, l_ref, acc_ref):
    m_prev = m_ref[...]
    m_new = jnp.maximum(m_prev, jnp.max(s_ref[...], axis=1, keepdims=True))
    m_ref[...] = m_new
    alpha = jnp.exp2(m_prev - m_new)
    p = jnp.exp2(s_ref[...] - _lane_tile(m_new, s_ref.shape[1]))
    if l_ref is not None:
        l_ref[...] = alpha * l_ref[...] + jnp.sum(p, axis=1, keepdims=True)
    acc_ref[...] = _lane_tile(alpha, acc_ref.shape[1]) * acc_ref[...] + _dot(p.astype(BF16), v)


def _mask_scores(s_ref, visible):
    s_ref[...] = jnp.where(visible, s_ref[...], NEG_INF)


def _lane_tile(x, width):
    reps = width // x.shape[1]
    return x if reps == 1 else jnp.concatenate([x] * reps, axis=1)


def _flash_sweep(n_tiles, qk, process):
    n_pairs = (n_tiles + 1) // 2
    qk(0, 0)

    def pair(jj, carry):
        j = 2 * jj
        qk(1, j + 1)
        process(0, j, False)
        qk(0, j + 2)
        process(1, j + 1, False)
        return carry

    lax.fori_loop(0, n_pairs - 1, pair, 0)
    j_last = 2 * (n_pairs - 1)
    qk(1, j_last + 1)
    process(0, j_last, True)
    process(1, j_last + 1, True)


def _ones_column(v):
    pad = jnp.zeros(v.shape[:-1] + (v.shape[-1] - 1,), v.dtype)
    return jnp.concatenate([v, jnp.ones(v.shape[:-1] + (1,), v.dtype), pad], axis=-1)


def _resident(block_shape, index_map):
    return pl.BlockSpec(block_shape, index_map, pipeline_mode=pl.Buffered(1))


HG_TILE = 512


def _hgrn_kernel(q_ref, f_ref, i_ref, g_ref, lb_ref, nw_ref, o_ref, st_ref):
    @pl.when(pl.program_id(1) == 0)
    def _():
        st_ref[...] = jnp.zeros_like(st_ref)

    lb = lb_ref[...]
    f = lb + (1.0 - lb) * _sigmoid(f_ref[...])
    logf = jnp.log(f)
    kk = 1.0 - f
    qf = _silu(q_ref[...])
    r64 = lax.broadcasted_iota(jnp.int32, logf.shape, 0) & (A_CHUNK - 1)
    b = logf
    step = 1
    while step < A_CHUNK:
        b = b + jnp.where(r64 >= step, pltpu.roll(b, step, 0), 0.0)
        step *= 2
    causal = (lax.broadcasted_iota(jnp.int32, (A_CHUNK, A_CHUNK), 0)
              >= lax.broadcasted_iota(jnp.int32, (A_CHUNK, A_CHUNK), 1))
    nw = nw_ref[...]
    for c in range(HG_TILE // A_CHUNK):
        sl = slice(c * A_CHUNK, (c + 1) * A_CHUNK)
        bc = b[sl]
        b_last = bc[A_CHUNK - 1:A_CHUNK, :]
        q_t = (qf[sl] * jnp.exp(bc)).astype(BF16)
        k_t = (kk[sl] * jnp.exp(-bc)).astype(BF16)
        vc = i_ref[sl, :].astype(BF16)
        att = jnp.where(causal, _dot_nt(q_t, k_t), 0.0)
        st = st_ref[...]
        o = _dot(att.astype(BF16), vc) + _dot_nt(q_t, st.astype(BF16))
        kdec = (kk[sl] * jnp.exp(b_last - bc)).astype(BF16)
        st_ref[...] = st * jnp.exp(b_last) + _dot_tn(vc, kdec)
        o = o * lax.rsqrt(jnp.mean(o * o, axis=-1, keepdims=True) + RMS_EPS) * nw
        o_ref[sl, :] = (o * _silu(g_ref[sl, :])).astype(o_ref.dtype)


def _hgrn2(z_a, lb, norm_w):
    t = z_a.shape[0]
    nh = A_HEADS
    col = lambda base: (lambda h, i: (i, base + h))
    return pl.pallas_call(
        _hgrn_kernel,
        grid=(nh, t // HG_TILE),
        in_specs=[pl.BlockSpec((HG_TILE, HEAD_DIM), col(0)),
                  pl.BlockSpec((HG_TILE, HEAD_DIM), col(nh)),
                  pl.BlockSpec((HG_TILE, HEAD_DIM), col(2 * nh)),
                  pl.BlockSpec((HG_TILE, HEAD_DIM), col(3 * nh)),
                  pl.BlockSpec((1, HEAD_DIM), lambda h, i: (0, h)),
                  pl.BlockSpec((1, HEAD_DIM), lambda h, i: (0, 0))],
        out_specs=pl.BlockSpec((HG_TILE, HEAD_DIM), lambda h, i: (i, h)),
        out_shape=jax.ShapeDtypeStruct((t, A_W), BF16),
        scratch_shapes=[pltpu.VMEM((HEAD_DIM, HEAD_DIM), F32)],
        compiler_params=_cparams(("parallel", "arbitrary")),
    )(z_a, z_a, z_a, z_a, lb.reshape(1, A_W), norm_w.reshape(1, HEAD_DIM))


SEG_W = CMP_STRIDE * HEAD_DIM


def _nsa_compress_kernel(seg_ref, pos_ref, w1_ref, w2_ref, o_ref):
    seg = seg_ref[0, 0]
    nseg = seg.shape[0]
    a = _dot(seg, w1_ref[0, :SEG_W, :])
    b = _dot(seg, w1_ref[0, SEG_W:, :])
    b_next = pltpu.roll(b, nseg - 1, 0)
    pos = jnp.broadcast_to(pos_ref[0], (8, CMP_LEN * HEAD_DIM))
    c = _dot(pos, w1_ref[0])[0:1, :]
    hid = _silu(a + b_next + c)
    o_ref[0, 0] = _dot(hid.astype(BF16), w2_ref[0]).astype(o_ref.dtype)


def _nsa_compress(seg, pos, w1, w2):
    _, ng, nseg, _ = seg.shape
    return pl.pallas_call(
        _nsa_compress_kernel,
        grid=(2, ng),
        in_specs=[pl.BlockSpec((1, 1, nseg, SEG_W), lambda a, g: (a, g, 0, 0)),
                  pl.BlockSpec((1, 1, CMP_LEN * HEAD_DIM), lambda a, g: (a, 0, 0)),
                  pl.BlockSpec((1, CMP_LEN * HEAD_DIM, HEAD_DIM), lambda a, g: (a, 0, 0)),
                  pl.BlockSpec((1, HEAD_DIM, HEAD_DIM), lambda a, g: (a, 0, 0))],
        out_specs=pl.BlockSpec((1, 1, nseg, HEAD_DIM), lambda a, g: (a, g, 0, 0)),
        out_shape=jax.ShapeDtypeStruct((2, ng, nseg, HEAD_DIM), BF16),
        compiler_params=_cparams(("parallel", "parallel")),
    )(seg, pos, w1, w2)


CS_TQ = 256
SL_NBP = 128


def _split3(x):
    hi = x.astype(BF16)
    r = x - hi.astype(F32)
    mid = r.astype(BF16)
    lo = (r - mid.astype(F32)).astype(BF16)
    return hi, mid, lo


def _topk_mask(score, lane, k):
    sel = jnp.zeros(score.shape, F32)
    width = float(score.shape[1])
    for _ in range(k):
        mx = jnp.max(score, axis=1, keepdims=True)
        idx = jnp.min(jnp.where(score == mx, lane, width), axis=1, keepdims=True)
        hit = lane == idx
        sel = jnp.where(hit, 1.0, sel)
        score = jnp.where(hit, -jnp.inf, score)
    return sel


def _nsa_cmp_kernel(q_ref, kc_ref, vc_ref, map_ref, o_ref, sel_ref, *, n_sel):
    i = pl.program_id(1)
    tq = q_ref.shape[0]
    ncmp = kc_ref.shape[2]
    nbp = map_ref.shape[1]
    t_pos = i * tq + lax.broadcasted_iota(jnp.int32, (tq, 1), 0)
    cmp_end = lax.broadcasted_iota(jnp.int32, (1, ncmp), 1) * CMP_STRIDE + (CMP_LEN - 1)
    mask = cmp_end <= t_pos
    kc = kc_ref[0, 0]
    vc = vc_ref[0, 0]
    psum = jnp.zeros((tq, ncmp), F32)
    for hh in range(B_HPG):
        hs = slice(hh * HEAD_DIM, (hh + 1) * HEAD_DIM)
        s = _dot_nt(q_ref[:, hs], kc) * ATT_SCALE
        s = jnp.where(mask, s, NEG_INF)
        e = jnp.exp(s - jnp.max(s, axis=1, keepdims=True))
        p = jnp.where(mask, e / jnp.sum(e, axis=1, keepdims=True), 0.0)
        o_ref[:, hs] = _dot(p.astype(BF16), vc).astype(o_ref.dtype)
        psum = psum + p
    hi, mid, lo = _split3(psum)
    cmap = map_ref[...]
    p_slc = _dot(hi, cmap) + _dot(mid, cmap) + _dot(lo, cmap)
    jb = lax.broadcasted_iota(jnp.int32, (1, nbp), 1)
    cur = t_pos // SLC_LEN
    valid = jb <= cur
    forced = (jb == 0) | (jb == cur) | (jb == cur - 1)
    score = jnp.where(valid, jnp.where(forced, FORCE_SCORE, p_slc), NEG_INF)
    sel = _topk_mask(score, jb.astype(F32), n_sel)
    sel_ref[0] = jnp.where(valid, sel, 0.0).astype(sel_ref.dtype)


def _cmp_to_slc_matrix(ncmp_pad, n_cmp, nb, nbp):
    ratio = SLC_LEN // CMP_STRIDE
    n_over = CMP_LEN // CMP_STRIDE
    mat = np.zeros((ncmp_pad, nbp), np.float32)
    for j in range(nb):
        for m in range(ratio):
            for n in range(n_over):
                c = ratio * j + m - n
                if 0 <= c < n_cmp:
                    mat[c, j] += 1.0
    return mat


def _nsa_cmp_select(z_b, cmp_kv):
    t = z_b.shape[0]
    ng = B_KV_GROUPS
    nseg = cmp_kv.shape[2]
    nb = t // SLC_LEN
    n_cmp = (t - CMP_LEN) // CMP_STRIDE + 1
    nbp = -(-nb // SL_NBP) * SL_NBP
    cmap = jnp.asarray(_cmp_to_slc_matrix(nseg, n_cmp, nb, nbp), BF16)
    gw = B_HPG * HEAD_DIM
    return pl.pallas_call(
        functools.partial(_nsa_cmp_kernel, n_sel=min(N_SLC, nb)),
        grid=(ng, t // CS_TQ),
        in_specs=[pl.BlockSpec((CS_TQ, gw), lambda g, i: (i, g)),
                  pl.BlockSpec((1, 1, nseg, HEAD_DIM), lambda g, i: (0, g, 0, 0)),
                  pl.BlockSpec((1, 1, nseg, HEAD_DIM), lambda g, i: (1, g, 0, 0)),
                  pl.BlockSpec((nseg, nbp), lambda g, i: (0, 0))],
        out_specs=[pl.BlockSpec((CS_TQ, gw), lambda g, i: (i, g)),
                   pl.BlockSpec((1, CS_TQ, nbp), lambda g, i: (g, i, 0))],
        out_shape=[jax.ShapeDtypeStruct((t, B_QW), BF16),
                   jax.ShapeDtypeStruct((ng, t, nbp), BF16)],
        compiler_params=_cparams(("parallel", "parallel")),
    )(z_b, cmp_kv, cmp_kv, cmap)


SL_TQ = 512
SL_TK = SL_TQ
SL_PHASE_TILES = SL_NBP * SLC_LEN // SL_TK


def _nsa_slc_kernel(q_ref, sel_ref, k_ref, v_ref, o_ref, qa_ref, s_ref, m_ref, acc_ref, *, n_phase):
    i = pl.program_id(1)
    tq = SL_TQ
    bias = ((sel_ref[0].astype(F32) - 1.0) * (-NEG_INF)).astype(BF16)
    for hh in range(B_HPG):
        rows = slice(hh * tq, (hh + 1) * tq)
        q = (q_ref[:, hh * HEAD_DIM:(hh + 1) * HEAD_DIM].astype(F32) * Q_SCALE).astype(BF16)
        for ph in range(n_phase):
            qa_ref[ph, rows, 0:HEAD_DIM] = q
            qa_ref[ph, rows, HEAD_DIM:] = bias[:, ph * SL_NBP:(ph + 1) * SL_NBP]
    _flash_init(m_ref, None, acc_ref)
    t_pos = i * tq + (lax.broadcasted_iota(jnp.int32, (B_HPG * tq, 1), 0) & (tq - 1))

    def qk(slot, j):
        off = pl.multiple_of(j * SL_TK, SL_TK)
        qa = qa_ref[0] if n_phase == 1 else qa_ref[j // SL_PHASE_TILES]
        s_ref[slot] = _dot_nt(qa, k_ref[0, pl.ds(off, SL_TK), :])

    def process(slot, j, causal):
        off = pl.multiple_of(j * SL_TK, SL_TK)
        if causal:
            kpos = off + lax.broadcasted_iota(jnp.int32, (1, SL_TK), 1)
            _mask_scores(s_ref.at[slot], kpos <= t_pos)
        _flash_update(s_ref.at[slot], v_ref[0, pl.ds(off, SL_TK), :], m_ref, None, acc_ref)

    _flash_sweep((i * tq + tq + SL_TK - 1) // SL_TK, qk, process)
    acc = acc_ref[...]
    o = acc[:, 0:HEAD_DIM] / acc[:, HEAD_DIM:HEAD_DIM + 1]
    for hh in range(B_HPG):
        o_ref[:, hh * HEAD_DIM:(hh + 1) * HEAD_DIM] = o[hh * tq:(hh + 1) * tq].astype(o_ref.dtype)


def _nsa_slc(z_b, sel, k_aug, v_aug):
    t = z_b.shape[0]
    nbp = sel.shape[2]
    n_phase = nbp // SL_NBP
    gw = B_HPG * HEAD_DIM
    rows = B_HPG * SL_TQ
    return pl.pallas_call(
        functools.partial(_nsa_slc_kernel, n_phase=n_phase),
        grid=(B_KV_GROUPS, t // SL_TQ),
        in_specs=[pl.BlockSpec((SL_TQ, gw), lambda g, i: (i, g)),
                  pl.BlockSpec((1, SL_TQ, nbp), lambda g, i: (g, i, 0)),
                  _resident((1, t, HEAD_DIM + SL_NBP), lambda g, i: (g, 0, 0)),
                  _resident((1, t, 2 * HEAD_DIM), lambda g, i: (g, 0, 0))],
        out_specs=pl.BlockSpec((SL_TQ, gw), lambda g, i: (i, g)),
        out_shape=jax.ShapeDtypeStruct((t, B_QW), BF16),
        scratch_shapes=[pltpu.VMEM((n_phase, rows, HEAD_DIM + SL_NBP), BF16),
                        pltpu.VMEM((2, rows, SL_TK), F32),
                        pltpu.VMEM((rows, LANES), F32),
                        pltpu.VMEM((rows, 2 * HEAD_DIM), F32)],
        compiler_params=_cparams(("parallel", "arbitrary")),
    )(z_b, sel, k_aug, v_aug)


WN_TQ = WIN


def _nsa_win_kernel(q_ref, ka_ref, kb_ref, va_ref, vb_ref, oc_ref, os_ref, gate_ref, o_ref, qs_ref):
    i = pl.program_id(1)
    tq = WN_TQ
    for hh in range(B_HPG):
        q = q_ref[:, hh * HEAD_DIM:(hh + 1) * HEAD_DIM].astype(F32) * Q_SCALE
        qs_ref[hh * tq:(hh + 1) * tq, :] = q.astype(BF16)
    qs = qs_ref[...]
    t_loc = lax.broadcasted_iota(jnp.int32, (B_HPG * tq, 1), 0) & (tq - 1)
    c_loc = lax.broadcasted_iota(jnp.int32, (1, tq), 1)
    t_prev = t_loc + jnp.where(i > 0, 0, tq)
    s_a = jnp.where(c_loc > t_prev, _dot_nt(qs, ka_ref[...]), NEG_INF)
    s_b = jnp.where(c_loc <= t_loc, _dot_nt(qs, kb_ref[...]), NEG_INF)
    m = jnp.maximum(jnp.max(s_a, axis=1, keepdims=True), jnp.max(s_b, axis=1, keepdims=True))
    p_a = jnp.exp2(s_a - m)
    p_b = jnp.exp2(s_b - m)
    l = jnp.sum(p_a, axis=1, keepdims=True) + jnp.sum(p_b, axis=1, keepdims=True)
    o_w = (_dot(p_a.astype(BF16), va_ref[...]) + _dot(p_b.astype(BF16), vb_ref[...])) / l
    gates = _sigmoid(gate_ref[...])
    for hh in range(B_HPG):
        hs = slice(hh * HEAD_DIM, (hh + 1) * HEAD_DIM)
        g_c, g_s, g_w = (gates[:, 3 * hh + c:3 * hh + c + 1] for c in range(3))
        o = (g_c * oc_ref[:, hs].astype(F32) + g_s * os_ref[:, hs].astype(F32)
             + g_w * o_w[hh * tq:(hh + 1) * tq])
        o_ref[:, hs] = o.astype(o_ref.dtype)


def _nsa_win_combine(z_b, o_c, o_s, z_gate):
    t = z_b.shape[0]
    gw = B_HPG * HEAD_DIM
    k_col0 = (B_QW + 4 * B_KVW) // HEAD_DIM
    v_col0 = (B_QW + 5 * B_KVW) // HEAD_DIM
    prev_tile = lambda col0: (lambda g, i: (jnp.maximum(i - 1, 0), col0 + g))
    this_tile = lambda col0: (lambda g, i: (i, col0 + g))
    kv_block = (WN_TQ, HEAD_DIM)
    return pl.pallas_call(
        _nsa_win_kernel,
        grid=(B_KV_GROUPS, t // WN_TQ),
        in_specs=[pl.BlockSpec((WN_TQ, gw), lambda g, i: (i, g)),
                  pl.BlockSpec(kv_block, prev_tile(k_col0)),
                  pl.BlockSpec(kv_block, this_tile(k_col0)),
                  pl.BlockSpec(kv_block, prev_tile(v_col0)),
                  pl.BlockSpec(kv_block, this_tile(v_col0)),
                  pl.BlockSpec((WN_TQ, gw), lambda g, i: (i, g)),
                  pl.BlockSpec((WN_TQ, gw), lambda g, i: (i, g)),
                  pl.BlockSpec((WN_TQ, HEAD_DIM), lambda g, i: (i, g))],
        out_specs=pl.BlockSpec((WN_TQ, gw), lambda g, i: (i, g)),
        out_shape=jax.ShapeDtypeStruct((t, B_QW), BF16),
        scratch_shapes=[pltpu.VMEM((B_HPG * WN_TQ, HEAD_DIM), BF16)],
        compiler_params=_cparams(("parallel", "parallel")),
    )(z_b, z_b, z_b, z_b, z_b, o_c, o_s, z_gate)


def _nsa(z_b, z_gate, pos_k, w1k, w2k, pos_v, w1v, w2v):
    t = z_b.shape[0]
    ng = B_KV_GROUPS

    def segs(col0):
        z = z_b[:, col0:col0 + B_KVW].reshape(t, ng, HEAD_DIM).transpose(1, 0, 2)
        return z.reshape(ng, t // CMP_STRIDE, SEG_W)

    seg = jnp.stack([segs(B_QW), segs(B_QW + B_KVW)])
    pos = jnp.stack([pos_k, pos_v]).reshape(2, 1, CMP_LEN * HEAD_DIM).astype(BF16)
    w1 = jnp.stack([w1k, w1v]).astype(BF16)
    w2 = jnp.stack([w2k, w2v]).astype(BF16)
    cmp_kv = _nsa_compress(seg, pos, w1, w2)
    o_c, sel = _nsa_cmp_select(z_b, cmp_kv)
    k_s = z_b[:, B_QW + 2 * B_KVW:B_QW + 3 * B_KVW].reshape(t, ng, HEAD_DIM).transpose(1, 0, 2)
    onehot = ((np.arange(t)[:, None] // SLC_LEN) % SL_NBP == np.arange(SL_NBP)[None, :]).astype(np.float32)
    k_aug = jnp.concatenate([k_s, jnp.broadcast_to(jnp.asarray(onehot, BF16), (ng, t, SL_NBP))], axis=-1)
    v_s = z_b[:, B_QW + 3 * B_KVW:B_QW + 4 * B_KVW].reshape(t, ng, HEAD_DIM).transpose(1, 0, 2)
    o_s = _nsa_slc(z_b, sel, k_aug, _ones_column(v_s))
    return _nsa_win_combine(z_b, o_c, o_s, z_gate)


KM_ROWS = 8


def _kmean_kernel(k_ref, o_ref):
    x = k_ref[...].astype(F32)
    o_ref[...] = jnp.mean(x.reshape(KM_ROWS, MOBA_BLOCK, x.shape[1]), axis=1)


def _moba_kmean(z_cd):
    t = z_cd.shape[0]
    nbm = t // MOBA_BLOCK
    return pl.pallas_call(
        _kmean_kernel,
        grid=(nbm // KM_ROWS,),
        in_specs=[pl.BlockSpec((KM_ROWS * MOBA_BLOCK, C_W), lambda i: (i, 1))],
        out_specs=pl.BlockSpec((KM_ROWS, C_W), lambda i: (i, 0)),
        out_shape=jax.ShapeDtypeStruct((nbm, C_W), F32),
        compiler_params=_cparams(("parallel",)),
    )(z_cd)


MB_TQ = 1024
MB_TK = MB_TQ // 2
MB_NBP = 128


def _moba_kernel(q_ref, km_ref, k_ref, v_ref, o_ref, qa_ref, s_ref, m_ref, acc_ref):
    i = pl.program_id(1)
    tq = MB_TQ
    q = q_ref[...]
    km = km_ref[...]
    km_hi = km.astype(BF16)
    km_lo = (km - km_hi.astype(F32)).astype(BF16)
    gate = _dot_nt(q, km_hi) + _dot_nt(q, km_lo)
    jb = lax.broadcasted_iota(jnp.int32, (1, MB_NBP), 1)
    t_pos = i * tq + lax.broadcasted_iota(jnp.int32, (tq, 1), 0)
    cur = t_pos // MOBA_BLOCK
    earlier = jb < cur
    sel = _topk_mask(jnp.where(earlier, gate, NEG_INF), jb.astype(F32), MOBA_TOPK)
    sel = jnp.where(jb == cur, 1.0, jnp.where(earlier, sel, 0.0))
    qa_ref[:, 0:HEAD_DIM] = (q.astype(F32) * Q_SCALE).astype(BF16)
    qa_ref[:, HEAD_DIM:] = ((sel - 1.0) * (-NEG_INF)).astype(BF16)
    _flash_init(m_ref, None, acc_ref)

    def qk(slot, j):
        off = pl.multiple_of(j * MB_TK, MB_TK)
        s_ref[slot] = _dot_nt(qa_ref[...], k_ref[0, pl.ds(off, MB_TK), :])

    def process(slot, j, causal):
        off = pl.multiple_of(j * MB_TK, MB_TK)
        if causal:
            kpos = off + lax.broadcasted_iota(jnp.int32, (1, MB_TK), 1)
            _mask_scores(s_ref.at[slot], kpos <= t_pos)
        _flash_update(s_ref.at[slot], v_ref[0, pl.ds(off, MB_TK), :], m_ref, None, acc_ref)

    _flash_sweep((i * tq + tq + MB_TK - 1) // MB_TK, qk, process)
    acc = acc_ref[...]
    o_ref[...] = (acc[:, 0:HEAD_DIM] / acc[:, HEAD_DIM:HEAD_DIM + 1]).astype(o_ref.dtype)


def _moba(z_cd):
    t = z_cd.shape[0]
    nbm = t // MOBA_BLOCK
    nh = C_HEADS
    k_mean = _moba_kmean(z_cd)
    k_mean = jnp.pad(k_mean, ((0, MB_NBP - nbm), (0, 0)))
    k_h = z_cd[:, C_W:2 * C_W].reshape(t, nh, HEAD_DIM).transpose(1, 0, 2)
    v_h = z_cd[:, 2 * C_W:3 * C_W].reshape(t, nh, HEAD_DIM).transpose(1, 0, 2)
    onehot = (np.arange(t)[:, None] // MOBA_BLOCK == np.arange(MB_NBP)[None, :]).astype(np.float32)
    k_aug = jnp.concatenate([k_h, jnp.broadcast_to(jnp.asarray(onehot, BF16), (nh, t, MB_NBP))], axis=-1)
    return pl.pallas_call(
        _moba_kernel,
        grid=(nh, t // MB_TQ),
        in_specs=[pl.BlockSpec((MB_TQ, HEAD_DIM), lambda h, i: (i, h)),
                  pl.BlockSpec((MB_NBP, HEAD_DIM), lambda h, i: (0, h)),
                  _resident((1, t, HEAD_DIM + MB_NBP), lambda h, i: (h, 0, 0)),
                  _resident((1, t, 2 * HEAD_DIM), lambda h, i: (h, 0, 0))],
        out_specs=pl.BlockSpec((MB_TQ, HEAD_DIM), lambda h, i: (i, h)),
        out_shape=jax.ShapeDtypeStruct((t, C_W), BF16),
        scratch_shapes=[pltpu.VMEM((MB_TQ, HEAD_DIM + MB_NBP), BF16),
                        pltpu.VMEM((2, MB_TQ, MB_TK), F32),
                        pltpu.VMEM((MB_TQ, LANES), F32),
                        pltpu.VMEM((MB_TQ, 2 * HEAD_DIM), F32)],
        compiler_params=_cparams(("parallel", "arbitrary")),
    )(z_cd, k_mean, k_aug, _ones_column(v_h))


DF_TQ = 1024
DF_TK = DF_TQ // 2


def _diff_kernel(q_ref, k_ref, v_ref, lq1_ref, lk1_ref, lq2_ref, lk2_ref, nw_ref, o_ref,
                 qs_ref, s_ref, m_ref, l_ref, acc_ref, *, lam_init):
    i = pl.program_id(1)
    tq = DF_TQ
    for mp in range(2):
        q = q_ref[:, mp * D_DK:(mp + 1) * D_DK].astype(F32) * (D_DK ** -0.5 * LOG2E)
        qs_ref[mp] = q.astype(BF16)
        _flash_init(m_ref.at[mp], l_ref.at[mp], acc_ref.at[mp])
    t_pos = i * tq + lax.broadcasted_iota(jnp.int32, (tq, 1), 0)

    def qk(slot, j):
        off = pl.multiple_of(j * DF_TK, DF_TK)
        for mp in range(2):
            s_ref[slot, mp] = _dot_nt(qs_ref[mp], k_ref[pl.ds(off, DF_TK), mp * D_DK:(mp + 1) * D_DK])

    def process(slot, j, causal):
        off = pl.multiple_of(j * DF_TK, DF_TK)
        v = v_ref[pl.ds(off, DF_TK), :]
        for mp in range(2):
            if causal:
                kpos = off + lax.broadcasted_iota(jnp.int32, (1, DF_TK), 1)
                _mask_scores(s_ref.at[slot, mp], kpos <= t_pos)
            _flash_update(s_ref.at[slot, mp], v, m_ref.at[mp], l_ref.at[mp], acc_ref.at[mp])

    _flash_sweep((i * tq + tq + DF_TK - 1) // DF_TK, qk, process)
    lam = (jnp.exp(jnp.sum(lq1_ref[...] * lk1_ref[...], axis=1, keepdims=True))
           - jnp.exp(jnp.sum(lq2_ref[...] * lk2_ref[...], axis=1, keepdims=True)) + lam_init)
    o = (acc_ref[0] / _lane_tile(l_ref[0], D_DV)
         - lam * (acc_ref[1] / _lane_tile(l_ref[1], D_DV)))
    o = o * lax.rsqrt(jnp.mean(o * o, axis=-1, keepdims=True) + RMS_EPS) * nw_ref[...]
    o_ref[...] = (o * (1.0 - lam_init)).astype(o_ref.dtype)


def _diff_attn(z_cd, lq1, lk1, lq2, lk2, subln_w, layer_idx):
    t = z_cd.shape[0]
    lam_init = 0.8 - 0.6 * math.exp(-0.3 * layer_idx)
    q_col0 = 3 * C_W // D_DV
    k_col0 = (3 * C_W + D_QW) // D_DV
    v_col0 = (3 * C_W + 2 * D_QW) // D_DV
    vec = pl.BlockSpec((1, D_DK), lambda h, i: (0, 0))
    return pl.pallas_call(
        functools.partial(_diff_kernel, lam_init=lam_init),
        grid=(D_HEADS, t // DF_TQ),
        in_specs=[pl.BlockSpec((DF_TQ, 2 * D_DK), lambda h, i: (i, q_col0 + h)),
                  _resident((t, 2 * D_DK), lambda h, i: (0, k_col0 + h)),
                  _resident((t, D_DV), lambda h, i: (0, v_col0 + h)),
                  vec, vec, vec, vec,
                  pl.BlockSpec((1, D_DV), lambda h, i: (0, 0))],
        out_specs=pl.BlockSpec((DF_TQ, D_DV), lambda h, i: (i, h)),
        out_shape=jax.ShapeDtypeStruct((t, D_VW), BF16),
        scratch_shapes=[pltpu.VMEM((2, DF_TQ, D_DK), BF16),
                        pltpu.VMEM((2, 2, DF_TQ, DF_TK), F32),
                        pltpu.VMEM((2, DF_TQ, LANES), F32),
                        pltpu.VMEM((2, DF_TQ, LANES), F32),
                        pltpu.VMEM((2, DF_TQ, D_DV), F32)],
        compiler_params=_cparams(("parallel", "arbitrary")),
    )(z_cd, z_cd, z_cd, lq1.reshape(1, D_DK), lk1.reshape(1, D_DK), lq2.reshape(1, D_DK),
      lk2.reshape(1, D_DK), subln_w.reshape(1, D_DV))


def _gate_weight(w_gate):
    d = w_gate.shape[0]
    per = 3 * B_HPG
    wg = w_gate.reshape(d, B_KV_GROUPS, per)
    wg = jnp.pad(wg, ((0, 0), (0, 0), (0, HEAD_DIM - per)))
    return wg.reshape(d, B_KV_GROUPS * HEAD_DIM)


def kernel(x, ln_w, ln_b, ffn_w_in, ffn_w_out, ab_w_in, ab_w_out, hgrn_lower_bounds, hgrn_norm_w,
           nsa_cmp_pos_k, nsa_cmp_k_w1, nsa_cmp_k_w2, nsa_cmp_pos_v, nsa_cmp_v_w1, nsa_cmp_v_w2,
           cd_w_in, cd_w_out, diff_lambda_q1, diff_lambda_k1, diff_lambda_q2, diff_lambda_k2, diff_subln_w):
    bsz, t, d = x.shape
    lb_all = jnp.cumsum(jax.nn.softmax(hgrn_lower_bounds.astype(F32), axis=0), axis=0)
    outs = []
    for bi in range(bsz):
        h = x[bi]
        h_bf = h.astype(BF16)
        for layer in range(DEPTH):
            if layer % 2 == 0:
                e = layer // 2
                w = ab_w_in[e]
                n_a = 4 * A_W
                n_b = B_QW + 6 * B_KVW
                z_a = _matmul(h_bf, w[:, :n_a].astype(BF16), F32)
                z_b = _matmul(h_bf, w[:, n_a:n_a + n_b].astype(BF16), BF16)
                z_g = _matmul(h_bf, _gate_weight(w[:, n_a + n_b:]).astype(BF16), F32)
                o_a = _hgrn2(z_a, lb_all[layer], hgrn_norm_w[e])
                o_b = _nsa(z_b, z_g, nsa_cmp_pos_k[e], nsa_cmp_k_w1[e], nsa_cmp_k_w2[e],
                           nsa_cmp_pos_v[e], nsa_cmp_v_w1[e], nsa_cmp_v_w2[e])
                mix_in = jnp.concatenate([o_a, o_b], axis=-1)
                w_out = ab_w_out[e]
            else:
                oi = layer // 2
                z_cd = _matmul(h_bf, cd_w_in[oi].astype(BF16), BF16)
                o_c = _moba(z_cd)
                o_d = _diff_attn(z_cd, diff_lambda_q1[oi], diff_lambda_k1[oi], diff_lambda_q2[oi],
                                 diff_lambda_k2[oi], diff_subln_w[oi], layer)
                mix_in = jnp.concatenate([o_c, o_d], axis=-1)
                w_out = cd_w_out[oi]
            h, h_bf = _mm_res_ln(mix_in, w_out.astype(BF16), h, ln_w[layer, 0], ln_b[layer, 0])
            act = _ffn_in(h_bf, ffn_w_in[layer].astype(BF16))
            h, h_bf = _mm_res_ln(act, ffn_w_out[layer].astype(BF16), h, ln_w[layer, 1], ln_b[layer, 1])
        outs.append(h)
    return jnp.stack(outs)
```

```python
import functools
import math

import numpy as np
import jax
import jax.numpy as jnp
from jax import lax
from jax.experimental import pallas as pl
from jax.experimental.pallas import tpu as pltpu

F32 = jnp.float32
BF16 = jnp.bfloat16

D_MODEL = 2048
DEPTH = 2
HEAD_DIM = 128
A_HEADS = 8
A_CHUNK = 64
B_HEADS = 8
B_KV_GROUPS = 2
B_HPG = B_HEADS // B_KV_GROUPS
CMP_LEN = 32
CMP_STRIDE = 16
SLC_LEN = 64
N_SLC = 16
N_FORCED = 3
WIN = 512
C_HEADS = 8
MOBA_BLOCK = 256
MOBA_TOPK = 3
D_HEADS = 4
D_DK = 128
D_DV = 2 * D_DK
D_FF = ((8 * D_MODEL + 3 * 256 - 1) // (3 * 256)) * 256

DEEPNORM_ALPHA = (2 * DEPTH) ** 0.25
NEG_INF = -1e30
FORCE_SCORE = 1e9
LN_EPS = 1e-5
RMS_EPS = 1e-6
ATT_SCALE = HEAD_DIM ** -0.5

A_W = A_HEADS * HEAD_DIM
B_QW = B_HEADS * HEAD_DIM
B_KVW = B_KV_GROUPS * HEAD_DIM
C_W = C_HEADS * HEAD_DIM
D_QW = D_HEADS * 2 * D_DK
D_VW = D_HEADS * D_DV

VMEM_LIMIT = 56 * 1024 * 1024
MM_SMALL_WEIGHT_BYTES = 8 * 1024 * 1024


def _cparams(sem):
    return pltpu.CompilerParams(dimension_semantics=sem, vmem_limit_bytes=VMEM_LIMIT)


def _dot(a, b):
    return jnp.dot(a, b, preferred_element_type=F32)


def _dot_nt(a, b):
    return lax.dot_general(a, b, (((1,), (1,)), ((), ())), preferred_element_type=F32)


def _dot_tn(a, b):
    return lax.dot_general(a, b, (((0,), (0,)), ((), ())), preferred_element_type=F32)


def _sigmoid(x):
    return 1.0 / (1.0 + jnp.exp(-x))


def _silu(x):
    return x * _sigmoid(x)


def _mm_kernel(x_ref, w_ref, o_ref):
    o_ref[...] = _dot(x_ref[...], w_ref[...]).astype(o_ref.dtype)


def _matmul(x, w, out_dtype, tm=1024, tn=512):
    m, k = x.shape
    n = w.shape[1]
    tn = min(tn, n)
    return pl.pallas_call(
        _mm_kernel,
        grid=(m // tm, n // tn),
        in_specs=[pl.BlockSpec((tm, k), lambda i, j: (i, 0)),
                  pl.BlockSpec((k, tn), lambda i, j: (0, j))],
        out_specs=pl.BlockSpec((tm, tn), lambda i, j: (i, j)),
        out_shape=jax.ShapeDtypeStruct((m, n), out_dtype),
        compiler_params=_cparams(("parallel", "parallel")),
    )(x, w)


def _ffn_in_kernel(x_ref, wg_ref, wu_ref, o_ref):
    x = x_ref[...]
    g = _dot(x, wg_ref[...])
    u = _dot(x, wu_ref[...])
    o_ref[...] = (_silu(g) * u).astype(o_ref.dtype)


def _ffn_in(x, w_in, tm=1024, tn=512):
    m, k = x.shape
    dff = w_in.shape[1] // 2
    nj = dff // tn
    return pl.pallas_call(
        _ffn_in_kernel,
        grid=(m // tm, nj),
        in_specs=[pl.BlockSpec((tm, k), lambda i, j: (i, 0)),
                  pl.BlockSpec((k, tn), lambda i, j: (0, j)),
                  pl.BlockSpec((k, tn), lambda i, j: (0, j + nj))],
        out_specs=pl.BlockSpec((tm, tn), lambda i, j: (i, j)),
        out_shape=jax.ShapeDtypeStruct((m, dff), BF16),
        compiler_params=_cparams(("parallel", "parallel")),
    )(x, w_in, w_in)


def _layer_norm(y, w, b):
    mu = jnp.mean(y, axis=-1, keepdims=True)
    d = y - mu
    var = jnp.mean(d * d, axis=-1, keepdims=True)
    return d * lax.rsqrt(var + LN_EPS) * w + b


def _mm_res_ln_kernel(a_ref, w_ref, h_ref, lnw_ref, lnb_ref, o_ref, obf_ref):
    y = DEEPNORM_ALPHA * h_ref[...] + _dot(a_ref[...], w_ref[...])
    out = _layer_norm(y, lnw_ref[...], lnb_ref[...])
    o_ref[...] = out
    obf_ref[...] = out.astype(BF16)


def _mm_res_ln(a, w, h, ln_w, ln_b):
    m, kdim = a.shape
    n = w.shape[1]
    tm = 512 if kdim * n * 2 <= MM_SMALL_WEIGHT_BYTES else 256
    return pl.pallas_call(
        _mm_res_ln_kernel,
        grid=(m // tm,),
        in_specs=[pl.BlockSpec((tm, kdim), lambda i: (i, 0)),
                  _resident((kdim, n), lambda i: (0, 0)),
                  pl.BlockSpec((tm, n), lambda i: (i, 0)),
                  pl.BlockSpec((1, n), lambda i: (0, 0)),
                  pl.BlockSpec((1, n), lambda i: (0, 0))],
        out_specs=[pl.BlockSpec((tm, n), lambda i: (i, 0)),
                   pl.BlockSpec((tm, n), lambda i: (i, 0))],
        out_shape=[jax.ShapeDtypeStruct((m, n), F32), jax.ShapeDtypeStruct((m, n), BF16)],
        compiler_params=_cparams(("parallel",)),
    )(a, w, h, ln_w.reshape(1, n), ln_b.reshape(1, n))


LOG2E = math.log2(math.e)
LANES = 128
Q_SCALE = ATT_SCALE * LOG2E


def _flash_init(m_ref, l_ref, acc_ref):
    m_ref[...] = jnp.full(m_ref.shape, -jnp.inf, F32)
    if l_ref is not None:
        l_ref[...] = jnp.zeros(l_ref.shape, F32)
    acc_ref[...] = jnp.zeros(acc_ref.shape, F32)


def _flash_update(s_ref, v, m_ref, l_ref, acc_ref):
    m_prev = m_ref[...]
    m_new = jnp.maximum(m_prev, jnp.max(s_ref[...], axis=1, keepdims=True))
    m_ref[...] = m_new
    alpha = jnp.exp2(m_prev - m_new)
    p = jnp.exp2(s_ref[...] - _lane_tile(m_new, s_ref.shape[1]))
    if l_ref is not None:
        l_ref[...] = alpha * l_ref[...] + jnp.sum(p, axis=1, keepdims=True)
    acc_ref[...] = _lane_tile(alpha, acc_ref.shape[1]) * acc_ref[...] + _dot(p.astype(BF16), v)


def _mask_scores(s_ref, visible):
    s_ref[...] = jnp.where(visible, s_ref[...], NEG_INF)


def _lane_tile(x, width):
    reps = width // x.shape[1]
    return x if reps == 1 else jnp.concatenate([x] * reps, axis=1)


def _flash_sweep(n_tiles, qk, process):
    n_pairs = (n_tiles + 1) // 2
    qk(0, 0)

    def pair(jj, carry):
        j = 2 * jj
        qk(1, j + 1)
        process(0, j, False)
        qk(0, j + 2)
        process(1, j + 1, False)
        return carry

    lax.fori_loop(0, n_pairs - 1, pair, 0)
    j_last = 2 * (n_pairs - 1)
    qk(1, j_last + 1)
    process(0, j_last, True)
    process(1, j_last + 1, True)


AUG_FILL_ROWS = 512


def _fill_augmented(k_ref, v_ref, ka_ref, va_ref, block_len):
    lane = lax.broadcasted_iota(jnp.int32, (AUG_FILL_ROWS, LANES), 1)
    ones_col = jnp.where(lane == 0, 1.0, 0.0).astype(BF16)

    def fill(c, carry):
        r0 = pl.multiple_of(c * AUG_FILL_ROWS, AUG_FILL_ROWS)
        rows = pl.ds(r0, AUG_FILL_ROWS)
        blk = (r0 + lax.broadcasted_iota(jnp.int32, (AUG_FILL_ROWS, LANES), 0)) // block_len
        ka_ref[rows, 0:HEAD_DIM] = k_ref[rows, :]
        ka_ref[rows, HEAD_DIM:] = jnp.where((blk & (LANES - 1)) == lane, 1.0, 0.0).astype(BF16)
        va_ref[rows, 0:HEAD_DIM] = v_ref[rows, :]
        va_ref[rows, HEAD_DIM:] = ones_col
        return carry

    lax.fori_loop(0, k_ref.shape[0] // AUG_FILL_ROWS, fill, 0)


def _resident(block_shape, index_map):
    return pl.BlockSpec(block_shape, index_map, pipeline_mode=pl.Buffered(1))


HG_TILE = 512


def _hgrn_kernel(q_ref, f_ref, i_ref, g_ref, lb_ref, nw_ref, o_ref, st_ref):
    @pl.when(pl.program_id(1) == 0)
    def _():
        st_ref[...] = jnp.zeros_like(st_ref)

    lb = lb_ref[...]
    f = lb + (1.0 - lb) * _sigmoid(f_ref[...])
    logf = jnp.log(f)
    kk = 1.0 - f
    qf = _silu(q_ref[...])
    r64 = lax.broadcasted_iota(jnp.int32, logf.shape, 0) & (A_CHUNK - 1)
    b = logf
    step = 1
    while step < A_CHUNK:
        b = b + jnp.where(r64 >= step, pltpu.roll(b, step, 0), 0.0)
        step *= 2
    causal = (lax.broadcasted_iota(jnp.int32, (A_CHUNK, A_CHUNK), 0)
              >= lax.broadcasted_iota(jnp.int32, (A_CHUNK, A_CHUNK), 1))
    nw = nw_ref[...]
    for c in range(HG_TILE // A_CHUNK):
        sl = slice(c * A_CHUNK, (c + 1) * A_CHUNK)
        bc = b[sl]
        b_last = bc[A_CHUNK - 1:A_CHUNK, :]
        q_t = (qf[sl] * jnp.exp(bc)).astype(BF16)
        k_t = (kk[sl] * jnp.exp(-bc)).astype(BF16)
        vc = i_ref[sl, :].astype(BF16)
        att = jnp.where(causal, _dot_nt(q_t, k_t), 0.0)
        st = st_ref[...]
        o = _dot(att.astype(BF16), vc) + _dot_nt(q_t, st.astype(BF16))
        kdec = (kk[sl] * jnp.exp(b_last - bc)).astype(BF16)
        st_ref[...] = st * jnp.exp(b_last) + _dot_tn(vc, kdec)
        o = o * lax.rsqrt(jnp.mean(o * o, axis=-1, keepdims=True) + RMS_EPS) * nw
        o_ref[sl, :] = (o * _silu(g_ref[sl, :])).astype(o_ref.dtype)


def _hgrn2(z_a, lb, norm_w):
    t = z_a.shape[0]
    nh = A_HEADS
    col = lambda base: (lambda h, i: (i, base + h))
    return pl.pallas_call(
        _hgrn_kernel,
        grid=(nh, t // HG_TILE),
        in_specs=[pl.BlockSpec((HG_TILE, HEAD_DIM), col(0)),
                  pl.BlockSpec((HG_TILE, HEAD_DIM), col(nh)),
                  pl.BlockSpec((HG_TILE, HEAD_DIM), col(2 * nh)),
                  pl.BlockSpec((HG_TILE, HEAD_DIM), col(3 * nh)),
                  pl.BlockSpec((1, HEAD_DIM), lambda h, i: (0, h)),
                  pl.BlockSpec((1, HEAD_DIM), lambda h, i: (0, 0))],
        out_specs=pl.BlockSpec((HG_TILE, HEAD_DIM), lambda h, i: (i, h)),
        out_shape=jax.ShapeDtypeStruct((t, A_W), BF16),
        scratch_shapes=[pltpu.VMEM((HEAD_DIM, HEAD_DIM), F32)],
        compiler_params=_cparams(("parallel", "arbitrary")),
    )(z_a, z_a, z_a, z_a, lb.reshape(1, A_W), norm_w.reshape(1, HEAD_DIM))


SEG_W = CMP_STRIDE * HEAD_DIM


def _nsa_compress_kernel(seg_ref, pos_ref, w1_ref, w2_ref, o_ref):
    seg = seg_ref[0, 0]
    nseg = seg.shape[0]
    a = _dot(seg, w1_ref[0, :SEG_W, :])
    b = _dot(seg, w1_ref[0, SEG_W:, :])
    b_next = pltpu.roll(b, nseg - 1, 0)
    pos = jnp.broadcast_to(pos_ref[0], (8, CMP_LEN * HEAD_DIM))
    c = _dot(pos, w1_ref[0])[0:1, :]
    hid = _silu(a + b_next + c)
    o_ref[0, 0] = _dot(hid.astype(BF16), w2_ref[0]).astype(o_ref.dtype)


def _nsa_compress(seg, pos, w1, w2):
    _, ng, nseg, _ = seg.shape
    return pl.pallas_call(
        _nsa_compress_kernel,
        grid=(2, ng),
        in_specs=[pl.BlockSpec((1, 1, nseg, SEG_W), lambda a, g: (a, g, 0, 0)),
                  pl.BlockSpec((1, 1, CMP_LEN * HEAD_DIM), lambda a, g: (a, 0, 0)),
                  pl.BlockSpec((1, CMP_LEN * HEAD_DIM, HEAD_DIM), lambda a, g: (a, 0, 0)),
                  pl.BlockSpec((1, HEAD_DIM, HEAD_DIM), lambda a, g: (a, 0, 0))],
        out_specs=pl.BlockSpec((1, 1, nseg, HEAD_DIM), lambda a, g: (a, g, 0, 0)),
        out_shape=jax.ShapeDtypeStruct((2, ng, nseg, HEAD_DIM), BF16),
        compiler_params=_cparams(("parallel", "parallel")),
    )(seg, pos, w1, w2)


CS_TQ = 256
SL_NBP = 128


def _split3(x):
    hi = x.astype(BF16)
    r = x - hi.astype(F32)
    mid = r.astype(BF16)
    lo = (r - mid.astype(F32)).astype(BF16)
    return hi, mid, lo


def _topk_mask(score, lane, k):
    sel = jnp.zeros(score.shape, F32)
    width = float(score.shape[1])
    for _ in range(k):
        mx = jnp.max(score, axis=1, keepdims=True)
        idx = jnp.min(jnp.where(score == mx, lane, width), axis=1, keepdims=True)
        hit = lane == idx
        sel = jnp.where(hit, 1.0, sel)
        score = jnp.where(hit, -jnp.inf, score)
    return sel


def _nsa_cmp_kernel(q_ref, kc_ref, vc_ref, map_ref, o_ref, sel_ref, *, n_sel):
    i = pl.program_id(1)
    tq = q_ref.shape[0]
    ncmp = kc_ref.shape[2]
    nbp = map_ref.shape[1]
    t_pos = i * tq + lax.broadcasted_iota(jnp.int32, (tq, 1), 0)
    cmp_end = lax.broadcasted_iota(jnp.int32, (1, ncmp), 1) * CMP_STRIDE + (CMP_LEN - 1)
    mask = cmp_end <= t_pos
    kc = kc_ref[0, 0]
    vc = vc_ref[0, 0]
    psum = jnp.zeros((tq, ncmp), F32)
    for hh in range(B_HPG):
        hs = slice(hh * HEAD_DIM, (hh + 1) * HEAD_DIM)
        s = _dot_nt(q_ref[:, hs], kc) * ATT_SCALE
        s = jnp.where(mask, s, NEG_INF)
        e = jnp.exp(s - jnp.max(s, axis=1, keepdims=True))
        p = jnp.where(mask, e / jnp.sum(e, axis=1, keepdims=True), 0.0)
        o_ref[:, hs] = _dot(p.astype(BF16), vc).astype(o_ref.dtype)
        psum = psum + p
    hi, mid, lo = _split3(psum)
    cmap = map_ref[...]
    p_slc = _dot(hi, cmap) + _dot(mid, cmap) + _dot(lo, cmap)
    jb = lax.broadcasted_iota(jnp.int32, (1, nbp), 1)
    cur = t_pos // SLC_LEN
    valid = jb <= cur
    forced = (jb == 0) | (jb == cur) | (jb == cur - 1)
    score = jnp.where(valid & jnp.logical_not(forced), p_slc, NEG_INF)
    sel = _topk_mask(score, jb.astype(F32), n_sel - N_FORCED)
    sel_ref[0] = jnp.where(valid, jnp.where(forced, 1.0, sel), 0.0).astype(sel_ref.dtype)


def _cmp_to_slc_matrix(ncmp_pad, n_cmp, nb, nbp):
    ratio = SLC_LEN // CMP_STRIDE
    n_over = CMP_LEN // CMP_STRIDE
    mat = np.zeros((ncmp_pad, nbp), np.float32)
    for j in range(nb):
        for m in range(ratio):
            for n in range(n_over):
                c = ratio * j + m - n
                if 0 <= c < n_cmp:
                    mat[c, j] += 1.0
    return mat


def _nsa_cmp_select(z_b, cmp_kv):
    t = z_b.shape[0]
    ng = B_KV_GROUPS
    nseg = cmp_kv.shape[2]
    nb = t // SLC_LEN
    n_cmp = (t - CMP_LEN) // CMP_STRIDE + 1
    nbp = -(-nb // SL_NBP) * SL_NBP
    cmap = jnp.asarray(_cmp_to_slc_matrix(nseg, n_cmp, nb, nbp), BF16)
    gw = B_HPG * HEAD_DIM
    return pl.pallas_call(
        functools.partial(_nsa_cmp_kernel, n_sel=min(N_SLC, nb)),
        grid=(ng, t // CS_TQ),
        in_specs=[pl.BlockSpec((CS_TQ, gw), lambda g, i: (i, g)),
                  pl.BlockSpec((1, 1, nseg, HEAD_DIM), lambda g, i: (0, g, 0, 0)),
                  pl.BlockSpec((1, 1, nseg, HEAD_DIM), lambda g, i: (1, g, 0, 0)),
                  pl.BlockSpec((nseg, nbp), lambda g, i: (0, 0))],
        out_specs=[pl.BlockSpec((CS_TQ, gw), lambda g, i: (i, g)),
                   pl.BlockSpec((1, CS_TQ, nbp), lambda g, i: (g, i, 0))],
        out_shape=[jax.ShapeDtypeStruct((t, B_QW), BF16),
                   jax.ShapeDtypeStruct((ng, t, nbp), BF16)],
        compiler_params=_cparams(("parallel", "parallel")),
    )(z_b, cmp_kv, cmp_kv, cmap)


SL_TQ = 512
SL_TK = SL_TQ
SL_PHASE_TILES = SL_NBP * SLC_LEN // SL_TK


def _nsa_slc_kernel(q_ref, sel_ref, k_ref, v_ref, o_ref, ka_ref, va_ref, qa_ref, s_ref, m_ref, acc_ref, *, n_phase):
    i = pl.program_id(1)
    tq = SL_TQ

    @pl.when(i == 0)
    def _():
        _fill_augmented(k_ref, v_ref, ka_ref, va_ref, SLC_LEN)

    bias = ((sel_ref[0].astype(F32) - 1.0) * (-NEG_INF)).astype(BF16)
    for hh in range(B_HPG):
        rows = slice(hh * tq, (hh + 1) * tq)
        q = (q_ref[:, hh * HEAD_DIM:(hh + 1) * HEAD_DIM].astype(F32) * Q_SCALE).astype(BF16)
        for ph in range(n_phase):
            qa_ref[ph, rows, 0:HEAD_DIM] = q
            qa_ref[ph, rows, HEAD_DIM:] = bias[:, ph * SL_NBP:(ph + 1) * SL_NBP]
    _flash_init(m_ref, None, acc_ref)
    t_pos = i * tq + (lax.broadcasted_iota(jnp.int32, (B_HPG * tq, 1), 0) & (tq - 1))

    def qk(slot, j):
        off = pl.multiple_of(j * SL_TK, SL_TK)
        qa = qa_ref[0] if n_phase == 1 else qa_ref[j // SL_PHASE_TILES]
        s_ref[slot] = _dot_nt(qa, ka_ref[pl.ds(off, SL_TK), :])

    def process(slot, j, causal):
        off = pl.multiple_of(j * SL_TK, SL_TK)
        if causal:
            kpos = off + lax.broadcasted_iota(jnp.int32, (1, SL_TK), 1)
            _mask_scores(s_ref.at[slot], kpos <= t_pos)
        _flash_update(s_ref.at[slot], va_ref[pl.ds(off, SL_TK), :], m_ref, None, acc_ref)

    _flash_sweep((i * tq + tq + SL_TK - 1) // SL_TK, qk, process)
    acc = acc_ref[...]
    o = acc[:, 0:HEAD_DIM] / acc[:, HEAD_DIM:HEAD_DIM + 1]
    for hh in range(B_HPG):
        o_ref[:, hh * HEAD_DIM:(hh + 1) * HEAD_DIM] = o[hh * tq:(hh + 1) * tq].astype(o_ref.dtype)


def _nsa_slc(z_b, sel):
    t = z_b.shape[0]
    nbp = sel.shape[2]
    n_phase = nbp // SL_NBP
    gw = B_HPG * HEAD_DIM
    rows = B_HPG * SL_TQ
    k_col0 = (B_QW + 2 * B_KVW) // HEAD_DIM
    v_col0 = (B_QW + 3 * B_KVW) // HEAD_DIM
    return pl.pallas_call(
        functools.partial(_nsa_slc_kernel, n_phase=n_phase),
        grid=(B_KV_GROUPS, t // SL_TQ),
        in_specs=[pl.BlockSpec((SL_TQ, gw), lambda g, i: (i, g)),
                  pl.BlockSpec((1, SL_TQ, nbp), lambda g, i: (g, i, 0)),
                  _resident((t, HEAD_DIM), lambda g, i: (0, k_col0 + g)),
                  _resident((t, HEAD_DIM), lambda g, i: (0, v_col0 + g))],
        out_specs=pl.BlockSpec((SL_TQ, gw), lambda g, i: (i, g)),
        out_shape=jax.ShapeDtypeStruct((t, B_QW), BF16),
        scratch_shapes=[pltpu.VMEM((t, HEAD_DIM + SL_NBP), BF16),
                        pltpu.VMEM((t, 2 * HEAD_DIM), BF16),
                        pltpu.VMEM((n_phase, rows, HEAD_DIM + SL_NBP), BF16),
                        pltpu.VMEM((2, rows, SL_TK), F32),
                        pltpu.VMEM((rows, LANES), F32),
                        pltpu.VMEM((rows, 2 * HEAD_DIM), F32)],
        compiler_params=_cparams(("arbitrary", "arbitrary")),
    )(z_b, sel, z_b, z_b)


WN_TQ = WIN


def _nsa_win_kernel(q_ref, ka_ref, kb_ref, va_ref, vb_ref, oc_ref, os_ref, gate_ref, o_ref, qs_ref):
    i = pl.program_id(1)
    tq = WN_TQ
    for hh in range(B_HPG):
        q = q_ref[:, hh * HEAD_DIM:(hh + 1) * HEAD_DIM].astype(F32) * Q_SCALE
        qs_ref[hh * tq:(hh + 1) * tq, :] = q.astype(BF16)
    qs = qs_ref[...]
    t_loc = lax.broadcasted_iota(jnp.int32, (B_HPG * tq, 1), 0) & (tq - 1)
    c_loc = lax.broadcasted_iota(jnp.int32, (1, tq), 1)
    t_prev = t_loc + jnp.where(i > 0, 0, tq)
    s_a = jnp.where(c_loc > t_prev, _dot_nt(qs, ka_ref[...]), NEG_INF)
    s_b = jnp.where(c_loc <= t_loc, _dot_nt(qs, kb_ref[...]), NEG_INF)
    m = jnp.maximum(jnp.max(s_a, axis=1, keepdims=True), jnp.max(s_b, axis=1, keepdims=True))
    p_a = jnp.exp2(s_a - m)
    p_b = jnp.exp2(s_b - m)
    l = jnp.sum(p_a, axis=1, keepdims=True) + jnp.sum(p_b, axis=1, keepdims=True)
    o_w = (_dot(p_a.astype(BF16), va_ref[...]) + _dot(p_b.astype(BF16), vb_ref[...])) / l
    gates = _sigmoid(gate_ref[...])
    for hh in range(B_HPG):
        hs = slice(hh * HEAD_DIM, (hh + 1) * HEAD_DIM)
        g_c, g_s, g_w = (gates[:, 3 * hh + c:3 * hh + c + 1] for c in range(3))
        o = (g_c * oc_ref[:, hs].astype(F32) + g_s * os_ref[:, hs].astype(F32)
             + g_w * o_w[hh * tq:(hh + 1) * tq])
        o_ref[:, hs] = o.astype(o_ref.dtype)


def _nsa_win_combine(z_b, o_c, o_s, z_gate):
    t = z_b.shape[0]
    gw = B_HPG * HEAD_DIM
    k_col0 = (B_QW + 4 * B_KVW) // HEAD_DIM
    v_col0 = (B_QW + 5 * B_KVW) // HEAD_DIM
    prev_tile = lambda col0: (lambda g, i: (jnp.maximum(i - 1, 0), col0 + g))
    this_tile = lambda col0: (lambda g, i: (i, col0 + g))
    kv_block = (WN_TQ, HEAD_DIM)
    return pl.pallas_call(
        _nsa_win_kernel,
        grid=(B_KV_GROUPS, t // WN_TQ),
        in_specs=[pl.BlockSpec((WN_TQ, gw), lambda g, i: (i, g)),
                  pl.BlockSpec(kv_block, prev_tile(k_col0)),
                  pl.BlockSpec(kv_block, this_tile(k_col0)),
                  pl.BlockSpec(kv_block, prev_tile(v_col0)),
                  pl.BlockSpec(kv_block, this_tile(v_col0)),
                  pl.BlockSpec((WN_TQ, gw), lambda g, i: (i, g)),
                  pl.BlockSpec((WN_TQ, gw), lambda g, i: (i, g)),
                  pl.BlockSpec((WN_TQ, HEAD_DIM), lambda g, i: (i, g))],
        out_specs=pl.BlockSpec((WN_TQ, gw), lambda g, i: (i, g)),
        out_shape=jax.ShapeDtypeStruct((t, B_QW), BF16),
        scratch_shapes=[pltpu.VMEM((B_HPG * WN_TQ, HEAD_DIM), BF16)],
        compiler_params=_cparams(("parallel", "parallel")),
    )(z_b, z_b, z_b, z_b, z_b, o_c, o_s, z_gate)


def _nsa(z_b, z_gate, pos_k, w1k, w2k, pos_v, w1v, w2v):
    t = z_b.shape[0]
    ng = B_KV_GROUPS

    def segs(col0):
        z = z_b[:, col0:col0 + B_KVW].reshape(t, ng, HEAD_DIM).transpose(1, 0, 2)
        return z.reshape(ng, t // CMP_STRIDE, SEG_W)

    seg = jnp.stack([segs(B_QW), segs(B_QW + B_KVW)])
    pos = jnp.stack([pos_k, pos_v]).reshape(2, 1, CMP_LEN * HEAD_DIM).astype(BF16)
    w1 = jnp.stack([w1k, w1v]).astype(BF16)
    w2 = jnp.stack([w2k, w2v]).astype(BF16)
    cmp_kv = _nsa_compress(seg, pos, w1, w2)
    o_c, sel = _nsa_cmp_select(z_b, cmp_kv)
    o_s = _nsa_slc(z_b, sel)
    return _nsa_win_combine(z_b, o_c, o_s, z_gate)


KM_ROWS = 8


def _kmean_kernel(k_ref, o_ref):
    x = k_ref[...].astype(F32)
    o_ref[...] = jnp.mean(x.reshape(KM_ROWS, MOBA_BLOCK, x.shape[1]), axis=1)


def _moba_kmean(z_cd):
    t = z_cd.shape[0]
    nbm = t // MOBA_BLOCK
    return pl.pallas_call(
        _kmean_kernel,
        grid=(nbm // KM_ROWS,),
        in_specs=[pl.BlockSpec((KM_ROWS * MOBA_BLOCK, C_W), lambda i: (i, 1))],
        out_specs=pl.BlockSpec((KM_ROWS, C_W), lambda i: (i, 0)),
        out_shape=jax.ShapeDtypeStruct((nbm, C_W), F32),
        compiler_params=_cparams(("parallel",)),
    )(z_cd)


MB_TQ = 1024
MB_TK = MB_TQ // 2
MB_NBP = 128


def _moba_kernel(q_ref, km_ref, k_ref, v_ref, o_ref, ka_ref, va_ref, qa_ref, s_ref, m_ref, acc_ref):
    i = pl.program_id(1)
    tq = MB_TQ

    @pl.when(i == 0)
    def _():
        _fill_augmented(k_ref, v_ref, ka_ref, va_ref, MOBA_BLOCK)

    q = q_ref[...]
    km = km_ref[...]
    km_hi = km.astype(BF16)
    km_lo = (km - km_hi.astype(F32)).astype(BF16)
    gate = _dot_nt(q, km_hi) + _dot_nt(q, km_lo)
    jb = lax.broadcasted_iota(jnp.int32, (1, MB_NBP), 1)
    t_pos = i * tq + lax.broadcasted_iota(jnp.int32, (tq, 1), 0)
    cur = t_pos // MOBA_BLOCK
    earlier = jb < cur
    sel = _topk_mask(jnp.where(earlier, gate, NEG_INF), jb.astype(F32), MOBA_TOPK)
    sel = jnp.where(jb == cur, 1.0, jnp.where(earlier, sel, 0.0))
    qa_ref[:, 0:HEAD_DIM] = (q.astype(F32) * Q_SCALE).astype(BF16)
    qa_ref[:, HEAD_DIM:] = ((sel - 1.0) * (-NEG_INF)).astype(BF16)
    _flash_init(m_ref, None, acc_ref)

    def qk(slot, j):
        off = pl.multiple_of(j * MB_TK, MB_TK)
        s_ref[slot] = _dot_nt(qa_ref[...], ka_ref[pl.ds(off, MB_TK), :])

    def process(slot, j, causal):
        off = pl.multiple_of(j * MB_TK, MB_TK)
        if causal:
            kpos = off + lax.broadcasted_iota(jnp.int32, (1, MB_TK), 1)
            _mask_scores(s_ref.at[slot], kpos <= t_pos)
        _flash_update(s_ref.at[slot], va_ref[pl.ds(off, MB_TK), :], m_ref, None, acc_ref)

    _flash_sweep((i * tq + tq + MB_TK - 1) // MB_TK, qk, process)
    acc = acc_ref[...]
    o_ref[...] = (acc[:, 0:HEAD_DIM] / acc[:, HEAD_DIM:HEAD_DIM + 1]).astype(o_ref.dtype)


def _moba(z_cd):
    t = z_cd.shape[0]
    nbm = t // MOBA_BLOCK
    nh = C_HEADS
    k_mean = _moba_kmean(z_cd)
    k_mean = jnp.pad(k_mean, ((0, MB_NBP - nbm), (0, 0)))
    return pl.pallas_call(
        _moba_kernel,
        grid=(nh, t // MB_TQ),
        in_specs=[pl.BlockSpec((MB_TQ, HEAD_DIM), lambda h, i: (i, h)),
                  pl.BlockSpec((MB_NBP, HEAD_DIM), lambda h, i: (0, h)),
                  _resident((t, HEAD_DIM), lambda h, i: (0, nh + h)),
                  _resident((t, HEAD_DIM), lambda h, i: (0, 2 * nh + h))],
        out_specs=pl.BlockSpec((MB_TQ, HEAD_DIM), lambda h, i: (i, h)),
        out_shape=jax.ShapeDtypeStruct((t, C_W), BF16),
        scratch_shapes=[pltpu.VMEM((t, HEAD_DIM + MB_NBP), BF16),
                        pltpu.VMEM((t, 2 * HEAD_DIM), BF16),
                        pltpu.VMEM((MB_TQ, HEAD_DIM + MB_NBP), BF16),
                        pltpu.VMEM((2, MB_TQ, MB_TK), F32),
                        pltpu.VMEM((MB_TQ, LANES), F32),
                        pltpu.VMEM((MB_TQ, 2 * HEAD_DIM), F32)],
        compiler_params=_cparams(("arbitrary", "arbitrary")),
    )(z_cd, k_mean, z_cd, z_cd)


DF_TQ = 1024
DF_TK = DF_TQ // 2


def _diff_kernel(q_ref, k_ref, v_ref, lq1_ref, lk1_ref, lq2_ref, lk2_ref, nw_ref, o_ref,
                 qs_ref, s_ref, m_ref, l_ref, acc_ref, *, lam_init):
    i = pl.program_id(1)
    tq = DF_TQ
    for mp in range(2):
        q = q_ref[:, mp * D_DK:(mp + 1) * D_DK].astype(F32) * (D_DK ** -0.5 * LOG2E)
        qs_ref[mp] = q.astype(BF16)
        _flash_init(m_ref.at[mp], l_ref.at[mp], acc_ref.at[mp])
    t_pos = i * tq + lax.broadcasted_iota(jnp.int32, (tq, 1), 0)

    def qk(slot, j):
        off = pl.multiple_of(j * DF_TK, DF_TK)
        for mp in range(2):
            s_ref[slot, mp] = _dot_nt(qs_ref[mp], k_ref[pl.ds(off, DF_TK), mp * D_DK:(mp + 1) * D_DK])

    def process(slot, j, causal):
        off = pl.multiple_of(j * DF_TK, DF_TK)
        v = v_ref[pl.ds(off, DF_TK), :]
        for mp in range(2):
            if causal:
                kpos = off + lax.broadcasted_iota(jnp.int32, (1, DF_TK), 1)
                _mask_scores(s_ref.at[slot, mp], kpos <= t_pos)
            _flash_update(s_ref.at[slot, mp], v, m_ref.at[mp], l_ref.at[mp], acc_ref.at[mp])

    _flash_sweep((i * tq + tq + DF_TK - 1) // DF_TK, qk, process)
    lam = (jnp.exp(jnp.sum(lq1_ref[...] * lk1_ref[...], axis=1, keepdims=True))
           - jnp.exp(jnp.sum(lq2_ref[...] * lk2_ref[...], axis=1, keepdims=True)) + lam_init)
    o = (acc_ref[0] / _lane_tile(l_ref[0], D_DV)
         - lam * (acc_ref[1] / _lane_tile(l_ref[1], D_DV)))
    o = o * lax.rsqrt(jnp.mean(o * o, axis=-1, keepdims=True) + RMS_EPS) * nw_ref[...]
    o_ref[...] = (o * (1.0 - lam_init)).astype(o_ref.dtype)


def _diff_attn(z_cd, lq1, lk1, lq2, lk2, subln_w, layer_idx):
    t = z_cd.shape[0]
    lam_init = 0.8 - 0.6 * math.exp(-0.3 * layer_idx)
    q_col0 = 3 * C_W // D_DV
    k_col0 = (3 * C_W + D_QW) // D_DV
    v_col0 = (3 * C_W + 2 * D_QW) // D_DV
    vec = pl.BlockSpec((1, D_DK), lambda h, i: (0, 0))
    return pl.pallas_call(
        functools.partial(_diff_kernel, lam_init=lam_init),
        grid=(D_HEADS, t // DF_TQ),
        in_specs=[pl.BlockSpec((DF_TQ, 2 * D_DK), lambda h, i: (i, q_col0 + h)),
                  _resident((t, 2 * D_DK), lambda h, i: (0, k_col0 + h)),
                  _resident((t, D_DV), lambda h, i: (0, v_col0 + h)),
                  vec, vec, vec, vec,
                  pl.BlockSpec((1, D_DV), lambda h, i: (0, 0))],
        out_specs=pl.BlockSpec((DF_TQ, D_DV), lambda h, i: (i, h)),
        out_shape=jax.ShapeDtypeStruct((t, D_VW), BF16),
        scratch_shapes=[pltpu.VMEM((2, DF_TQ, D_DK), BF16),
                        pltpu.VMEM((2, 2, DF_TQ, DF_TK), F32),
                        pltpu.VMEM((2, DF_TQ, LANES), F32),
                        pltpu.VMEM((2, DF_TQ, LANES), F32),
                        pltpu.VMEM((2, DF_TQ, D_DV), F32)],
        compiler_params=_cparams(("parallel", "arbitrary")),
    )(z_cd, z_cd, z_cd, lq1.reshape(1, D_DK), lk1.reshape(1, D_DK), lq2.reshape(1, D_DK),
      lk2.reshape(1, D_DK), subln_w.reshape(1, D_DV))


def _gate_weight(w_gate):
    d = w_gate.shape[0]
    per = 3 * B_HPG
    wg = w_gate.reshape(d, B_KV_GROUPS, per)
    wg = jnp.pad(wg, ((0, 0), (0, 0), (0, HEAD_DIM - per)))
    return wg.reshape(d, B_KV_GROUPS * HEAD_DIM)


def kernel(x, ln_w, ln_b, ffn_w_in, ffn_w_out, ab_w_in, ab_w_out, hgrn_lower_bounds, hgrn_norm_w,
           nsa_cmp_pos_k, nsa_cmp_k_w1, nsa_cmp_k_w2, nsa_cmp_pos_v, nsa_cmp_v_w1, nsa_cmp_v_w2,
           cd_w_in, cd_w_out, diff_lambda_q1, diff_lambda_k1, diff_lambda_q2, diff_lambda_k2, diff_subln_w):
    bsz, t, d = x.shape
    lb_all = jnp.cumsum(jax.nn.softmax(hgrn_lower_bounds.astype(F32), axis=0), axis=0)
    outs = []
    for bi in range(bsz):
        h = x[bi]
        h_bf = h.astype(BF16)
        for layer in range(DEPTH):
            if layer % 2 == 0:
                e = layer // 2
                w = ab_w_in[e]
                n_a = 4 * A_W
                n_b = B_QW + 6 * B_KVW
                z_a = _matmul(h_bf, w[:, :n_a].astype(BF16), F32)
                z_b = _matmul(h_bf, w[:, n_a:n_a + n_b].astype(BF16), BF16)
                z_g = _matmul(h_bf, _gate_weight(w[:, n_a + n_b:]).astype(BF16), F32)
                o_a = _hgrn2(z_a, lb_all[layer], hgrn_norm_w[e])
                o_b = _nsa(z_b, z_g, nsa_cmp_pos_k[e], nsa_cmp_k_w1[e], nsa_cmp_k_w2[e],
                           nsa_cmp_pos_v[e], nsa_cmp_v_w1[e], nsa_cmp_v_w2[e])
                mix_in = jnp.concatenate([o_a, o_b], axis=-1)
                w_out = ab_w_out[e]
            else:
                oi = layer // 2
                z_cd = _matmul(h_bf, cd_w_in[oi].astype(BF16), BF16)
                o_c = _moba(z_cd)
                o_d = _diff_attn(z_cd, diff_lambda_q1[oi], diff_lambda_k1[oi], diff_lambda_q2[oi],
                                 diff_lambda_k2[oi], diff_subln_w[oi], layer)
                mix_in = jnp.concatenate([o_c, o_d], axis=-1)
                w_out = cd_w_out[oi]
            h, h_bf = _mm_res_ln(mix_in, w_out.astype(BF16), h, ln_w[layer, 0], ln_b[layer, 0])
            act = _ffn_in(h_bf, ffn_w_in[layer].astype(BF16))
            h, h_bf = _mm_res_ln(act, ffn_w_out[layer].astype(BF16), h, ln_w[layer, 1], ln_b[layer, 1])
        outs.append(h)
    return jnp.stack(outs)
```

```python
import functools
import math

import numpy as np
import jax
import jax.numpy as jnp
from jax import lax
from jax.experimental import pallas as pl
from jax.experimental.pallas import tpu as pltpu

F32 = jnp.float32
BF16 = jnp.bfloat16

D_MODEL = 2048
DEPTH = 2
HEAD_DIM = 128
A_HEADS = 8
A_CHUNK = 64
B_HEADS = 8
B_KV_GROUPS = 2
B_HPG = B_HEADS // B_KV_GROUPS
CMP_LEN = 32
CMP_STRIDE = 16
SLC_LEN = 64
N_SLC = 16
N_FORCED = 3
WIN = 512
C_HEADS = 8
MOBA_BLOCK = 256
MOBA_TOPK = 3
D_HEADS = 4
D_DK = 128
D_DV = 2 * D_DK
D_FF = ((8 * D_MODEL + 3 * 256 - 1) // (3 * 256)) * 256

DEEPNORM_ALPHA = (2 * DEPTH) ** 0.25
NEG_INF = -1e30
FORCE_SCORE = 1e9
LN_EPS = 1e-5
RMS_EPS = 1e-6
ATT_SCALE = HEAD_DIM ** -0.5

A_W = A_HEADS * HEAD_DIM
B_QW = B_HEADS * HEAD_DIM
B_KVW = B_KV_GROUPS * HEAD_DIM
C_W = C_HEADS * HEAD_DIM
D_QW = D_HEADS * 2 * D_DK
D_VW = D_HEADS * D_DV

VMEM_LIMIT = 56 * 1024 * 1024
MM_SMALL_WEIGHT_BYTES = 8 * 1024 * 1024


def _cparams(sem):
    return pltpu.CompilerParams(dimension_semantics=sem, vmem_limit_bytes=VMEM_LIMIT)


def _dot(a, b):
    return jnp.dot(a, b, preferred_element_type=F32)


def _dot_nt(a, b):
    return lax.dot_general(a, b, (((1,), (1,)), ((), ())), preferred_element_type=F32)


def _dot_tn(a, b):
    return lax.dot_general(a, b, (((0,), (0,)), ((), ())), preferred_element_type=F32)


def _sigmoid(x):
    return 1.0 / (1.0 + jnp.exp(-x))


def _silu(x):
    return x * _sigmoid(x)


MM_TM = 1024
MM_TN = 512


def _mm_kernel(x_ref, w_ref, o_ref, wb_ref):
    @pl.when(pl.program_id(1) == 0)
    def _():
        wb_ref[...] = w_ref[0].astype(BF16)

    o_ref[...] = _dot(x_ref[...].astype(BF16), wb_ref[...]).astype(o_ref.dtype)


def _matmul(x, w, widx, col0, n, out_dtype):
    m, k = x.shape
    tn = min(MM_TN, n)
    j0 = col0 // tn
    return pl.pallas_call(
        _mm_kernel,
        grid=(n // tn, m // MM_TM),
        in_specs=[pl.BlockSpec((MM_TM, k), lambda j, i: (i, 0)),
                  pl.BlockSpec((1, k, tn), lambda j, i: (widx, 0, j0 + j))],
        out_specs=pl.BlockSpec((MM_TM, tn), lambda j, i: (i, j)),
        out_shape=jax.ShapeDtypeStruct((m, n), out_dtype),
        scratch_shapes=[pltpu.VMEM((k, tn), BF16)],
        compiler_params=_cparams(("arbitrary", "arbitrary")),
    )(x, w)


def _ffn_in_kernel(x_ref, wg_ref, wu_ref, o_ref, wgb_ref, wub_ref):
    @pl.when(pl.program_id(1) == 0)
    def _():
        wgb_ref[...] = wg_ref[0].astype(BF16)
        wub_ref[...] = wu_ref[0].astype(BF16)

    x = x_ref[...]
    g = _dot(x, wgb_ref[...])
    u = _dot(x, wub_ref[...])
    o_ref[...] = (_silu(g) * u).astype(o_ref.dtype)


def _ffn_in(x, w_in, layer):
    m, k = x.shape
    dff = w_in.shape[2] // 2
    nj = dff // MM_TN
    return pl.pallas_call(
        _ffn_in_kernel,
        grid=(nj, m // MM_TM),
        in_specs=[pl.BlockSpec((MM_TM, k), lambda j, i: (i, 0)),
                  pl.BlockSpec((1, k, MM_TN), lambda j, i: (layer, 0, j)),
                  pl.BlockSpec((1, k, MM_TN), lambda j, i: (layer, 0, j + nj))],
        out_specs=pl.BlockSpec((MM_TM, MM_TN), lambda j, i: (i, j)),
        out_shape=jax.ShapeDtypeStruct((m, dff), BF16),
        scratch_shapes=[pltpu.VMEM((k, MM_TN), BF16), pltpu.VMEM((k, MM_TN), BF16)],
        compiler_params=_cparams(("arbitrary", "arbitrary")),
    )(x, w_in, w_in)


def _layer_norm(y, w, b):
    mu = jnp.mean(y, axis=-1, keepdims=True)
    d = y - mu
    var = jnp.mean(d * d, axis=-1, keepdims=True)
    return d * lax.rsqrt(var + LN_EPS) * w + b


def _mm_res_ln_kernel(*refs, widths):
    part_refs = refs[:len(widths)]
    w_ref, h_ref, lnw_ref, lnb_ref, o_ref, obf_ref = refs[len(widths):]
    y = DEEPNORM_ALPHA * h_ref[...]
    k0 = 0
    for a_ref, width in zip(part_refs, widths):
        y = y + _dot(a_ref[...], w_ref[k0:k0 + width, :])
        k0 += width
    out = _layer_norm(y, lnw_ref[...], lnb_ref[...])
    o_ref[...] = out
    obf_ref[...] = out.astype(BF16)


def _mm_res_ln(parts, w, h, ln_w, ln_b):
    m = parts[0].shape[0]
    widths = tuple(a.shape[1] for a in parts)
    kdim, n = w.shape
    tm = 512 if kdim * n * 2 <= MM_SMALL_WEIGHT_BYTES else 256
    return pl.pallas_call(
        functools.partial(_mm_res_ln_kernel, widths=widths),
        grid=(m // tm,),
        in_specs=[pl.BlockSpec((tm, width), lambda i: (i, 0)) for width in widths]
        + [_resident((kdim, n), lambda i: (0, 0)),
           pl.BlockSpec((tm, n), lambda i: (i, 0)),
           pl.BlockSpec((1, n), lambda i: (0, 0)),
           pl.BlockSpec((1, n), lambda i: (0, 0))],
        out_specs=[pl.BlockSpec((tm, n), lambda i: (i, 0)),
                   pl.BlockSpec((tm, n), lambda i: (i, 0))],
        out_shape=[jax.ShapeDtypeStruct((m, n), F32), jax.ShapeDtypeStruct((m, n), BF16)],
        compiler_params=_cparams(("parallel",)),
    )(*parts, w, h, ln_w.reshape(1, n), ln_b.reshape(1, n))


LOG2E = math.log2(math.e)
LANES = 128
Q_SCALE = ATT_SCALE * LOG2E


def _flash_init(m_ref, l_ref, acc_ref):
    m_ref[...] = jnp.full(m_ref.shape, -jnp.inf, F32)
    if l_ref is not None:
        l_ref[...] = jnp.zeros(l_ref.shape, F32)
    acc_ref[...] = jnp.zeros(acc_ref.shape, F32)


def _flash_update(s_ref, v, m_ref, l_ref, acc_ref):
    m_prev = m_ref[...]
    m_new = jnp.maximum(m_prev, jnp.max(s_ref[...], axis=1, keepdims=True))
    m_ref[...] = m_new
    alpha = jnp.exp2(m_prev - m_new)
    p = jnp.exp2(s_ref[...] - _lane_tile(m_new, s_ref.shape[1]))
    if l_ref is not None:
        l_ref[...] = alpha * l_ref[...] + jnp.sum(p, axis=1, keepdims=True)
    acc_ref[...] = _lane_tile(alpha, acc_ref.shape[1]) * acc_ref[...] + _dot(p.astype(BF16), v)


def _mask_scores(s_ref, visible):
    s_ref[...] = jnp.where(visible, s_ref[...], NEG_INF)


def _lane_tile(x, width):
    reps = width // x.shape[1]
    return x if reps == 1 else jnp.concatenate([x] * reps, axis=1)


def _flash_sweep(n_tiles, qk, process):
    n_pairs = (n_tiles + 1) // 2
    qk(0, 0)

    def pair(jj, carry):
        j = 2 * jj
        qk(1, j + 1)
        process(0, j, False)
        qk(0, j + 2)
        process(1, j + 1, False)
        return carry

    lax.fori_loop(0, n_pairs - 1, pair, 0)
    j_last = 2 * (n_pairs - 1)
    qk(1, j_last + 1)
    process(0, j_last, True)
    process(1, j_last + 1, True)


AUG_FILL_ROWS = 512


def _fill_augmented(k_ref, v_ref, ka_ref, va_ref, block_len):
    lane = lax.broadcasted_iota(jnp.int32, (AUG_FILL_ROWS, LANES), 1)
    ones_col = jnp.where(lane == 0, 1.0, 0.0).astype(BF16)

    def fill(c, carry):
        r0 = pl.multiple_of(c * AUG_FILL_ROWS, AUG_FILL_ROWS)
        rows = pl.ds(r0, AUG_FILL_ROWS)
        blk = (r0 + lax.broadcasted_iota(jnp.int32, (AUG_FILL_ROWS, LANES), 0)) // block_len
        ka_ref[rows, 0:HEAD_DIM] = k_ref[rows, :]
        ka_ref[rows, HEAD_DIM:] = jnp.where((blk & (LANES - 1)) == lane, 1.0, 0.0).astype(BF16)
        va_ref[rows, 0:HEAD_DIM] = v_ref[rows, :]
        va_ref[rows, HEAD_DIM:] = ones_col
        return carry

    lax.fori_loop(0, k_ref.shape[0] // AUG_FILL_ROWS, fill, 0)


def _resident(block_shape, index_map):
    return pl.BlockSpec(block_shape, index_map, pipeline_mode=pl.Buffered(1))


HG_TILE = 512


def _hgrn_kernel(q_ref, f_ref, i_ref, g_ref, lb_ref, nw_ref, o_ref, st_ref):
    @pl.when(pl.program_id(1) == 0)
    def _():
        st_ref[...] = jnp.zeros_like(st_ref)

    lb = lb_ref[...]
    f = lb + (1.0 - lb) * _sigmoid(f_ref[...])
    logf = jnp.log(f)
    kk = 1.0 - f
    qf = _silu(q_ref[...])
    r64 = lax.broadcasted_iota(jnp.int32, logf.shape, 0) & (A_CHUNK - 1)
    b = logf
    step = 1
    while step < A_CHUNK:
        b = b + jnp.where(r64 >= step, pltpu.roll(b, step, 0), 0.0)
        step *= 2
    causal = (lax.broadcasted_iota(jnp.int32, (A_CHUNK, A_CHUNK), 0)
              >= lax.broadcasted_iota(jnp.int32, (A_CHUNK, A_CHUNK), 1))
    nw = nw_ref[...]
    for c in range(HG_TILE // A_CHUNK):
        sl = slice(c * A_CHUNK, (c + 1) * A_CHUNK)
        bc = b[sl]
        b_last = bc[A_CHUNK - 1:A_CHUNK, :]
        q_t = (qf[sl] * jnp.exp(bc)).astype(BF16)
        k_t = (kk[sl] * jnp.exp(-bc)).astype(BF16)
        vc = i_ref[sl, :].astype(BF16)
        att = jnp.where(causal, _dot_nt(q_t, k_t), 0.0)
        st = st_ref[...]
        o = _dot(att.astype(BF16), vc) + _dot_nt(q_t, st.astype(BF16))
        kdec = (kk[sl] * jnp.exp(b_last - bc)).astype(BF16)
        st_ref[...] = st * jnp.exp(b_last) + _dot_tn(vc, kdec)
        o = o * lax.rsqrt(jnp.mean(o * o, axis=-1, keepdims=True) + RMS_EPS) * nw
        o_ref[sl, :] = (o * _silu(g_ref[sl, :])).astype(o_ref.dtype)


def _hgrn2(z_a, lb, norm_w):
    t = z_a.shape[0]
    nh = A_HEADS
    col = lambda base: (lambda h, i: (i, base + h))
    return pl.pallas_call(
        _hgrn_kernel,
        grid=(nh, t // HG_TILE),
        in_specs=[pl.BlockSpec((HG_TILE, HEAD_DIM), col(0)),
                  pl.BlockSpec((HG_TILE, HEAD_DIM), col(nh)),
                  pl.BlockSpec((HG_TILE, HEAD_DIM), col(2 * nh)),
                  pl.BlockSpec((HG_TILE, HEAD_DIM), col(3 * nh)),
                  pl.BlockSpec((1, HEAD_DIM), lambda h, i: (0, h)),
                  pl.BlockSpec((1, HEAD_DIM), lambda h, i: (0, 0))],
        out_specs=pl.BlockSpec((HG_TILE, HEAD_DIM), lambda h, i: (i, h)),
        out_shape=jax.ShapeDtypeStruct((t, A_W), BF16),
        scratch_shapes=[pltpu.VMEM((HEAD_DIM, HEAD_DIM), F32)],
        compiler_params=_cparams(("parallel", "arbitrary")),
    )(z_a, z_a, z_a, z_a, lb.reshape(1, A_W), norm_w.reshape(1, HEAD_DIM))


SEG_W = CMP_STRIDE * HEAD_DIM


def _nsa_compress_kernel(seg_ref, pos_ref, w1_ref, w2_ref, o_ref):
    seg = seg_ref[0, 0]
    nseg = seg.shape[0]
    a = _dot(seg, w1_ref[0, :SEG_W, :])
    b = _dot(seg, w1_ref[0, SEG_W:, :])
    b_next = pltpu.roll(b, nseg - 1, 0)
    pos = jnp.broadcast_to(pos_ref[0], (8, CMP_LEN * HEAD_DIM))
    c = _dot(pos, w1_ref[0])[0:1, :]
    hid = _silu(a + b_next + c)
    o_ref[0, 0] = _dot(hid.astype(BF16), w2_ref[0]).astype(o_ref.dtype)


def _nsa_compress(seg, pos, w1, w2):
    _, ng, nseg, _ = seg.shape
    return pl.pallas_call(
        _nsa_compress_kernel,
        grid=(2, ng),
        in_specs=[pl.BlockSpec((1, 1, nseg, SEG_W), lambda a, g: (a, g, 0, 0)),
                  pl.BlockSpec((1, 1, CMP_LEN * HEAD_DIM), lambda a, g: (a, 0, 0)),
                  pl.BlockSpec((1, CMP_LEN * HEAD_DIM, HEAD_DIM), lambda a, g: (a, 0, 0)),
                  pl.BlockSpec((1, HEAD_DIM, HEAD_DIM), lambda a, g: (a, 0, 0))],
        out_specs=pl.BlockSpec((1, 1, nseg, HEAD_DIM), lambda a, g: (a, g, 0, 0)),
        out_shape=jax.ShapeDtypeStruct((2, ng, nseg, HEAD_DIM), BF16),
        compiler_params=_cparams(("parallel", "parallel")),
    )(seg, pos, w1, w2)


CS_TQ = 256
SL_NBP = 128


def _split3(x):
    hi = x.astype(BF16)
    r = x - hi.astype(F32)
    mid = r.astype(BF16)
    lo = (r - mid.astype(F32)).astype(BF16)
    return hi, mid, lo


def _topk_mask(score, lane, k):
    sel = jnp.zeros(score.shape, F32)
    width = float(score.shape[1])
    for _ in range(k):
        mx = jnp.max(score, axis=1, keepdims=True)
        idx = jnp.min(jnp.where(score == mx, lane, width), axis=1, keepdims=True)
        hit = lane == idx
        sel = jnp.where(hit, 1.0, sel)
        score = jnp.where(hit, -jnp.inf, score)
    return sel


def _nsa_cmp_kernel(q_ref, kc_ref, vc_ref, map_ref, o_ref, sel_ref, *, n_sel):
    i = pl.program_id(1)
    tq = q_ref.shape[0]
    ncmp = kc_ref.shape[2]
    nbp = map_ref.shape[1]
    t_pos = i * tq + lax.broadcasted_iota(jnp.int32, (tq, 1), 0)
    cmp_end = lax.broadcasted_iota(jnp.int32, (1, ncmp), 1) * CMP_STRIDE + (CMP_LEN - 1)
    mask = cmp_end <= t_pos
    kc = kc_ref[0, 0]
    vc = vc_ref[0, 0]
    psum = jnp.zeros((tq, ncmp), F32)
    for hh in range(B_HPG):
        hs = slice(hh * HEAD_DIM, (hh + 1) * HEAD_DIM)
        s = _dot_nt(q_ref[:, hs], kc) * ATT_SCALE
        s = jnp.where(mask, s, NEG_INF)
        e = jnp.exp(s - jnp.max(s, axis=1, keepdims=True))
        p = jnp.where(mask, e / jnp.sum(e, axis=1, keepdims=True), 0.0)
        o_ref[:, hs] = _dot(p.astype(BF16), vc).astype(o_ref.dtype)
        psum = psum + p
    hi, mid, lo = _split3(psum)
    cmap = map_ref[...]
    p_slc = _dot(hi, cmap) + _dot(mid, cmap) + _dot(lo, cmap)
    jb = lax.broadcasted_iota(jnp.int32, (1, nbp), 1)
    cur = t_pos // SLC_LEN
    valid = jb <= cur
    forced = (jb == 0) | (jb == cur) | (jb == cur - 1)
    score = jnp.where(valid & jnp.logical_not(forced), p_slc, NEG_INF)
    sel = _topk_mask(score, jb.astype(F32), n_sel - N_FORCED)
    sel_ref[0] = jnp.where(valid, jnp.where(forced, 1.0, sel), 0.0).astype(sel_ref.dtype)


def _cmp_to_slc_matrix(ncmp_pad, n_cmp, nb, nbp):
    ratio = SLC_LEN // CMP_STRIDE
    n_over = CMP_LEN // CMP_STRIDE
    mat = np.zeros((ncmp_pad, nbp), np.float32)
    for j in range(nb):
        for m in range(ratio):
            for n in range(n_over):
                c = ratio * j + m - n
                if 0 <= c < n_cmp:
                    mat[c, j] += 1.0
    return mat


def _nsa_cmp_select(z_b, cmp_kv):
    t = z_b.shape[0]
    ng = B_KV_GROUPS
    nseg = cmp_kv.shape[2]
    nb = t // SLC_LEN
    n_cmp = (t - CMP_LEN) // CMP_STRIDE + 1
    nbp = -(-nb // SL_NBP) * SL_NBP
    cmap = jnp.asarray(_cmp_to_slc_matrix(nseg, n_cmp, nb, nbp), BF16)
    gw = B_HPG * HEAD_DIM
    return pl.pallas_call(
        functools.partial(_nsa_cmp_kernel, n_sel=min(N_SLC, nb)),
        grid=(ng, t // CS_TQ),
        in_specs=[pl.BlockSpec((CS_TQ, gw), lambda g, i: (i, g)),
                  pl.BlockSpec((1, 1, nseg, HEAD_DIM), lambda g, i: (0, g, 0, 0)),
                  pl.BlockSpec((1, 1, nseg, HEAD_DIM), lambda g, i: (1, g, 0, 0)),
                  pl.BlockSpec((nseg, nbp), lambda g, i: (0, 0))],
        out_specs=[pl.BlockSpec((CS_TQ, gw), lambda g, i: (i, g)),
                   pl.BlockSpec((1, CS_TQ, nbp), lambda g, i: (g, i, 0))],
        out_shape=[jax.ShapeDtypeStruct((t, B_QW), BF16),
                   jax.ShapeDtypeStruct((ng, t, nbp), BF16)],
        compiler_params=_cparams(("parallel", "parallel")),
    )(z_b, cmp_kv, cmp_kv, cmap)


SL_TQ = 512
SL_TK = SL_TQ
SL_PHASE_TILES = SL_NBP * SLC_LEN // SL_TK


def _nsa_slc_kernel(q_ref, sel_ref, k_ref, v_ref, o_ref, ka_ref, va_ref, qa_ref, s_ref, m_ref, acc_ref, *, n_phase):
    i = pl.program_id(1)
    tq = SL_TQ

    @pl.when(i == 0)
    def _():
        _fill_augmented(k_ref, v_ref, ka_ref, va_ref, SLC_LEN)

    bias = ((sel_ref[0].astype(F32) - 1.0) * (-NEG_INF)).astype(BF16)
    for hh in range(B_HPG):
        rows = slice(hh * tq, (hh + 1) * tq)
        q = (q_ref[:, hh * HEAD_DIM:(hh + 1) * HEAD_DIM].astype(F32) * Q_SCALE).astype(BF16)
        for ph in range(n_phase):
            qa_ref[ph, rows, 0:HEAD_DIM] = q
            qa_ref[ph, rows, HEAD_DIM:] = bias[:, ph * SL_NBP:(ph + 1) * SL_NBP]
    _flash_init(m_ref, None, acc_ref)
    t_pos = i * tq + (lax.broadcasted_iota(jnp.int32, (B_HPG * tq, 1), 0) & (tq - 1))

    def qk(slot, j):
        off = pl.multiple_of(j * SL_TK, SL_TK)
        qa = qa_ref[0] if n_phase == 1 else qa_ref[j // SL_PHASE_TILES]
        s_ref[slot] = _dot_nt(qa, ka_ref[pl.ds(off, SL_TK), :])

    def process(slot, j, causal):
        off = pl.multiple_of(j * SL_TK, SL_TK)
        if causal:
            kpos = off + lax.broadcasted_iota(jnp.int32, (1, SL_TK), 1)
            _mask_scores(s_ref.at[slot], kpos <= t_pos)
        _flash_update(s_ref.at[slot], va_ref[pl.ds(off, SL_TK), :], m_ref, None, acc_ref)

    _flash_sweep((i * tq + tq + SL_TK - 1) // SL_TK, qk, process)
    acc = acc_ref[...]
    o = acc[:, 0:HEAD_DIM] / acc[:, HEAD_DIM:HEAD_DIM + 1]
    for hh in range(B_HPG):
        o_ref[:, hh * HEAD_DIM:(hh + 1) * HEAD_DIM] = o[hh * tq:(hh + 1) * tq].astype(o_ref.dtype)


def _nsa_slc(z_b, sel):
    t = z_b.shape[0]
    nbp = sel.shape[2]
    n_phase = nbp // SL_NBP
    gw = B_HPG * HEAD_DIM
    rows = B_HPG * SL_TQ
    k_col0 = (B_QW + 2 * B_KVW) // HEAD_DIM
    v_col0 = (B_QW + 3 * B_KVW) // HEAD_DIM
    return pl.pallas_call(
        functools.partial(_nsa_slc_kernel, n_phase=n_phase),
        grid=(B_KV_GROUPS, t // SL_TQ),
        in_specs=[pl.BlockSpec((SL_TQ, gw), lambda g, i: (i, g)),
                  pl.BlockSpec((1, SL_TQ, nbp), lambda g, i: (g, i, 0)),
                  _resident((t, HEAD_DIM), lambda g, i: (0, k_col0 + g)),
                  _resident((t, HEAD_DIM), lambda g, i: (0, v_col0 + g))],
        out_specs=pl.BlockSpec((SL_TQ, gw), lambda g, i: (i, g)),
        out_shape=jax.ShapeDtypeStruct((t, B_QW), BF16),
        scratch_shapes=[pltpu.VMEM((t, HEAD_DIM + SL_NBP), BF16),
                        pltpu.VMEM((t, 2 * HEAD_DIM), BF16),
                        pltpu.VMEM((n_phase, rows, HEAD_DIM + SL_NBP), BF16),
                        pltpu.VMEM((2, rows, SL_TK), F32),
                        pltpu.VMEM((rows, LANES), F32),
                        pltpu.VMEM((rows, 2 * HEAD_DIM), F32)],
        compiler_params=_cparams(("arbitrary", "arbitrary")),
    )(z_b, sel, z_b, z_b)


WN_TQ = WIN


def _nsa_win_kernel(q_ref, ka_ref, kb_ref, va_ref, vb_ref, oc_ref, os_ref, gate_ref, o_ref, qs_ref):
    i = pl.program_id(1)
    tq = WN_TQ
    for hh in range(B_HPG):
        q = q_ref[:, hh * HEAD_DIM:(hh + 1) * HEAD_DIM].astype(F32) * Q_SCALE
        qs_ref[hh * tq:(hh + 1) * tq, :] = q.astype(BF16)
    qs = qs_ref[...]
    t_loc = lax.broadcasted_iota(jnp.int32, (B_HPG * tq, 1), 0) & (tq - 1)
    c_loc = lax.broadcasted_iota(jnp.int32, (1, tq), 1)
    t_prev = t_loc + jnp.where(i > 0, 0, tq)
    s_a = jnp.where(c_loc > t_prev, _dot_nt(qs, ka_ref[...]), NEG_INF)
    s_b = jnp.where(c_loc <= t_loc, _dot_nt(qs, kb_ref[...]), NEG_INF)
    m = jnp.maximum(jnp.max(s_a, axis=1, keepdims=True), jnp.max(s_b, axis=1, keepdims=True))
    p_a = jnp.exp2(s_a - m)
    p_b = jnp.exp2(s_b - m)
    l = jnp.sum(p_a, axis=1, keepdims=True) + jnp.sum(p_b, axis=1, keepdims=True)
    o_w = (_dot(p_a.astype(BF16), va_ref[...]) + _dot(p_b.astype(BF16), vb_ref[...])) / l
    gates = _sigmoid(gate_ref[...])
    for hh in range(B_HPG):
        hs = slice(hh * HEAD_DIM, (hh + 1) * HEAD_DIM)
        g_c, g_s, g_w = (gates[:, 3 * hh + c:3 * hh + c + 1] for c in range(3))
        o = (g_c * oc_ref[:, hs].astype(F32) + g_s * os_ref[:, hs].astype(F32)
             + g_w * o_w[hh * tq:(hh + 1) * tq])
        o_ref[:, hs] = o.astype(o_ref.dtype)


def _nsa_win_combine(z_b, o_c, o_s, z_gate):
    t = z_b.shape[0]
    gw = B_HPG * HEAD_DIM
    k_col0 = (B_QW + 4 * B_KVW) // HEAD_DIM
    v_col0 = (B_QW + 5 * B_KVW) // HEAD_DIM
    prev_tile = lambda col0: (lambda g, i: (jnp.maximum(i - 1, 0), col0 + g))
    this_tile = lambda col0: (lambda g, i: (i, col0 + g))
    kv_block = (WN_TQ, HEAD_DIM)
    return pl.pallas_call(
        _nsa_win_kernel,
        grid=(B_KV_GROUPS, t // WN_TQ),
        in_specs=[pl.BlockSpec((WN_TQ, gw), lambda g, i: (i, g)),
                  pl.BlockSpec(kv_block, prev_tile(k_col0)),
                  pl.BlockSpec(kv_block, this_tile(k_col0)),
                  pl.BlockSpec(kv_block, prev_tile(v_col0)),
                  pl.BlockSpec(kv_block, this_tile(v_col0)),
                  pl.BlockSpec((WN_TQ, gw), lambda g, i: (i, g)),
                  pl.BlockSpec((WN_TQ, gw), lambda g, i: (i, g)),
                  pl.BlockSpec((WN_TQ, HEAD_DIM), lambda g, i: (i, g))],
        out_specs=pl.BlockSpec((WN_TQ, gw), lambda g, i: (i, g)),
        out_shape=jax.ShapeDtypeStruct((t, B_QW), BF16),
        scratch_shapes=[pltpu.VMEM((B_HPG * WN_TQ, HEAD_DIM), BF16)],
        compiler_params=_cparams(("parallel", "parallel")),
    )(z_b, z_b, z_b, z_b, z_b, o_c, o_s, z_gate)


def _nsa(z_b, z_gate, pos_k, w1k, w2k, pos_v, w1v, w2v):
    t = z_b.shape[0]
    ng = B_KV_GROUPS

    def segs(col0):
        z = z_b[:, col0:col0 + B_KVW].reshape(t, ng, HEAD_DIM).transpose(1, 0, 2)
        return z.reshape(ng, t // CMP_STRIDE, SEG_W)

    seg = jnp.stack([segs(B_QW), segs(B_QW + B_KVW)])
    pos = jnp.stack([pos_k, pos_v]).reshape(2, 1, CMP_LEN * HEAD_DIM).astype(BF16)
    w1 = jnp.stack([w1k, w1v]).astype(BF16)
    w2 = jnp.stack([w2k, w2v]).astype(BF16)
    cmp_kv = _nsa_compress(seg, pos, w1, w2)
    o_c, sel = _nsa_cmp_select(z_b, cmp_kv)
    o_s = _nsa_slc(z_b, sel)
    return _nsa_win_combine(z_b, o_c, o_s, z_gate)


KM_ROWS = 8


def _kmean_kernel(k_ref, o_ref):
    x = k_ref[...].astype(F32)
    o_ref[...] = jnp.mean(x.reshape(KM_ROWS, MOBA_BLOCK, x.shape[1]), axis=1)


def _moba_kmean(z_cd):
    t = z_cd.shape[0]
    nbm = t // MOBA_BLOCK
    return pl.pallas_call(
        _kmean_kernel,
        grid=(nbm // KM_ROWS,),
        in_specs=[pl.BlockSpec((KM_ROWS * MOBA_BLOCK, C_W), lambda i: (i, 1))],
        out_specs=pl.BlockSpec((KM_ROWS, C_W), lambda i: (i, 0)),
        out_shape=jax.ShapeDtypeStruct((nbm, C_W), F32),
        compiler_params=_cparams(("parallel",)),
    )(z_cd)


MB_TQ = 1024
MB_TK = MB_TQ // 2
MB_NBP = 128


def _moba_kernel(q_ref, km_ref, k_ref, v_ref, o_ref, ka_ref, va_ref, qa_ref, s_ref, m_ref, acc_ref):
    i = pl.program_id(1)
    tq = MB_TQ

    @pl.when(i == 0)
    def _():
        _fill_augmented(k_ref, v_ref, ka_ref, va_ref, MOBA_BLOCK)

    q = q_ref[...]
    km = km_ref[...]
    km_hi = km.astype(BF16)
    km_lo = (km - km_hi.astype(F32)).astype(BF16)
    gate = _dot_nt(q, km_hi) + _dot_nt(q, km_lo)
    jb = lax.broadcasted_iota(jnp.int32, (1, MB_NBP), 1)
    t_pos = i * tq + lax.broadcasted_iota(jnp.int32, (tq, 1), 0)
    cur = t_pos // MOBA_BLOCK
    earlier = jb < cur
    sel = _topk_mask(jnp.where(earlier, gate, NEG_INF), jb.astype(F32), MOBA_TOPK)
    sel = jnp.where(jb == cur, 1.0, jnp.where(earlier, sel, 0.0))
    qa_ref[:, 0:HEAD_DIM] = (q.astype(F32) * Q_SCALE).astype(BF16)
    qa_ref[:, HEAD_DIM:] = ((sel - 1.0) * (-NEG_INF)).astype(BF16)
    _flash_init(m_ref, None, acc_ref)

    def qk(slot, j):
        off = pl.multiple_of(j * MB_TK, MB_TK)
        s_ref[slot] = _dot_nt(qa_ref[...], ka_ref[pl.ds(off, MB_TK), :])

    def process(slot, j, causal):
        off = pl.multiple_of(j * MB_TK, MB_TK)
        if causal:
            kpos = off + lax.broadcasted_iota(jnp.int32, (1, MB_TK), 1)
            _mask_scores(s_ref.at[slot], kpos <= t_pos)
        _flash_update(s_ref.at[slot], va_ref[pl.ds(off, MB_TK), :], m_ref, None, acc_ref)

    _flash_sweep((i * tq + tq + MB_TK - 1) // MB_TK, qk, process)
    acc = acc_ref[...]
    o_ref[...] = (acc[:, 0:HEAD_DIM] / acc[:, HEAD_DIM:HEAD_DIM + 1]).astype(o_ref.dtype)


def _moba(z_cd):
    t = z_cd.shape[0]
    nbm = t // MOBA_BLOCK
    nh = C_HEADS
    k_mean = _moba_kmean(z_cd)
    k_mean = jnp.pad(k_mean, ((0, MB_NBP - nbm), (0, 0)))
    return pl.pallas_call(
        _moba_kernel,
        grid=(nh, t // MB_TQ),
        in_specs=[pl.BlockSpec((MB_TQ, HEAD_DIM), lambda h, i: (i, h)),
                  pl.BlockSpec((MB_NBP, HEAD_DIM), lambda h, i: (0, h)),
                  _resident((t, HEAD_DIM), lambda h, i: (0, nh + h)),
                  _resident((t, HEAD_DIM), lambda h, i: (0, 2 * nh + h))],
        out_specs=pl.BlockSpec((MB_TQ, HEAD_DIM), lambda h, i: (i, h)),
        out_shape=jax.ShapeDtypeStruct((t, C_W), BF16),
        scratch_shapes=[pltpu.VMEM((t, HEAD_DIM + MB_NBP), BF16),
                        pltpu.VMEM((t, 2 * HEAD_DIM), BF16),
                        pltpu.VMEM((MB_TQ, HEAD_DIM + MB_NBP), BF16),
                        pltpu.VMEM((2, MB_TQ, MB_TK), F32),
                        pltpu.VMEM((MB_TQ, LANES), F32),
                        pltpu.VMEM((MB_TQ, 2 * HEAD_DIM), F32)],
        compiler_params=_cparams(("arbitrary", "arbitrary")),
    )(z_cd, k_mean, z_cd, z_cd)


DF_TQ = 1024
DF_TK = DF_TQ // 2


def _diff_kernel(q_ref, k_ref, v_ref, lq1_ref, lk1_ref, lq2_ref, lk2_ref, nw_ref, o_ref,
                 qs_ref, s_ref, m_ref, l_ref, acc_ref, *, lam_init):
    i = pl.program_id(1)
    tq = DF_TQ
    for mp in range(2):
        q = q_ref[:, mp * D_DK:(mp + 1) * D_DK].astype(F32) * (D_DK ** -0.5 * LOG2E)
        qs_ref[mp] = q.astype(BF16)
        _flash_init(m_ref.at[mp], l_ref.at[mp], acc_ref.at[mp])
    t_pos = i * tq + lax.broadcasted_iota(jnp.int32, (tq, 1), 0)

    def qk(slot, j):
        off = pl.multiple_of(j * DF_TK, DF_TK)
        for mp in range(2):
            s_ref[slot, mp] = _dot_nt(qs_ref[mp], k_ref[pl.ds(off, DF_TK), mp * D_DK:(mp + 1) * D_DK])

    def process(slot, j, causal):
        off = pl.multiple_of(j * DF_TK, DF_TK)
        v = v_ref[pl.ds(off, DF_TK), :]
        for mp in range(2):
            if causal:
                kpos = off + lax.broadcasted_iota(jnp.int32, (1, DF_TK), 1)
                _mask_scores(s_ref.at[slot, mp], kpos <= t_pos)
            _flash_update(s_ref.at[slot, mp], v, m_ref.at[mp], l_ref.at[mp], acc_ref.at[mp])

    _flash_sweep((i * tq + tq + DF_TK - 1) // DF_TK, qk, process)
    lam = (jnp.exp(jnp.sum(lq1_ref[...] * lk1_ref[...], axis=1, keepdims=True))
           - jnp.exp(jnp.sum(lq2_ref[...] * lk2_ref[...], axis=1, keepdims=True)) + lam_init)
    o = (acc_ref[0] / _lane_tile(l_ref[0], D_DV)
         - lam * (acc_ref[1] / _lane_tile(l_ref[1], D_DV)))
    o = o * lax.rsqrt(jnp.mean(o * o, axis=-1, keepdims=True) + RMS_EPS) * nw_ref[...]
    o_ref[...] = (o * (1.0 - lam_init)).astype(o_ref.dtype)


def _diff_attn(z_cd, lq1, lk1, lq2, lk2, subln_w, layer_idx):
    t = z_cd.shape[0]
    lam_init = 0.8 - 0.6 * math.exp(-0.3 * layer_idx)
    q_col0 = 3 * C_W // D_DV
    k_col0 = (3 * C_W + D_QW) // D_DV
    v_col0 = (3 * C_W + 2 * D_QW) // D_DV
    vec = pl.BlockSpec((1, D_DK), lambda h, i: (0, 0))
    return pl.pallas_call(
        functools.partial(_diff_kernel, lam_init=lam_init),
        grid=(D_HEADS, t // DF_TQ),
        in_specs=[pl.BlockSpec((DF_TQ, 2 * D_DK), lambda h, i: (i, q_col0 + h)),
                  _resident((t, 2 * D_DK), lambda h, i: (0, k_col0 + h)),
                  _resident((t, D_DV), lambda h, i: (0, v_col0 + h)),
                  vec, vec, vec, vec,
                  pl.BlockSpec((1, D_DV), lambda h, i: (0, 0))],
        out_specs=pl.BlockSpec((DF_TQ, D_DV), lambda h, i: (i, h)),
        out_shape=jax.ShapeDtypeStruct((t, D_VW), BF16),
        scratch_shapes=[pltpu.VMEM((2, DF_TQ, D_DK), BF16),
                        pltpu.VMEM((2, 2, DF_TQ, DF_TK), F32),
                        pltpu.VMEM((2, DF_TQ, LANES), F32),
                        pltpu.VMEM((2, DF_TQ, LANES), F32),
                        pltpu.VMEM((2, DF_TQ, D_DV), F32)],
        compiler_params=_cparams(("parallel", "arbitrary")),
    )(z_cd, z_cd, z_cd, lq1.reshape(1, D_DK), lk1.reshape(1, D_DK), lq2.reshape(1, D_DK),
      lk2.reshape(1, D_DK), subln_w.reshape(1, D_DV))


def _gate_weight(w_gate):
    d = w_gate.shape[0]
    per = 3 * B_HPG
    wg = w_gate.reshape(d, B_KV_GROUPS, per)
    wg = jnp.pad(wg, ((0, 0), (0, 0), (0, HEAD_DIM - per)))
    return wg.reshape(d, B_KV_GROUPS * HEAD_DIM)


def kernel(x, ln_w, ln_b, ffn_w_in, ffn_w_out, ab_w_in, ab_w_out, hgrn_lower_bounds, hgrn_norm_w,
           nsa_cmp_pos_k, nsa_cmp_k_w1, nsa_cmp_k_w2, nsa_cmp_pos_v, nsa_cmp_v_w1, nsa_cmp_v_w2,
           cd_w_in, cd_w_out, diff_lambda_q1, diff_lambda_k1, diff_lambda_q2, diff_lambda_k2, diff_subln_w):
    bsz, t, d = x.shape
    lb_all = jnp.cumsum(jax.nn.softmax(hgrn_lower_bounds.astype(F32), axis=0), axis=0)
    outs = []
    for bi in range(bsz):
        h = x[bi]
        h_in = h
        for layer in range(DEPTH):
            if layer % 2 == 0:
                e = layer // 2
                n_a = 4 * A_W
                n_b = B_QW + 6 * B_KVW
                z_a = _matmul(h_in, ab_w_in, e, 0, n_a, F32)
                z_b = _matmul(h_in, ab_w_in, e, n_a, n_b, BF16)
                w_gate = _gate_weight(ab_w_in[e, :, n_a + n_b:])
                z_g = _matmul(h_in, w_gate[None], 0, 0, w_gate.shape[1], F32)
                o_a = _hgrn2(z_a, lb_all[layer], hgrn_norm_w[e])
                o_b = _nsa(z_b, z_g, nsa_cmp_pos_k[e], nsa_cmp_k_w1[e], nsa_cmp_k_w2[e],
                           nsa_cmp_pos_v[e], nsa_cmp_v_w1[e], nsa_cmp_v_w2[e])
                mix_in = (o_a, o_b)
                w_out = ab_w_out[e]
            else:
                oi = layer // 2
                z_cd = _matmul(h_in, cd_w_in, oi, 0, cd_w_in.shape[2], BF16)
                o_c = _moba(z_cd)
                o_d = _diff_attn(z_cd, diff_lambda_q1[oi], diff_lambda_k1[oi], diff_lambda_q2[oi],
                                 diff_lambda_k2[oi], diff_subln_w[oi], layer)
                mix_in = (o_c, o_d)
                w_out = cd_w_out[oi]
            h, h_in = _mm_res_ln(mix_in, w_out.astype(BF16), h, ln_w[layer, 0], ln_b[layer, 0])
            act = _ffn_in(h_in, ffn_w_in, layer)
            h, h_in = _mm_res_ln((act,), ffn_w_out[layer].astype(BF16), h, ln_w[layer, 1], ln_b[layer, 1])
        outs.append(h)
    return jnp.stack(outs)
```

```python
import functools
import math

import numpy as np
import jax
import jax.numpy as jnp
from jax import lax
from jax.experimental import pallas as pl
from jax.experimental.pallas import tpu as pltpu

F32 = jnp.float32
BF16 = jnp.bfloat16

D_MODEL = 2048
DEPTH = 2
HEAD_DIM = 128
A_HEADS = 8
A_CHUNK = 64
B_HEADS = 8
B_KV_GROUPS = 2
B_HPG = B_HEADS // B_KV_GROUPS
CMP_LEN = 32
CMP_STRIDE = 16
SLC_LEN = 64
N_SLC = 16
N_FORCED = 3
WIN = 512
C_HEADS = 8
MOBA_BLOCK = 256
MOBA_TOPK = 3
D_HEADS = 4
D_DK = 128
D_DV = 2 * D_DK
D_FF = ((8 * D_MODEL + 3 * 256 - 1) // (3 * 256)) * 256

DEEPNORM_ALPHA = (2 * DEPTH) ** 0.25
NEG_INF = -1e30
FORCE_SCORE = 1e9
LN_EPS = 1e-5
RMS_EPS = 1e-6
ATT_SCALE = HEAD_DIM ** -0.5

A_W = A_HEADS * HEAD_DIM
B_QW = B_HEADS * HEAD_DIM
B_KVW = B_KV_GROUPS * HEAD_DIM
C_W = C_HEADS * HEAD_DIM
D_QW = D_HEADS * 2 * D_DK
D_VW = D_HEADS * D_DV

VMEM_LIMIT = 56 * 1024 * 1024
MM_SMALL_WEIGHT_BYTES = 8 * 1024 * 1024


def _cparams(sem):
    return pltpu.CompilerParams(dimension_semantics=sem, vmem_limit_bytes=VMEM_LIMIT)


def _dot(a, b):
    return jnp.dot(a, b, preferred_element_type=F32)


def _dot_nt(a, b):
    return lax.dot_general(a, b, (((1,), (1,)), ((), ())), preferred_element_type=F32)


def _dot_tn(a, b):
    return lax.dot_general(a, b, (((0,), (0,)), ((), ())), preferred_element_type=F32)


def _sigmoid(x):
    return 1.0 / (1.0 + jnp.exp(-x))


def _silu(x):
    return x * _sigmoid(x)


MM_TM = 1024
MM_TN = 512


def _mm_kernel(x_ref, w_ref, o_ref, wb_ref):
    @pl.when(pl.program_id(1) == 0)
    def _():
        wb_ref[...] = w_ref[0].astype(BF16)

    o_ref[...] = _dot(x_ref[...].astype(BF16), wb_ref[...]).astype(o_ref.dtype)


def _matmul(x, w, widx, col0, n, out_dtype):
    m, k = x.shape
    tn = min(MM_TN, n)
    j0 = col0 // tn
    return pl.pallas_call(
        _mm_kernel,
        grid=(n // tn, m // MM_TM),
        in_specs=[pl.BlockSpec((MM_TM, k), lambda j, i: (i, 0)),
                  pl.BlockSpec((1, k, tn), lambda j, i: (widx, 0, j0 + j))],
        out_specs=pl.BlockSpec((MM_TM, tn), lambda j, i: (i, j)),
        out_shape=jax.ShapeDtypeStruct((m, n), out_dtype),
        scratch_shapes=[pltpu.VMEM((k, tn), BF16)],
        compiler_params=_cparams(("arbitrary", "arbitrary")),
    )(x, w)


def _ffn_in_kernel(x_ref, wg_ref, wu_ref, o_ref, wgb_ref, wub_ref):
    @pl.when(pl.program_id(1) == 0)
    def _():
        wgb_ref[...] = wg_ref[0].astype(BF16)
        wub_ref[...] = wu_ref[0].astype(BF16)

    x = x_ref[...]
    g = _dot(x, wgb_ref[...])
    u = _dot(x, wub_ref[...])
    o_ref[...] = (_silu(g) * u).astype(o_ref.dtype)


def _ffn_in(x, w_in, layer):
    m, k = x.shape
    dff = w_in.shape[2] // 2
    nj = dff // MM_TN
    return pl.pallas_call(
        _ffn_in_kernel,
        grid=(nj, m // MM_TM),
        in_specs=[pl.BlockSpec((MM_TM, k), lambda j, i: (i, 0)),
                  pl.BlockSpec((1, k, MM_TN), lambda j, i: (layer, 0, j)),
                  pl.BlockSpec((1, k, MM_TN), lambda j, i: (layer, 0, j + nj))],
        out_specs=pl.BlockSpec((MM_TM, MM_TN), lambda j, i: (i, j)),
        out_shape=jax.ShapeDtypeStruct((m, dff), BF16),
        scratch_shapes=[pltpu.VMEM((k, MM_TN), BF16), pltpu.VMEM((k, MM_TN), BF16)],
        compiler_params=_cparams(("arbitrary", "arbitrary")),
    )(x, w_in, w_in)


def _layer_norm(y, w, b):
    mu = jnp.mean(y, axis=-1, keepdims=True)
    d = y - mu
    var = jnp.mean(d * d, axis=-1, keepdims=True)
    return d * lax.rsqrt(var + LN_EPS) * w + b


def _mm_res_ln_kernel(*refs, widths):
    part_refs = refs[:len(widths)]
    w_ref, h_ref, lnw_ref, lnb_ref, o_ref, obf_ref = refs[len(widths):]
    y = DEEPNORM_ALPHA * h_ref[...]
    k0 = 0
    for a_ref, width in zip(part_refs, widths):
        y = y + _dot(a_ref[...], w_ref[k0:k0 + width, :])
        k0 += width
    out = _layer_norm(y, lnw_ref[...], lnb_ref[...])
    o_ref[...] = out
    obf_ref[...] = out.astype(BF16)


def _mm_res_ln(parts, w, h, ln_w, ln_b):
    m = parts[0].shape[0]
    widths = tuple(a.shape[1] for a in parts)
    kdim, n = w.shape
    tm = 512 if kdim * n * 2 <= MM_SMALL_WEIGHT_BYTES else 256
    return pl.pallas_call(
        functools.partial(_mm_res_ln_kernel, widths=widths),
        grid=(m // tm,),
        in_specs=[pl.BlockSpec((tm, width), lambda i: (i, 0)) for width in widths]
        + [_resident((kdim, n), lambda i: (0, 0)),
           pl.BlockSpec((tm, n), lambda i: (i, 0)),
           pl.BlockSpec((1, n), lambda i: (0, 0)),
           pl.BlockSpec((1, n), lambda i: (0, 0))],
        out_specs=[pl.BlockSpec((tm, n), lambda i: (i, 0)),
                   pl.BlockSpec((tm, n), lambda i: (i, 0))],
        out_shape=[jax.ShapeDtypeStruct((m, n), F32), jax.ShapeDtypeStruct((m, n), BF16)],
        compiler_params=_cparams(("parallel",)),
    )(*parts, w, h, ln_w.reshape(1, n), ln_b.reshape(1, n))


LOG2E = math.log2(math.e)
LANES = 128
Q_SCALE = ATT_SCALE * LOG2E


def _flash_init(m_ref, l_ref, acc_ref):
    m_ref[...] = jnp.full(m_ref.shape, -jnp.inf, F32)
    if l_ref is not None:
        l_ref[...] = jnp.zeros(l_ref.shape, F32)
    acc_ref[...] = jnp.zeros(acc_ref.shape, F32)


def _flash_update(s_ref, v, m_ref, l_ref, acc_ref):
    m_prev = m_ref[...]
    m_new = jnp.maximum(m_prev, jnp.max(s_ref[...], axis=1, keepdims=True))
    m_ref[...] = m_new
    alpha = jnp.exp2(m_prev - m_new)
    p = jnp.exp2(s_ref[...] - _lane_tile(m_new, s_ref.shape[1]))
    if l_ref is not None:
        l_ref[...] = alpha * l_ref[...] + jnp.sum(p, axis=1, keepdims=True)
    acc_ref[...] = _lane_tile(alpha, acc_ref.shape[1]) * acc_ref[...] + _dot(p.astype(BF16), v)


def _mask_scores(s_ref, visible):
    s_ref[...] = jnp.where(visible, s_ref[...], NEG_INF)


def _lane_tile(x, width):
    reps = width // x.shape[1]
    return x if reps == 1 else jnp.concatenate([x] * reps, axis=1)


def _flash_sweep(n_tiles, qk, process):
    n_pairs = (n_tiles + 1) // 2
    qk(0, 0)

    def pair(jj, carry):
        j = 2 * jj
        qk(1, j + 1)
        process(0, j, False)
        qk(0, j + 2)
        process(1, j + 1, False)
        return carry

    lax.fori_loop(0, n_pairs - 1, pair, 0)
    j_last = 2 * (n_pairs - 1)
    qk(1, j_last + 1)
    process(0, j_last, True)
    process(1, j_last + 1, True)


AUG_FILL_ROWS = 512


def _fill_augmented(k_ref, v_ref, ka_ref, va_ref, block_len):
    lane = lax.broadcasted_iota(jnp.int32, (AUG_FILL_ROWS, LANES), 1)
    ones_col = jnp.where(lane == 0, 1.0, 0.0).astype(BF16)

    def fill(c, carry):
        r0 = pl.multiple_of(c * AUG_FILL_ROWS, AUG_FILL_ROWS)
        rows = pl.ds(r0, AUG_FILL_ROWS)
        blk = (r0 + lax.broadcasted_iota(jnp.int32, (AUG_FILL_ROWS, LANES), 0)) // block_len
        ka_ref[rows, 0:HEAD_DIM] = k_ref[rows, :]
        ka_ref[rows, HEAD_DIM:] = jnp.where((blk & (LANES - 1)) == lane, 1.0, 0.0).astype(BF16)
        va_ref[rows, 0:HEAD_DIM] = v_ref[rows, :]
        va_ref[rows, HEAD_DIM:] = ones_col
        return carry

    lax.fori_loop(0, k_ref.shape[0] // AUG_FILL_ROWS, fill, 0)


def _resident(block_shape, index_map):
    return pl.BlockSpec(block_shape, index_map, pipeline_mode=pl.Buffered(1))


HG_TILE = 512


def _hgrn_kernel(q_ref, f_ref, i_ref, g_ref, lb_ref, nw_ref, o_ref, st_ref):
    @pl.when(pl.program_id(1) == 0)
    def _():
        st_ref[...] = jnp.zeros_like(st_ref)

    lb = lb_ref[...]
    f = lb + (1.0 - lb) * _sigmoid(f_ref[...])
    logf = jnp.log(f)
    kk = 1.0 - f
    qf = _silu(q_ref[...])
    r64 = lax.broadcasted_iota(jnp.int32, logf.shape, 0) & (A_CHUNK - 1)
    b = logf
    step = 1
    while step < A_CHUNK:
        b = b + jnp.where(r64 >= step, pltpu.roll(b, step, 0), 0.0)
        step *= 2
    causal = (lax.broadcasted_iota(jnp.int32, (A_CHUNK, A_CHUNK), 0)
              >= lax.broadcasted_iota(jnp.int32, (A_CHUNK, A_CHUNK), 1))
    nw = nw_ref[...]
    for c in range(HG_TILE // A_CHUNK):
        sl = slice(c * A_CHUNK, (c + 1) * A_CHUNK)
        bc = b[sl]
        b_last = bc[A_CHUNK - 1:A_CHUNK, :]
        q_t = (qf[sl] * jnp.exp(bc)).astype(BF16)
        k_t = (kk[sl] * jnp.exp(-bc)).astype(BF16)
        vc = i_ref[sl, :].astype(BF16)
        att = jnp.where(causal, _dot_nt(q_t, k_t), 0.0)
        st = st_ref[...]
        o = _dot(att.astype(BF16), vc) + _dot_nt(q_t, st.astype(BF16))
        kdec = (kk[sl] * jnp.exp(b_last - bc)).astype(BF16)
        st_ref[...] = st * jnp.exp(b_last) + _dot_tn(vc, kdec)
        o = o * lax.rsqrt(jnp.mean(o * o, axis=-1, keepdims=True) + RMS_EPS) * nw
        o_ref[sl, :] = (o * _silu(g_ref[sl, :])).astype(o_ref.dtype)


def _hgrn2(z_a, lb, norm_w):
    t = z_a.shape[0]
    nh = A_HEADS
    col = lambda base: (lambda h, i: (i, base + h))
    return pl.pallas_call(
        _hgrn_kernel,
        grid=(nh, t // HG_TILE),
        in_specs=[pl.BlockSpec((HG_TILE, HEAD_DIM), col(0)),
                  pl.BlockSpec((HG_TILE, HEAD_DIM), col(nh)),
                  pl.BlockSpec((HG_TILE, HEAD_DIM), col(2 * nh)),
                  pl.BlockSpec((HG_TILE, HEAD_DIM), col(3 * nh)),
                  pl.BlockSpec((1, HEAD_DIM), lambda h, i: (0, h)),
                  pl.BlockSpec((1, HEAD_DIM), lambda h, i: (0, 0))],
        out_specs=pl.BlockSpec((HG_TILE, HEAD_DIM), lambda h, i: (i, h)),
        out_shape=jax.ShapeDtypeStruct((t, A_W), BF16),
        scratch_shapes=[pltpu.VMEM((HEAD_DIM, HEAD_DIM), F32)],
        compiler_params=_cparams(("parallel", "arbitrary")),
    )(z_a, z_a, z_a, z_a, lb.reshape(1, A_W), norm_w.reshape(1, HEAD_DIM))


SEG_W = CMP_STRIDE * HEAD_DIM


def _nsa_compress_kernel(seg_ref, pos_ref, w1_ref, w2_ref, o_ref):
    seg = seg_ref[0, 0]
    nseg = seg.shape[0]
    a = _dot(seg, w1_ref[0, :SEG_W, :])
    b = _dot(seg, w1_ref[0, SEG_W:, :])
    b_next = pltpu.roll(b, nseg - 1, 0)
    pos = jnp.broadcast_to(pos_ref[0], (8, CMP_LEN * HEAD_DIM))
    c = _dot(pos, w1_ref[0])[0:1, :]
    hid = _silu(a + b_next + c)
    o_ref[0, 0] = _dot(hid.astype(BF16), w2_ref[0]).astype(o_ref.dtype)


def _nsa_compress(seg, pos, w1, w2):
    _, ng, nseg, _ = seg.shape
    return pl.pallas_call(
        _nsa_compress_kernel,
        grid=(2, ng),
        in_specs=[pl.BlockSpec((1, 1, nseg, SEG_W), lambda a, g: (a, g, 0, 0)),
                  pl.BlockSpec((1, 1, CMP_LEN * HEAD_DIM), lambda a, g: (a, 0, 0)),
                  pl.BlockSpec((1, CMP_LEN * HEAD_DIM, HEAD_DIM), lambda a, g: (a, 0, 0)),
                  pl.BlockSpec((1, HEAD_DIM, HEAD_DIM), lambda a, g: (a, 0, 0))],
        out_specs=pl.BlockSpec((1, 1, nseg, HEAD_DIM), lambda a, g: (a, g, 0, 0)),
        out_shape=jax.ShapeDtypeStruct((2, ng, nseg, HEAD_DIM), BF16),
        compiler_params=_cparams(("parallel", "parallel")),
    )(seg, pos, w1, w2)


CS_TQ = 256
CS_WIDTH_STEP = 256
SL_NBP = 128


def _split3(x):
    hi = x.astype(BF16)
    r = x - hi.astype(F32)
    mid = r.astype(BF16)
    lo = (r - mid.astype(F32)).astype(BF16)
    return hi, mid, lo


def _topk_mask(score, lane, k):
    sel = jnp.zeros(score.shape, F32)
    width = float(score.shape[1])
    for _ in range(k):
        mx = jnp.max(score, axis=1, keepdims=True)
        idx = jnp.min(jnp.where(score == mx, lane, width), axis=1, keepdims=True)
        hit = lane == idx
        sel = jnp.where(hit, 1.0, sel)
        score = jnp.where(hit, -jnp.inf, score)
    return sel


def _nsa_cmp_kernel(q_ref, kc_ref, vc_ref, map_ref, o_ref, sel_ref, *, n_sel):
    i = pl.program_id(1)
    tq = q_ref.shape[0]
    ncmp = kc_ref.shape[2]
    nbp = map_ref.shape[1]
    t_pos = i * tq + lax.broadcasted_iota(jnp.int32, (tq, 1), 0)
    row_has_keys = t_pos >= CMP_LEN - 1
    jb = lax.broadcasted_iota(jnp.int32, (1, nbp), 1)
    cur = t_pos // SLC_LEN
    valid = jb <= cur
    forced = (jb == 0) | (jb == cur) | (jb == cur - 1)

    def body(width):
        cmp_end = lax.broadcasted_iota(jnp.int32, (1, width), 1) * CMP_STRIDE + (CMP_LEN - 1)
        mask = cmp_end <= t_pos
        kc = kc_ref[0, 0, 0:width, :]
        vc = vc_ref[0, 0, 0:width, :]
        psum = jnp.zeros((tq, width), F32)
        for hh in range(B_HPG):
            hs = slice(hh * HEAD_DIM, (hh + 1) * HEAD_DIM)
            s = jnp.where(mask, _dot_nt(q_ref[:, hs], kc) * Q_SCALE, NEG_INF)
            e = jnp.exp2(s - jnp.max(s, axis=1, keepdims=True))
            p = e * jnp.where(row_has_keys, 1.0 / jnp.sum(e, axis=1, keepdims=True), 0.0)
            o_ref[:, hs] = _dot(p.astype(BF16), vc).astype(o_ref.dtype)
            psum = psum + p
        hi, mid, lo = _split3(psum)
        cmap = map_ref[0:width, :]
        p_slc = _dot(hi, cmap) + _dot(mid, cmap) + _dot(lo, cmap)
        score = jnp.where(valid & jnp.logical_not(forced), p_slc, NEG_INF)
        sel = _topk_mask(score, jb.astype(F32), n_sel - N_FORCED)
        sel_ref[0] = jnp.where(valid, jnp.where(forced, 1.0, sel), 0.0).astype(sel_ref.dtype)

    widths = list(range(CS_WIDTH_STEP, ncmp + 1, CS_WIDTH_STEP)) if ncmp % CS_WIDTH_STEP == 0 else [ncmp]
    if len(widths) == 1:
        body(widths[0])
    else:
        n_vis = ((i + 1) * tq - CMP_LEN) // CMP_STRIDE + 1
        case = jnp.clip((n_vis + CS_WIDTH_STEP - 1) // CS_WIDTH_STEP - 1, 0, len(widths) - 1)
        for idx, width in enumerate(widths):
            pl.when(case == idx)(functools.partial(body, width))


def _cmp_to_slc_matrix(ncmp_pad, n_cmp, nb, nbp):
    ratio = SLC_LEN // CMP_STRIDE
    n_over = CMP_LEN // CMP_STRIDE
    mat = np.zeros((ncmp_pad, nbp), np.float32)
    for j in range(nb):
        for m in range(ratio):
            for n in range(n_over):
                c = ratio * j + m - n
                if 0 <= c < n_cmp:
                    mat[c, j] += 1.0
    return mat


def _nsa_cmp_select(z_b, cmp_kv):
    t = z_b.shape[0]
    ng = B_KV_GROUPS
    nseg = cmp_kv.shape[2]
    nb = t // SLC_LEN
    n_cmp = (t - CMP_LEN) // CMP_STRIDE + 1
    nbp = -(-nb // SL_NBP) * SL_NBP
    cmap = jnp.asarray(_cmp_to_slc_matrix(nseg, n_cmp, nb, nbp), BF16)
    gw = B_HPG * HEAD_DIM
    return pl.pallas_call(
        functools.partial(_nsa_cmp_kernel, n_sel=min(N_SLC, nb)),
        grid=(ng, t // CS_TQ),
        in_specs=[pl.BlockSpec((CS_TQ, gw), lambda g, i: (i, g)),
                  pl.BlockSpec((1, 1, nseg, HEAD_DIM), lambda g, i: (0, g, 0, 0)),
                  pl.BlockSpec((1, 1, nseg, HEAD_DIM), lambda g, i: (1, g, 0, 0)),
                  pl.BlockSpec((nseg, nbp), lambda g, i: (0, 0))],
        out_specs=[pl.BlockSpec((CS_TQ, gw), lambda g, i: (i, g)),
                   pl.BlockSpec((1, CS_TQ, nbp), lambda g, i: (g, i, 0))],
        out_shape=[jax.ShapeDtypeStruct((t, B_QW), BF16),
                   jax.ShapeDtypeStruct((ng, t, nbp), BF16)],
        compiler_params=_cparams(("parallel", "parallel")),
    )(z_b, cmp_kv, cmp_kv, cmap)


SL_TQ = 512
SL_TK = SL_TQ
SL_PHASE_TILES = SL_NBP * SLC_LEN // SL_TK


def _nsa_slc_kernel(q_ref, sel_ref, k_ref, v_ref, o_ref, ka_ref, va_ref, qa_ref, s_ref, m_ref, acc_ref, *, n_phase):
    i = pl.program_id(1)
    tq = SL_TQ

    @pl.when(i == 0)
    def _():
        _fill_augmented(k_ref, v_ref, ka_ref, va_ref, SLC_LEN)

    bias = ((sel_ref[0].astype(F32) - 1.0) * (-NEG_INF)).astype(BF16)
    for hh in range(B_HPG):
        rows = slice(hh * tq, (hh + 1) * tq)
        q = (q_ref[:, hh * HEAD_DIM:(hh + 1) * HEAD_DIM].astype(F32) * Q_SCALE).astype(BF16)
        for ph in range(n_phase):
            qa_ref[ph, rows, 0:HEAD_DIM] = q
            qa_ref[ph, rows, HEAD_DIM:] = bias[:, ph * SL_NBP:(ph + 1) * SL_NBP]
    _flash_init(m_ref, None, acc_ref)
    t_pos = i * tq + (lax.broadcasted_iota(jnp.int32, (B_HPG * tq, 1), 0) & (tq - 1))

    def qk(slot, j):
        off = pl.multiple_of(j * SL_TK, SL_TK)
        qa = qa_ref[0] if n_phase == 1 else qa_ref[j // SL_PHASE_TILES]
        s_ref[slot] = _dot_nt(qa, ka_ref[pl.ds(off, SL_TK), :])

    def process(slot, j, causal):
        off = pl.multiple_of(j * SL_TK, SL_TK)
        if causal:
            kpos = off + lax.broadcasted_iota(jnp.int32, (1, SL_TK), 1)
            _mask_scores(s_ref.at[slot], kpos <= t_pos)
        _flash_update(s_ref.at[slot], va_ref[pl.ds(off, SL_TK), :], m_ref, None, acc_ref)

    _flash_sweep((i * tq + tq + SL_TK - 1) // SL_TK, qk, process)
    acc = acc_ref[...]
    o = acc[:, 0:HEAD_DIM] / acc[:, HEAD_DIM:HEAD_DIM + 1]
    for hh in range(B_HPG):
        o_ref[:, hh * HEAD_DIM:(hh + 1) * HEAD_DIM] = o[hh * tq:(hh + 1) * tq].astype(o_ref.dtype)


def _nsa_slc(z_b, sel):
    t = z_b.shape[0]
    nbp = sel.shape[2]
    n_phase = nbp // SL_NBP
    gw = B_HPG * HEAD_DIM
    rows = B_HPG * SL_TQ
    k_col0 = (B_QW + 2 * B_KVW) // HEAD_DIM
    v_col0 = (B_QW + 3 * B_KVW) // HEAD_DIM
    return pl.pallas_call(
        functools.partial(_nsa_slc_kernel, n_phase=n_phase),
        grid=(B_KV_GROUPS, t // SL_TQ),
        in_specs=[pl.BlockSpec((SL_TQ, gw), lambda g, i: (i, g)),
                  pl.BlockSpec((1, SL_TQ, nbp), lambda g, i: (g, i, 0)),
                  _resident((t, HEAD_DIM), lambda g, i: (0, k_col0 + g)),
                  _resident((t, HEAD_DIM), lambda g, i: (0, v_col0 + g))],
        out_specs=pl.BlockSpec((SL_TQ, gw), lambda g, i: (i, g)),
        out_shape=jax.ShapeDtypeStruct((t, B_QW), BF16),
        scratch_shapes=[pltpu.VMEM((t, HEAD_DIM + SL_NBP), BF16),
                        pltpu.VMEM((t, 2 * HEAD_DIM), BF16),
                        pltpu.VMEM((n_phase, rows, HEAD_DIM + SL_NBP), BF16),
                        pltpu.VMEM((2, rows, SL_TK), F32),
                        pltpu.VMEM((rows, LANES), F32),
                        pltpu.VMEM((rows, 2 * HEAD_DIM), F32)],
        compiler_params=_cparams(("arbitrary", "arbitrary")),
    )(z_b, sel, z_b, z_b)


WN_TQ = WIN


def _nsa_win_kernel(q_ref, ka_ref, kb_ref, va_ref, vb_ref, oc_ref, os_ref, gate_ref, o_ref, qs_ref):
    i = pl.program_id(1)
    tq = WN_TQ
    for hh in range(B_HPG):
        q = q_ref[:, hh * HEAD_DIM:(hh + 1) * HEAD_DIM].astype(F32) * Q_SCALE
        qs_ref[hh * tq:(hh + 1) * tq, :] = q.astype(BF16)
    qs = qs_ref[...]
    t_loc = lax.broadcasted_iota(jnp.int32, (B_HPG * tq, 1), 0) & (tq - 1)
    c_loc = lax.broadcasted_iota(jnp.int32, (1, tq), 1)
    t_prev = t_loc + jnp.where(i > 0, 0, tq)
    s_a = jnp.where(c_loc > t_prev, _dot_nt(qs, ka_ref[...]), NEG_INF)
    s_b = jnp.where(c_loc <= t_loc, _dot_nt(qs, kb_ref[...]), NEG_INF)
    m = jnp.maximum(jnp.max(s_a, axis=1, keepdims=True), jnp.max(s_b, axis=1, keepdims=True))
    p_a = jnp.exp2(s_a - m)
    p_b = jnp.exp2(s_b - m)
    l = jnp.sum(p_a, axis=1, keepdims=True) + jnp.sum(p_b, axis=1, keepdims=True)
    o_w = (_dot(p_a.astype(BF16), va_ref[...]) + _dot(p_b.astype(BF16), vb_ref[...])) / l
    gates = _sigmoid(gate_ref[...])
    for hh in range(B_HPG):
        hs = slice(hh * HEAD_DIM, (hh + 1) * HEAD_DIM)
        g_c, g_s, g_w = (gates[:, 3 * hh + c:3 * hh + c + 1] for c in range(3))
        o = (g_c * oc_ref[:, hs].astype(F32) + g_s * os_ref[:, hs].astype(F32)
             + g_w * o_w[hh * tq:(hh + 1) * tq])
        o_ref[:, hs] = o.astype(o_ref.dtype)


def _nsa_win_combine(z_b, o_c, o_s, z_gate):
    t = z_b.shape[0]
    gw = B_HPG * HEAD_DIM
    k_col0 = (B_QW + 4 * B_KVW) // HEAD_DIM
    v_col0 = (B_QW + 5 * B_KVW) // HEAD_DIM
    prev_tile = lambda col0: (lambda g, i: (jnp.maximum(i - 1, 0), col0 + g))
    this_tile = lambda col0: (lambda g, i: (i, col0 + g))
    kv_block = (WN_TQ, HEAD_DIM)
    return pl.pallas_call(
        _nsa_win_kernel,
        grid=(B_KV_GROUPS, t // WN_TQ),
        in_specs=[pl.BlockSpec((WN_TQ, gw), lambda g, i: (i, g)),
                  pl.BlockSpec(kv_block, prev_tile(k_col0)),
                  pl.BlockSpec(kv_block, this_tile(k_col0)),
                  pl.BlockSpec(kv_block, prev_tile(v_col0)),
                  pl.BlockSpec(kv_block, this_tile(v_col0)),
                  pl.BlockSpec((WN_TQ, gw), lambda g, i: (i, g)),
                  pl.BlockSpec((WN_TQ, gw), lambda g, i: (i, g)),
                  pl.BlockSpec((WN_TQ, HEAD_DIM), lambda g, i: (i, g))],
        out_specs=pl.BlockSpec((WN_TQ, gw), lambda g, i: (i, g)),
        out_shape=jax.ShapeDtypeStruct((t, B_QW), BF16),
        scratch_shapes=[pltpu.VMEM((B_HPG * WN_TQ, HEAD_DIM), BF16)],
        compiler_params=_cparams(("parallel", "parallel")),
    )(z_b, z_b, z_b, z_b, z_b, o_c, o_s, z_gate)


def _nsa(z_b, z_gate, pos_k, w1k, w2k, pos_v, w1v, w2v):
    t = z_b.shape[0]
    ng = B_KV_GROUPS

    def segs(col0):
        z = z_b[:, col0:col0 + B_KVW].reshape(t, ng, HEAD_DIM).transpose(1, 0, 2)
        return z.reshape(ng, t // CMP_STRIDE, SEG_W)

    seg = jnp.stack([segs(B_QW), segs(B_QW + B_KVW)])
    pos = jnp.stack([pos_k, pos_v]).reshape(2, 1, CMP_LEN * HEAD_DIM).astype(BF16)
    w1 = jnp.stack([w1k, w1v]).astype(BF16)
    w2 = jnp.stack([w2k, w2v]).astype(BF16)
    cmp_kv = _nsa_compress(seg, pos, w1, w2)
    o_c, sel = _nsa_cmp_select(z_b, cmp_kv)
    o_s = _nsa_slc(z_b, sel)
    return _nsa_win_combine(z_b, o_c, o_s, z_gate)


KM_ROWS = 8


def _kmean_kernel(k_ref, o_ref):
    x = k_ref[...].astype(F32)
    o_ref[...] = jnp.mean(x.reshape(KM_ROWS, MOBA_BLOCK, x.shape[1]), axis=1)


def _moba_kmean(z_cd):
    t = z_cd.shape[0]
    nbm = t // MOBA_BLOCK
    return pl.pallas_call(
        _kmean_kernel,
        grid=(nbm // KM_ROWS,),
        in_specs=[pl.BlockSpec((KM_ROWS * MOBA_BLOCK, C_W), lambda i: (i, 1))],
        out_specs=pl.BlockSpec((KM_ROWS, C_W), lambda i: (i, 0)),
        out_shape=jax.ShapeDtypeStruct((nbm, C_W), F32),
        compiler_params=_cparams(("parallel",)),
    )(z_cd)


MB_TQ = 1024
MB_TK = MB_TQ // 2
MB_NBP = 128


def _moba_kernel(q_ref, km_ref, k_ref, v_ref, o_ref, ka_ref, va_ref, qa_ref, s_ref, m_ref, acc_ref):
    i = pl.program_id(1)
    tq = MB_TQ

    @pl.when(i == 0)
    def _():
        _fill_augmented(k_ref, v_ref, ka_ref, va_ref, MOBA_BLOCK)

    q = q_ref[...]
    km = km_ref[...]
    km_hi = km.astype(BF16)
    km_lo = (km - km_hi.astype(F32)).astype(BF16)
    gate = _dot_nt(q, km_hi) + _dot_nt(q, km_lo)
    jb = lax.broadcasted_iota(jnp.int32, (1, MB_NBP), 1)
    t_pos = i * tq + lax.broadcasted_iota(jnp.int32, (tq, 1), 0)
    cur = t_pos // MOBA_BLOCK
    earlier = jb < cur
    sel = _topk_mask(jnp.where(earlier, gate, NEG_INF), jb.astype(F32), MOBA_TOPK)
    sel = jnp.where(jb == cur, 1.0, jnp.where(earlier, sel, 0.0))
    qa_ref[:, 0:HEAD_DIM] = (q.astype(F32) * Q_SCALE).astype(BF16)
    qa_ref[:, HEAD_DIM:] = ((sel - 1.0) * (-NEG_INF)).astype(BF16)
    _flash_init(m_ref, None, acc_ref)

    def qk(slot, j):
        off = pl.multiple_of(j * MB_TK, MB_TK)
        s_ref[slot] = _dot_nt(qa_ref[...], ka_ref[pl.ds(off, MB_TK), :])

    def process(slot, j, causal):
        off = pl.multiple_of(j * MB_TK, MB_TK)
        if causal:
            kpos = off + lax.broadcasted_iota(jnp.int32, (1, MB_TK), 1)
            _mask_scores(s_ref.at[slot], kpos <= t_pos)
        _flash_update(s_ref.at[slot], va_ref[pl.ds(off, MB_TK), :], m_ref, None, acc_ref)

    _flash_sweep((i * tq + tq + MB_TK - 1) // MB_TK, qk, process)
    acc = acc_ref[...]
    o_ref[...] = (acc[:, 0:HEAD_DIM] / acc[:, HEAD_DIM:HEAD_DIM + 1]).astype(o_ref.dtype)


def _moba(z_cd):
    t = z_cd.shape[0]
    nbm = t // MOBA_BLOCK
    nh = C_HEADS
    k_mean = _moba_kmean(z_cd)
    k_mean = jnp.pad(k_mean, ((0, MB_NBP - nbm), (0, 0)))
    return pl.pallas_call(
        _moba_kernel,
        grid=(nh, t // MB_TQ),
        in_specs=[pl.BlockSpec((MB_TQ, HEAD_DIM), lambda h, i: (i, h)),
                  pl.BlockSpec((MB_NBP, HEAD_DIM), lambda h, i: (0, h)),
                  _resident((t, HEAD_DIM), lambda h, i: (0, nh + h)),
                  _resident((t, HEAD_DIM), lambda h, i: (0, 2 * nh + h))],
        out_specs=pl.BlockSpec((MB_TQ, HEAD_DIM), lambda h, i: (i, h)),
        out_shape=jax.ShapeDtypeStruct((t, C_W), BF16),
        scratch_shapes=[pltpu.VMEM((t, HEAD_DIM + MB_NBP), BF16),
                        pltpu.VMEM((t, 2 * HEAD_DIM), BF16),
                        pltpu.VMEM((MB_TQ, HEAD_DIM + MB_NBP), BF16),
                        pltpu.VMEM((2, MB_TQ, MB_TK), F32),
                        pltpu.VMEM((MB_TQ, LANES), F32),
                        pltpu.VMEM((MB_TQ, 2 * HEAD_DIM), F32)],
        compiler_params=_cparams(("arbitrary", "arbitrary")),
    )(z_cd, k_mean, z_cd, z_cd)


DF_TQ = 1024
DF_TK = DF_TQ // 2


def _diff_kernel(q_ref, k_ref, v_ref, lq1_ref, lk1_ref, lq2_ref, lk2_ref, nw_ref, o_ref,
                 qs_ref, s_ref, m_ref, l_ref, acc_ref, *, lam_init):
    i = pl.program_id(1)
    tq = DF_TQ
    for mp in range(2):
        q = q_ref[:, mp * D_DK:(mp + 1) * D_DK].astype(F32) * (D_DK ** -0.5 * LOG2E)
        qs_ref[mp] = q.astype(BF16)
        _flash_init(m_ref.at[mp], l_ref.at[mp], acc_ref.at[mp])
    t_pos = i * tq + lax.broadcasted_iota(jnp.int32, (tq, 1), 0)

    def qk(slot, j):
        off = pl.multiple_of(j * DF_TK, DF_TK)
        for mp in range(2):
            s_ref[slot, mp] = _dot_nt(qs_ref[mp], k_ref[pl.ds(off, DF_TK), mp * D_DK:(mp + 1) * D_DK])

    def process(slot, j, causal):
        off = pl.multiple_of(j * DF_TK, DF_TK)
        v = v_ref[pl.ds(off, DF_TK), :]
        for mp in range(2):
            if causal:
                kpos = off + lax.broadcasted_iota(jnp.int32, (1, DF_TK), 1)
                _mask_scores(s_ref.at[slot, mp], kpos <= t_pos)
            _flash_update(s_ref.at[slot, mp], v, m_ref.at[mp], l_ref.at[mp], acc_ref.at[mp])

    _flash_sweep((i * tq + tq + DF_TK - 1) // DF_TK, qk, process)
    lam = (jnp.exp(jnp.sum(lq1_ref[...] * lk1_ref[...], axis=1, keepdims=True))
           - jnp.exp(jnp.sum(lq2_ref[...] * lk2_ref[...], axis=1, keepdims=True)) + lam_init)
    o = (acc_ref[0] / _lane_tile(l_ref[0], D_DV)
         - lam * (acc_ref[1] / _lane_tile(l_ref[1], D_DV)))
    o = o * lax.rsqrt(jnp.mean(o * o, axis=-1, keepdims=True) + RMS_EPS) * nw_ref[...]
    o_ref[...] = (o * (1.0 - lam_init)).astype(o_ref.dtype)


def _diff_attn(z_cd, lq1, lk1, lq2, lk2, subln_w, layer_idx):
    t = z_cd.shape[0]
    lam_init = 0.8 - 0.6 * math.exp(-0.3 * layer_idx)
    q_col0 = 3 * C_W // D_DV
    k_col0 = (3 * C_W + D_QW) // D_DV
    v_col0 = (3 * C_W + 2 * D_QW) // D_DV
    vec = pl.BlockSpec((1, D_DK), lambda h, i: (0, 0))
    return pl.pallas_call(
        functools.partial(_diff_kernel, lam_init=lam_init),
        grid=(D_HEADS, t // DF_TQ),
        in_specs=[pl.BlockSpec((DF_TQ, 2 * D_DK), lambda h, i: (i, q_col0 + h)),
                  _resident((t, 2 * D_DK), lambda h, i: (0, k_col0 + h)),
                  _resident((t, D_DV), lambda h, i: (0, v_col0 + h)),
                  vec, vec, vec, vec,
                  pl.BlockSpec((1, D_DV), lambda h, i: (0, 0))],
        out_specs=pl.BlockSpec((DF_TQ, D_DV), lambda h, i: (i, h)),
        out_shape=jax.ShapeDtypeStruct((t, D_VW), BF16),
        scratch_shapes=[pltpu.VMEM((2, DF_TQ, D_DK), BF16),
                        pltpu.VMEM((2, 2, DF_TQ, DF_TK), F32),
                        pltpu.VMEM((2, DF_TQ, LANES), F32),
                        pltpu.VMEM((2, DF_TQ, LANES), F32),
                        pltpu.VMEM((2, DF_TQ, D_DV), F32)],
        compiler_params=_cparams(("parallel", "arbitrary")),
    )(z_cd, z_cd, z_cd, lq1.reshape(1, D_DK), lk1.reshape(1, D_DK), lq2.reshape(1, D_DK),
      lk2.reshape(1, D_DK), subln_w.reshape(1, D_DV))


def _gate_weight(w_gate):
    d = w_gate.shape[0]
    per = 3 * B_HPG
    wg = w_gate.reshape(d, B_KV_GROUPS, per)
    wg = jnp.pad(wg, ((0, 0), (0, 0), (0, HEAD_DIM - per)))
    return wg.reshape(d, B_KV_GROUPS * HEAD_DIM)


def kernel(x, ln_w, ln_b, ffn_w_in, ffn_w_out, ab_w_in, ab_w_out, hgrn_lower_bounds, hgrn_norm_w,
           nsa_cmp_pos_k, nsa_cmp_k_w1, nsa_cmp_k_w2, nsa_cmp_pos_v, nsa_cmp_v_w1, nsa_cmp_v_w2,
           cd_w_in, cd_w_out, diff_lambda_q1, diff_lambda_k1, diff_lambda_q2, diff_lambda_k2, diff_subln_w):
    bsz, t, d = x.shape
    lb_all = jnp.cumsum(jax.nn.softmax(hgrn_lower_bounds.astype(F32), axis=0), axis=0)
    outs = []
    for bi in range(bsz):
        h = x[bi]
        h_in = h.astype(BF16)
        for layer in range(DEPTH):
            if layer % 2 == 0:
                e = layer // 2
                n_a = 4 * A_W
                n_b = B_QW + 6 * B_KVW
                z_a = _matmul(h_in, ab_w_in, e, 0, n_a, F32)
                z_b = _matmul(h_in, ab_w_in, e, n_a, n_b, BF16)
                w_gate = _gate_weight(ab_w_in[e, :, n_a + n_b:])
                z_g = _matmul(h_in, w_gate[None], 0, 0, w_gate.shape[1], F32)
                o_a = _hgrn2(z_a, lb_all[layer], hgrn_norm_w[e])
                o_b = _nsa(z_b, z_g, nsa_cmp_pos_k[e], nsa_cmp_k_w1[e], nsa_cmp_k_w2[e],
                           nsa_cmp_pos_v[e], nsa_cmp_v_w1[e], nsa_cmp_v_w2[e])
                mix_in = (o_a, o_b)
                w_out = ab_w_out[e]
            else:
                oi = layer // 2
                z_cd = _matmul(h_in, cd_w_in, oi, 0, cd_w_in.shape[2], BF16)
                o_c = _moba(z_cd)
                o_d = _diff_attn(z_cd, diff_lambda_q1[oi], diff_lambda_k1[oi], diff_lambda_q2[oi],
                                 diff_lambda_k2[oi], diff_subln_w[oi], layer)
                mix_in = (o_c, o_d)
                w_out = cd_w_out[oi]
            h, h_in = _mm_res_ln(mix_in, w_out.astype(BF16), h, ln_w[layer, 0], ln_b[layer, 0])
            act = _ffn_in(h_in, ffn_w_in, layer)
            h, h_in = _mm_res_ln((act,), ffn_w_out[layer].astype(BF16), h, ln_w[layer, 1], ln_b[layer, 1])
        outs.append(h)
    return jnp.stack(outs)
```

```python
import functools
import math

import numpy as np
import jax
import jax.numpy as jnp
from jax import lax
from jax.experimental import pallas as pl
from jax.experimental.pallas import tpu as pltpu

F32 = jnp.float32
BF16 = jnp.bfloat16

D_MODEL = 2048
DEPTH = 2
HEAD_DIM = 128
A_HEADS = 8
A_CHUNK = 64
B_HEADS = 8
B_KV_GROUPS = 2
B_HPG = B_HEADS // B_KV_GROUPS
CMP_LEN = 32
CMP_STRIDE = 16
SLC_LEN = 64
N_SLC = 16
N_FORCED = 3
WIN = 512
C_HEADS = 8
MOBA_BLOCK = 256
MOBA_TOPK = 3
D_HEADS = 4
D_DK = 128
D_DV = 2 * D_DK
D_FF = ((8 * D_MODEL + 3 * 256 - 1) // (3 * 256)) * 256

DEEPNORM_ALPHA = (2 * DEPTH) ** 0.25
NEG_INF = -1e30
FORCE_SCORE = 1e9
LN_EPS = 1e-5
RMS_EPS = 1e-6
ATT_SCALE = HEAD_DIM ** -0.5

A_W = A_HEADS * HEAD_DIM
B_QW = B_HEADS * HEAD_DIM
B_KVW = B_KV_GROUPS * HEAD_DIM
C_W = C_HEADS * HEAD_DIM
D_QW = D_HEADS * 2 * D_DK
D_VW = D_HEADS * D_DV

VMEM_LIMIT = 56 * 1024 * 1024
MM_SMALL_WEIGHT_BYTES = 8 * 1024 * 1024


def _cparams(sem):
    return pltpu.CompilerParams(dimension_semantics=sem, vmem_limit_bytes=VMEM_LIMIT)


def _dot(a, b):
    return jnp.dot(a, b, preferred_element_type=F32)


def _dot_nt(a, b):
    return lax.dot_general(a, b, (((1,), (1,)), ((), ())), preferred_element_type=F32)


def _dot_tn(a, b):
    return lax.dot_general(a, b, (((0,), (0,)), ((), ())), preferred_element_type=F32)


def _sigmoid(x):
    return 1.0 / (1.0 + jnp.exp(-x))


def _silu(x):
    return x * _sigmoid(x)


MM_TM = 1024
MM_TN = 512


def _mm_kernel(x_ref, w_ref, o_ref, wb_ref):
    @pl.when(pl.program_id(1) == 0)
    def _():
        wb_ref[...] = w_ref[0].astype(BF16)

    o_ref[...] = _dot(x_ref[...].astype(BF16), wb_ref[...]).astype(o_ref.dtype)


def _matmul(x, w, widx, col0, n, out_dtype):
    m, k = x.shape
    tn = min(MM_TN, n)
    j0 = col0 // tn
    return pl.pallas_call(
        _mm_kernel,
        grid=(n // tn, m // MM_TM),
        in_specs=[pl.BlockSpec((MM_TM, k), lambda j, i: (i, 0)),
                  pl.BlockSpec((1, k, tn), lambda j, i: (widx, 0, j0 + j))],
        out_specs=pl.BlockSpec((MM_TM, tn), lambda j, i: (i, j)),
        out_shape=jax.ShapeDtypeStruct((m, n), out_dtype),
        scratch_shapes=[pltpu.VMEM((k, tn), BF16)],
        compiler_params=_cparams(("arbitrary", "arbitrary")),
    )(x, w)


def _ffn_in_kernel(x_ref, wg_ref, wu_ref, o_ref, wgb_ref, wub_ref):
    @pl.when(pl.program_id(1) == 0)
    def _():
        wgb_ref[...] = wg_ref[0].astype(BF16)
        wub_ref[...] = wu_ref[0].astype(BF16)

    x = x_ref[...]
    g = _dot(x, wgb_ref[...])
    u = _dot(x, wub_ref[...])
    o_ref[...] = (_silu(g) * u).astype(o_ref.dtype)


def _ffn_in(x, w_in, layer):
    m, k = x.shape
    dff = w_in.shape[2] // 2
    nj = dff // MM_TN
    return pl.pallas_call(
        _ffn_in_kernel,
        grid=(nj, m // MM_TM),
        in_specs=[pl.BlockSpec((MM_TM, k), lambda j, i: (i, 0)),
                  pl.BlockSpec((1, k, MM_TN), lambda j, i: (layer, 0, j)),
                  pl.BlockSpec((1, k, MM_TN), lambda j, i: (layer, 0, j + nj))],
        out_specs=pl.BlockSpec((MM_TM, MM_TN), lambda j, i: (i, j)),
        out_shape=jax.ShapeDtypeStruct((m, dff), BF16),
        scratch_shapes=[pltpu.VMEM((k, MM_TN), BF16), pltpu.VMEM((k, MM_TN), BF16)],
        compiler_params=_cparams(("arbitrary", "arbitrary")),
    )(x, w_in, w_in)


def _layer_norm(y, w, b):
    mu = jnp.mean(y, axis=-1, keepdims=True)
    d = y - mu
    var = jnp.mean(d * d, axis=-1, keepdims=True)
    return d * lax.rsqrt(var + LN_EPS) * w + b


def _mm_res_ln_kernel(*refs, widths):
    part_refs = refs[:len(widths)]
    w_ref, h_ref, lnw_ref, lnb_ref, o_ref, obf_ref = refs[len(widths):]
    y = DEEPNORM_ALPHA * h_ref[...]
    k0 = 0
    for a_ref, width in zip(part_refs, widths):
        y = y + _dot(a_ref[...], w_ref[k0:k0 + width, :])
        k0 += width
    out = _layer_norm(y, lnw_ref[...], lnb_ref[...])
    o_ref[...] = out
    obf_ref[...] = out.astype(BF16)


def _mm_res_ln(parts, w, h, ln_w, ln_b):
    m = parts[0].shape[0]
    widths = tuple(a.shape[1] for a in parts)
    kdim, n = w.shape
    tm = 512 if kdim * n * 2 <= MM_SMALL_WEIGHT_BYTES else 256
    return pl.pallas_call(
        functools.partial(_mm_res_ln_kernel, widths=widths),
        grid=(m // tm,),
        in_specs=[pl.BlockSpec((tm, width), lambda i: (i, 0)) for width in widths]
        + [_resident((kdim, n), lambda i: (0, 0)),
           pl.BlockSpec((tm, n), lambda i: (i, 0)),
           pl.BlockSpec((1, n), lambda i: (0, 0)),
           pl.BlockSpec((1, n), lambda i: (0, 0))],
        out_specs=[pl.BlockSpec((tm, n), lambda i: (i, 0)),
                   pl.BlockSpec((tm, n), lambda i: (i, 0))],
        out_shape=[jax.ShapeDtypeStruct((m, n), F32), jax.ShapeDtypeStruct((m, n), BF16)],
        compiler_params=_cparams(("parallel",)),
    )(*parts, w, h, ln_w.reshape(1, n), ln_b.reshape(1, n))


LOG2E = math.log2(math.e)
LANES = 128
Q_SCALE = ATT_SCALE * LOG2E


def _flash_init(m_ref, l_ref, acc_ref):
    m_ref[...] = jnp.full(m_ref.shape, -jnp.inf, F32)
    if l_ref is not None:
        l_ref[...] = jnp.zeros(l_ref.shape, F32)
    acc_ref[...] = jnp.zeros(acc_ref.shape, F32)


def _flash_update(s_ref, v, m_ref, l_ref, acc_ref):
    m_prev = m_ref[...]
    m_new = jnp.maximum(m_prev, jnp.max(s_ref[...], axis=1, keepdims=True))
    m_ref[...] = m_new
    alpha = jnp.exp2(m_prev - m_new)
    p = jnp.exp2(s_ref[...] - _lane_tile(m_new, s_ref.shape[1]))
    if l_ref is not None:
        l_ref[...] = alpha * l_ref[...] + jnp.sum(p, axis=1, keepdims=True)
    acc_ref[...] = _lane_tile(alpha, acc_ref.shape[1]) * acc_ref[...] + _dot(p.astype(BF16), v)


def _mask_scores(s_ref, visible):
    s_ref[...] = jnp.where(visible, s_ref[...], NEG_INF)


def _lane_tile(x, width):
    reps = width // x.shape[1]
    return x if reps == 1 else jnp.concatenate([x] * reps, axis=1)


def _flash_sweep(n_tiles, qk, process):
    n_pairs = (n_tiles + 1) // 2
    qk(0, 0)

    def pair(jj, carry):
        j = 2 * jj
        qk(1, j + 1)
        process(0, j, False)
        qk(0, j + 2)
        process(1, j + 1, False)
        return carry

    lax.fori_loop(0, n_pairs - 1, pair, 0)
    j_last = 2 * (n_pairs - 1)
    qk(1, j_last + 1)
    process(0, j_last, True)
    process(1, j_last + 1, True)


AUG_FILL_ROWS = 512


def _fill_augmented(k_ref, v_ref, ka_ref, va_ref, block_len):
    lane = lax.broadcasted_iota(jnp.int32, (AUG_FILL_ROWS, LANES), 1)
    ones_col = jnp.where(lane == 0, 1.0, 0.0).astype(BF16)

    def fill(c, carry):
        r0 = pl.multiple_of(c * AUG_FILL_ROWS, AUG_FILL_ROWS)
        rows = pl.ds(r0, AUG_FILL_ROWS)
        blk = (r0 + lax.broadcasted_iota(jnp.int32, (AUG_FILL_ROWS, LANES), 0)) // block_len
        ka_ref[rows, 0:HEAD_DIM] = k_ref[rows, :]
        ka_ref[rows, HEAD_DIM:] = jnp.where((blk & (LANES - 1)) == lane, 1.0, 0.0).astype(BF16)
        va_ref[rows, 0:HEAD_DIM] = v_ref[rows, :]
        va_ref[rows, HEAD_DIM:] = ones_col
        return carry

    lax.fori_loop(0, k_ref.shape[0] // AUG_FILL_ROWS, fill, 0)


def _resident(block_shape, index_map):
    return pl.BlockSpec(block_shape, index_map, pipeline_mode=pl.Buffered(1))


HG_TILE = 512
HG_HEADS = 4


def _hgrn_kernel(q_ref, f_ref, i_ref, g_ref, lb_ref, nw_ref, o_ref, st_ref):
    @pl.when(pl.program_id(1) == 0)
    def _():
        st_ref[...] = jnp.zeros_like(st_ref)

    lb = lb_ref[...]
    f = lb + (1.0 - lb) * _sigmoid(f_ref[...])
    logf = jnp.log(f)
    kk = 1.0 - f
    qf = _silu(q_ref[...])
    r64 = lax.broadcasted_iota(jnp.int32, logf.shape, 0) & (A_CHUNK - 1)
    b = logf
    step = 1
    while step < A_CHUNK:
        b = b + jnp.where(r64 >= step, pltpu.roll(b, step, 0), 0.0)
        step *= 2
    causal = (lax.broadcasted_iota(jnp.int32, (A_CHUNK, A_CHUNK), 0)
              >= lax.broadcasted_iota(jnp.int32, (A_CHUNK, A_CHUNK), 1))
    nw = nw_ref[...]
    for c in range(HG_TILE // A_CHUNK):
        sl = slice(c * A_CHUNK, (c + 1) * A_CHUNK)
        for hd in range(HG_HEADS):
            hs = slice(hd * HEAD_DIM, (hd + 1) * HEAD_DIM)
            bc = b[sl, hs]
            b_last = bc[A_CHUNK - 1:A_CHUNK, :]
            q_t = (qf[sl, hs] * jnp.exp(bc)).astype(BF16)
            k_t = (kk[sl, hs] * jnp.exp(-bc)).astype(BF16)
            vc = i_ref[sl, hs].astype(BF16)
            att = jnp.where(causal, _dot_nt(q_t, k_t), 0.0)
            st = st_ref[hd]
            o = _dot(att.astype(BF16), vc) + _dot_nt(q_t, st.astype(BF16))
            kdec = (kk[sl, hs] * jnp.exp(b_last - bc)).astype(BF16)
            st_ref[hd] = st * jnp.exp(b_last) + _dot_tn(vc, kdec)
            o = o * lax.rsqrt(jnp.mean(o * o, axis=-1, keepdims=True) + RMS_EPS) * nw
            o_ref[sl, hs] = (o * _silu(g_ref[sl, hs])).astype(o_ref.dtype)


def _hgrn2(z_a, lb, norm_w):
    t = z_a.shape[0]
    ngrp = A_HEADS // HG_HEADS
    gw = HG_HEADS * HEAD_DIM
    col = lambda base: (lambda h, i: (i, base + h))
    return pl.pallas_call(
        _hgrn_kernel,
        grid=(ngrp, t // HG_TILE),
        in_specs=[pl.BlockSpec((HG_TILE, gw), col(0)),
                  pl.BlockSpec((HG_TILE, gw), col(ngrp)),
                  pl.BlockSpec((HG_TILE, gw), col(2 * ngrp)),
                  pl.BlockSpec((HG_TILE, gw), col(3 * ngrp)),
                  pl.BlockSpec((1, gw), lambda h, i: (0, h)),
                  pl.BlockSpec((1, HEAD_DIM), lambda h, i: (0, 0))],
        out_specs=pl.BlockSpec((HG_TILE, gw), lambda h, i: (i, h)),
        out_shape=jax.ShapeDtypeStruct((t, A_W), BF16),
        scratch_shapes=[pltpu.VMEM((HG_HEADS, HEAD_DIM, HEAD_DIM), F32)],
        compiler_params=_cparams(("parallel", "arbitrary")),
    )(z_a, z_a, z_a, z_a, lb.reshape(1, A_W), norm_w.reshape(1, HEAD_DIM))


SEG_W = CMP_STRIDE * HEAD_DIM


def _nsa_compress_kernel(seg_ref, pos_ref, w1_ref, w2_ref, o_ref):
    seg = seg_ref[0, 0]
    nseg = seg.shape[0]
    a = _dot(seg, w1_ref[0, :SEG_W, :])
    b = _dot(seg, w1_ref[0, SEG_W:, :])
    b_next = pltpu.roll(b, nseg - 1, 0)
    pos = jnp.broadcast_to(pos_ref[0], (8, CMP_LEN * HEAD_DIM))
    c = _dot(pos, w1_ref[0])[0:1, :]
    hid = _silu(a + b_next + c)
    o_ref[0, 0] = _dot(hid.astype(BF16), w2_ref[0]).astype(o_ref.dtype)


def _nsa_compress(seg, pos, w1, w2):
    _, ng, nseg, _ = seg.shape
    return pl.pallas_call(
        _nsa_compress_kernel,
        grid=(2, ng),
        in_specs=[pl.BlockSpec((1, 1, nseg, SEG_W), lambda a, g: (a, g, 0, 0)),
                  pl.BlockSpec((1, 1, CMP_LEN * HEAD_DIM), lambda a, g: (a, 0, 0)),
                  pl.BlockSpec((1, CMP_LEN * HEAD_DIM, HEAD_DIM), lambda a, g: (a, 0, 0)),
                  pl.BlockSpec((1, HEAD_DIM, HEAD_DIM), lambda a, g: (a, 0, 0))],
        out_specs=pl.BlockSpec((1, 1, nseg, HEAD_DIM), lambda a, g: (a, g, 0, 0)),
        out_shape=jax.ShapeDtypeStruct((2, ng, nseg, HEAD_DIM), BF16),
        compiler_params=_cparams(("parallel", "parallel")),
    )(seg, pos, w1, w2)


CS_TQ = 256
CS_WIDTH_STEP = 256
SL_NBP = 128


def _split3(x):
    hi = x.astype(BF16)
    r = x - hi.astype(F32)
    mid = r.astype(BF16)
    lo = (r - mid.astype(F32)).astype(BF16)
    return hi, mid, lo


def _topk_mask(score, lane, k):
    sel = jnp.zeros(score.shape, F32)
    width = float(score.shape[1])
    for _ in range(k):
        mx = jnp.max(score, axis=1, keepdims=True)
        idx = jnp.min(jnp.where(score == mx, lane, width), axis=1, keepdims=True)
        hit = lane == idx
        sel = jnp.where(hit, 1.0, sel)
        score = jnp.where(hit, -jnp.inf, score)
    return sel


def _nsa_cmp_kernel(q_ref, kc_ref, vc_ref, map_ref, o_ref, sel_ref, *, n_sel):
    i = pl.program_id(1)
    tq = q_ref.shape[0]
    ncmp = kc_ref.shape[2]
    nbp = map_ref.shape[1]
    t_pos = i * tq + lax.broadcasted_iota(jnp.int32, (tq, 1), 0)
    row_has_keys = t_pos >= CMP_LEN - 1
    jb = lax.broadcasted_iota(jnp.int32, (1, nbp), 1)
    cur = t_pos // SLC_LEN
    valid = jb <= cur
    forced = (jb == 0) | (jb == cur) | (jb == cur - 1)

    def body(width):
        cmp_end = lax.broadcasted_iota(jnp.int32, (1, width), 1) * CMP_STRIDE + (CMP_LEN - 1)
        mask = cmp_end <= t_pos
        kc = kc_ref[0, 0, 0:width, :]
        vc = vc_ref[0, 0, 0:width, :]
        psum = jnp.zeros((tq, width), F32)
        for hh in range(B_HPG):
            hs = slice(hh * HEAD_DIM, (hh + 1) * HEAD_DIM)
            s = jnp.where(mask, _dot_nt(q_ref[:, hs], kc) * Q_SCALE, NEG_INF)
            e = jnp.exp2(s - jnp.max(s, axis=1, keepdims=True))
            p = e * jnp.where(row_has_keys, 1.0 / jnp.sum(e, axis=1, keepdims=True), 0.0)
            o_ref[:, hs] = _dot(p.astype(BF16), vc).astype(o_ref.dtype)
            psum = psum + p
        hi, mid, lo = _split3(psum)
        cmap = map_ref[0:width, :]
        p_slc = _dot(hi, cmap) + _dot(mid, cmap) + _dot(lo, cmap)
        score = jnp.where(valid & jnp.logical_not(forced), p_slc, NEG_INF)
        sel = _topk_mask(score, jb.astype(F32), n_sel - N_FORCED)
        sel_ref[0] = jnp.where(valid, jnp.where(forced, 1.0, sel), 0.0).astype(sel_ref.dtype)

    widths = list(range(CS_WIDTH_STEP, ncmp + 1, CS_WIDTH_STEP)) if ncmp % CS_WIDTH_STEP == 0 else [ncmp]
    if len(widths) == 1:
        body(widths[0])
    else:
        n_vis = ((i + 1) * tq - CMP_LEN) // CMP_STRIDE + 1
        case = jnp.clip((n_vis + CS_WIDTH_STEP - 1) // CS_WIDTH_STEP - 1, 0, len(widths) - 1)
        for idx, width in enumerate(widths):
            pl.when(case == idx)(functools.partial(body, width))


def _cmp_to_slc_matrix(ncmp_pad, n_cmp, nb, nbp):
    ratio = SLC_LEN // CMP_STRIDE
    n_over = CMP_LEN // CMP_STRIDE
    mat = np.zeros((ncmp_pad, nbp), np.float32)
    for j in range(nb):
        for m in range(ratio):
            for n in range(n_over):
                c = ratio * j + m - n
                if 0 <= c < n_cmp:
                    mat[c, j] += 1.0
    return mat


def _nsa_cmp_select(z_b, cmp_kv):
    t = z_b.shape[0]
    ng = B_KV_GROUPS
    nseg = cmp_kv.shape[2]
    nb = t // SLC_LEN
    n_cmp = (t - CMP_LEN) // CMP_STRIDE + 1
    nbp = -(-nb // SL_NBP) * SL_NBP
    cmap = jnp.asarray(_cmp_to_slc_matrix(nseg, n_cmp, nb, nbp), BF16)
    gw = B_HPG * HEAD_DIM
    return pl.pallas_call(
        functools.partial(_nsa_cmp_kernel, n_sel=min(N_SLC, nb)),
        grid=(ng, t // CS_TQ),
        in_specs=[pl.BlockSpec((CS_TQ, gw), lambda g, i: (i, g)),
                  pl.BlockSpec((1, 1, nseg, HEAD_DIM), lambda g, i: (0, g, 0, 0)),
                  pl.BlockSpec((1, 1, nseg, HEAD_DIM), lambda g, i: (1, g, 0, 0)),
                  pl.BlockSpec((nseg, nbp), lambda g, i: (0, 0))],
        out_specs=[pl.BlockSpec((CS_TQ, gw), lambda g, i: (i, g)),
                   pl.BlockSpec((1, CS_TQ, nbp), lambda g, i: (g, i, 0))],
        out_shape=[jax.ShapeDtypeStruct((t, B_QW), BF16),
                   jax.ShapeDtypeStruct((ng, t, nbp), BF16)],
        compiler_params=_cparams(("parallel", "parallel")),
    )(z_b, cmp_kv, cmp_kv, cmap)


SL_TQ = 512
SL_TK = SL_TQ
SL_PHASE_TILES = SL_NBP * SLC_LEN // SL_TK


def _nsa_slc_kernel(q_ref, sel_ref, k_ref, v_ref, o_ref, ka_ref, va_ref, qa_ref, s_ref, m_ref, acc_ref, *, n_phase):
    i = pl.program_id(1)
    tq = SL_TQ

    @pl.when(i == 0)
    def _():
        _fill_augmented(k_ref, v_ref, ka_ref, va_ref, SLC_LEN)

    bias = ((sel_ref[0].astype(F32) - 1.0) * (-NEG_INF)).astype(BF16)
    for hh in range(B_HPG):
        rows = slice(hh * tq, (hh + 1) * tq)
        q = (q_ref[:, hh * HEAD_DIM:(hh + 1) * HEAD_DIM].astype(F32) * Q_SCALE).astype(BF16)
        for ph in range(n_phase):
            qa_ref[ph, rows, 0:HEAD_DIM] = q
            qa_ref[ph, rows, HEAD_DIM:] = bias[:, ph * SL_NBP:(ph + 1) * SL_NBP]
    _flash_init(m_ref, None, acc_ref)
    t_pos = i * tq + (lax.broadcasted_iota(jnp.int32, (B_HPG * tq, 1), 0) & (tq - 1))

    def qk(slot, j):
        off = pl.multiple_of(j * SL_TK, SL_TK)
        qa = qa_ref[0] if n_phase == 1 else qa_ref[j // SL_PHASE_TILES]
        s_ref[slot] = _dot_nt(qa, ka_ref[pl.ds(off, SL_TK), :])

    def process(slot, j, causal):
        off = pl.multiple_of(j * SL_TK, SL_TK)
        if causal:
            kpos = off + lax.broadcasted_iota(jnp.int32, (1, SL_TK), 1)
            _mask_scores(s_ref.at[slot], kpos <= t_pos)
        _flash_update(s_ref.at[slot], va_ref[pl.ds(off, SL_TK), :], m_ref, None, acc_ref)

    _flash_sweep((i * tq + tq + SL_TK - 1) // SL_TK, qk, process)
    acc = acc_ref[...]
    o = acc[:, 0:HEAD_DIM] / acc[:, HEAD_DIM:HEAD_DIM + 1]
    for hh in range(B_HPG):
        o_ref[:, hh * HEAD_DIM:(hh + 1) * HEAD_DIM] = o[hh * tq:(hh + 1) * tq].astype(o_ref.dtype)


def _nsa_slc(z_b, sel):
    t = z_b.shape[0]
    nbp = sel.shape[2]
    n_phase = nbp // SL_NBP
    gw = B_HPG * HEAD_DIM
    rows = B_HPG * SL_TQ
    k_col0 = (B_QW + 2 * B_KVW) // HEAD_DIM
    v_col0 = (B_QW + 3 * B_KVW) // HEAD_DIM
    return pl.pallas_call(
        functools.partial(_nsa_slc_kernel, n_phase=n_phase),
        grid=(B_KV_GROUPS, t // SL_TQ),
        in_specs=[pl.BlockSpec((SL_TQ, gw), lambda g, i: (i, g)),
                  pl.BlockSpec((1, SL_TQ, nbp), lambda g, i: (g, i, 0)),
                  _resident((t, HEAD_DIM), lambda g, i: (0, k_col0 + g)),
                  _resident((t, HEAD_DIM), lambda g, i: (0, v_col0 + g))],
        out_specs=pl.BlockSpec((SL_TQ, gw), lambda g, i: (i, g)),
        out_shape=jax.ShapeDtypeStruct((t, B_QW), BF16),
        scratch_shapes=[pltpu.VMEM((t, HEAD_DIM + SL_NBP), BF16),
                        pltpu.VMEM((t, 2 * HEAD_DIM), BF16),
                        pltpu.VMEM((n_phase, rows, HEAD_DIM + SL_NBP), BF16),
                        pltpu.VMEM((2, rows, SL_TK), F32),
                        pltpu.VMEM((rows, LANES), F32),
                        pltpu.VMEM((rows, 2 * HEAD_DIM), F32)],
        compiler_params=_cparams(("arbitrary", "arbitrary")),
    )(z_b, sel, z_b, z_b)


WN_TQ = WIN


def _nsa_win_kernel(q_ref, ka_ref, kb_ref, va_ref, vb_ref, oc_ref, os_ref, gate_ref, o_ref, qs_ref):
    i = pl.program_id(1)
    tq = WN_TQ
    for hh in range(B_HPG):
        q = q_ref[:, hh * HEAD_DIM:(hh + 1) * HEAD_DIM].astype(F32) * Q_SCALE
        qs_ref[hh * tq:(hh + 1) * tq, :] = q.astype(BF16)
    qs = qs_ref[...]
    t_loc = lax.broadcasted_iota(jnp.int32, (B_HPG * tq, 1), 0) & (tq - 1)
    c_loc = lax.broadcasted_iota(jnp.int32, (1, tq), 1)
    t_prev = t_loc + jnp.where(i > 0, 0, tq)
    s_a = jnp.where(c_loc > t_prev, _dot_nt(qs, ka_ref[...]), NEG_INF)
    s_b = jnp.where(c_loc <= t_loc, _dot_nt(qs, kb_ref[...]), NEG_INF)
    m = jnp.maximum(jnp.max(s_a, axis=1, keepdims=True), jnp.max(s_b, axis=1, keepdims=True))
    p_a = jnp.exp2(s_a - m)
    p_b = jnp.exp2(s_b - m)
    l = jnp.sum(p_a, axis=1, keepdims=True) + jnp.sum(p_b, axis=1, keepdims=True)
    o_w = (_dot(p_a.astype(BF16), va_ref[...]) + _dot(p_b.astype(BF16), vb_ref[...])) / l
    gates = _sigmoid(gate_ref[...])
    for hh in range(B_HPG):
        hs = slice(hh * HEAD_DIM, (hh + 1) * HEAD_DIM)
        g_c, g_s, g_w = (gates[:, 3 * hh + c:3 * hh + c + 1] for c in range(3))
        o = (g_c * oc_ref[:, hs].astype(F32) + g_s * os_ref[:, hs].astype(F32)
             + g_w * o_w[hh * tq:(hh + 1) * tq])
        o_ref[:, hs] = o.astype(o_ref.dtype)


def _nsa_win_combine(z_b, o_c, o_s, z_gate):
    t = z_b.shape[0]
    gw = B_HPG * HEAD_DIM
    k_col0 = (B_QW + 4 * B_KVW) // HEAD_DIM
    v_col0 = (B_QW + 5 * B_KVW) // HEAD_DIM
    prev_tile = lambda col0: (lambda g, i: (jnp.maximum(i - 1, 0), col0 + g))
    this_tile = lambda col0: (lambda g, i: (i, col0 + g))
    kv_block = (WN_TQ, HEAD_DIM)
    return pl.pallas_call(
        _nsa_win_kernel,
        grid=(B_KV_GROUPS, t // WN_TQ),
        in_specs=[pl.BlockSpec((WN_TQ, gw), lambda g, i: (i, g)),
                  pl.BlockSpec(kv_block, prev_tile(k_col0)),
                  pl.BlockSpec(kv_block, this_tile(k_col0)),
                  pl.BlockSpec(kv_block, prev_tile(v_col0)),
                  pl.BlockSpec(kv_block, this_tile(v_col0)),
                  pl.BlockSpec((WN_TQ, gw), lambda g, i: (i, g)),
                  pl.BlockSpec((WN_TQ, gw), lambda g, i: (i, g)),
                  pl.BlockSpec((WN_TQ, HEAD_DIM), lambda g, i: (i, g))],
        out_specs=pl.BlockSpec((WN_TQ, gw), lambda g, i: (i, g)),
        out_shape=jax.ShapeDtypeStruct((t, B_QW), BF16),
        scratch_shapes=[pltpu.VMEM((B_HPG * WN_TQ, HEAD_DIM), BF16)],
        compiler_params=_cparams(("parallel", "parallel")),
    )(z_b, z_b, z_b, z_b, z_b, o_c, o_s, z_gate)


def _nsa(z_b, z_gate, pos_k, w1k, w2k, pos_v, w1v, w2v):
    t = z_b.shape[0]
    ng = B_KV_GROUPS

    def segs(col0):
        z = z_b[:, col0:col0 + B_KVW].reshape(t, ng, HEAD_DIM).transpose(1, 0, 2)
        return z.reshape(ng, t // CMP_STRIDE, SEG_W)

    seg = jnp.stack([segs(B_QW), segs(B_QW + B_KVW)])
    pos = jnp.stack([pos_k, pos_v]).reshape(2, 1, CMP_LEN * HEAD_DIM).astype(BF16)
    w1 = jnp.stack([w1k, w1v]).astype(BF16)
    w2 = jnp.stack([w2k, w2v]).astype(BF16)
    cmp_kv = _nsa_compress(seg, pos, w1, w2)
    o_c, sel = _nsa_cmp_select(z_b, cmp_kv)
    o_s = _nsa_slc(z_b, sel)
    return _nsa_win_combine(z_b, o_c, o_s, z_gate)


KM_ROWS = 8


def _kmean_kernel(k_ref, o_ref):
    x = k_ref[...].astype(F32)
    o_ref[...] = jnp.mean(x.reshape(KM_ROWS, MOBA_BLOCK, x.shape[1]), axis=1)


def _moba_kmean(z_cd):
    t = z_cd.shape[0]
    nbm = t // MOBA_BLOCK
    return pl.pallas_call(
        _kmean_kernel,
        grid=(nbm // KM_ROWS,),
        in_specs=[pl.BlockSpec((KM_ROWS * MOBA_BLOCK, C_W), lambda i: (i, 1))],
        out_specs=pl.BlockSpec((KM_ROWS, C_W), lambda i: (i, 0)),
        out_shape=jax.ShapeDtypeStruct((nbm, C_W), F32),
        compiler_params=_cparams(("parallel",)),
    )(z_cd)


MB_TQ = 1024
MB_TK = MB_TQ // 2
MB_NBP = 128


def _moba_kernel(q_ref, km_ref, k_ref, v_ref, o_ref, ka_ref, va_ref, qa_ref, s_ref, m_ref, acc_ref):
    i = pl.program_id(1)
    tq = MB_TQ

    @pl.when(i == 0)
    def _():
        _fill_augmented(k_ref, v_ref, ka_ref, va_ref, MOBA_BLOCK)

    q = q_ref[...]
    km = km_ref[...]
    km_hi = km.astype(BF16)
    km_lo = (km - km_hi.astype(F32)).astype(BF16)
    gate = _dot_nt(q, km_hi) + _dot_nt(q, km_lo)
    jb = lax.broadcasted_iota(jnp.int32, (1, MB_NBP), 1)
    t_pos = i * tq + lax.broadcasted_iota(jnp.int32, (tq, 1), 0)
    cur = t_pos // MOBA_BLOCK
    earlier = jb < cur
    sel = _topk_mask(jnp.where(earlier, gate, NEG_INF), jb.astype(F32), MOBA_TOPK)
    sel = jnp.where(jb == cur, 1.0, jnp.where(earlier, sel, 0.0))
    qa_ref[:, 0:HEAD_DIM] = (q.astype(F32) * Q_SCALE).astype(BF16)
    qa_ref[:, HEAD_DIM:] = ((sel - 1.0) * (-NEG_INF)).astype(BF16)
    _flash_init(m_ref, None, acc_ref)

    def qk(slot, j):
        off = pl.multiple_of(j * MB_TK, MB_TK)
        s_ref[slot] = _dot_nt(qa_ref[...], ka_ref[pl.ds(off, MB_TK), :])

    def process(slot, j, causal):
        off = pl.multiple_of(j * MB_TK, MB_TK)
        if causal:
            kpos = off + lax.broadcasted_iota(jnp.int32, (1, MB_TK), 1)
            _mask_scores(s_ref.at[slot], kpos <= t_pos)
        _flash_update(s_ref.at[slot], va_ref[pl.ds(off, MB_TK), :], m_ref, None, acc_ref)

    _flash_sweep((i * tq + tq + MB_TK - 1) // MB_TK, qk, process)
    acc = acc_ref[...]
    o_ref[...] = (acc[:, 0:HEAD_DIM] / acc[:, HEAD_DIM:HEAD_DIM + 1]).astype(o_ref.dtype)


def _moba(z_cd):
    t = z_cd.shape[0]
    nbm = t // MOBA_BLOCK
    nh = C_HEADS
    k_mean = _moba_kmean(z_cd)
    k_mean = jnp.pad(k_mean, ((0, MB_NBP - nbm), (0, 0)))
    return pl.pallas_call(
        _moba_kernel,
        grid=(nh, t // MB_TQ),
        in_specs=[pl.BlockSpec((MB_TQ, HEAD_DIM), lambda h, i: (i, h)),
                  pl.BlockSpec((MB_NBP, HEAD_DIM), lambda h, i: (0, h)),
                  _resident((t, HEAD_DIM), lambda h, i: (0, nh + h)),
                  _resident((t, HEAD_DIM), lambda h, i: (0, 2 * nh + h))],
        out_specs=pl.BlockSpec((MB_TQ, HEAD_DIM), lambda h, i: (i, h)),
        out_shape=jax.ShapeDtypeStruct((t, C_W), BF16),
        scratch_shapes=[pltpu.VMEM((t, HEAD_DIM + MB_NBP), BF16),
                        pltpu.VMEM((t, 2 * HEAD_DIM), BF16),
                        pltpu.VMEM((MB_TQ, HEAD_DIM + MB_NBP), BF16),
                        pltpu.VMEM((2, MB_TQ, MB_TK), F32),
                        pltpu.VMEM((MB_TQ, LANES), F32),
                        pltpu.VMEM((MB_TQ, 2 * HEAD_DIM), F32)],
        compiler_params=_cparams(("arbitrary", "arbitrary")),
    )(z_cd, k_mean, z_cd, z_cd)


DF_TQ = 1024
DF_TK = DF_TQ // 2


def _diff_kernel(q_ref, k_ref, v_ref, lq1_ref, lk1_ref, lq2_ref, lk2_ref, nw_ref, o_ref,
                 qs_ref, s_ref, m_ref, l_ref, acc_ref, *, lam_init):
    i = pl.program_id(1)
    tq = DF_TQ
    for mp in range(2):
        q = q_ref[:, mp * D_DK:(mp + 1) * D_DK].astype(F32) * (D_DK ** -0.5 * LOG2E)
        qs_ref[mp] = q.astype(BF16)
    _flash_init(m_ref, l_ref, acc_ref)
    t_pos = i * tq + (lax.broadcasted_iota(jnp.int32, (2 * tq, 1), 0) & (tq - 1))

    def qk(slot, j):
        off = pl.multiple_of(j * DF_TK, DF_TK)
        for mp in range(2):
            s_ref[slot, mp * tq:(mp + 1) * tq, :] = _dot_nt(
                qs_ref[mp], k_ref[pl.ds(off, DF_TK), mp * D_DK:(mp + 1) * D_DK])

    def process(slot, j, causal):
        off = pl.multiple_of(j * DF_TK, DF_TK)
        if causal:
            kpos = off + lax.broadcasted_iota(jnp.int32, (1, DF_TK), 1)
            _mask_scores(s_ref.at[slot], kpos <= t_pos)
        _flash_update(s_ref.at[slot], v_ref[pl.ds(off, DF_TK), :], m_ref, l_ref, acc_ref)

    _flash_sweep((i * tq + tq + DF_TK - 1) // DF_TK, qk, process)
    lam = (jnp.exp(jnp.sum(lq1_ref[...] * lk1_ref[...], axis=1, keepdims=True))
           - jnp.exp(jnp.sum(lq2_ref[...] * lk2_ref[...], axis=1, keepdims=True)) + lam_init)
    o = (acc_ref[0:tq, :] / _lane_tile(l_ref[0:tq, :], D_DV)
         - lam * (acc_ref[tq:, :] / _lane_tile(l_ref[tq:, :], D_DV)))
    o = o * lax.rsqrt(jnp.mean(o * o, axis=-1, keepdims=True) + RMS_EPS) * nw_ref[...]
    o_ref[...] = (o * (1.0 - lam_init)).astype(o_ref.dtype)


def _diff_attn(z_cd, lq1, lk1, lq2, lk2, subln_w, layer_idx):
    t = z_cd.shape[0]
    lam_init = 0.8 - 0.6 * math.exp(-0.3 * layer_idx)
    q_col0 = 3 * C_W // D_DV
    k_col0 = (3 * C_W + D_QW) // D_DV
    v_col0 = (3 * C_W + 2 * D_QW) // D_DV
    vec = pl.BlockSpec((1, D_DK), lambda h, i: (0, 0))
    return pl.pallas_call(
        functools.partial(_diff_kernel, lam_init=lam_init),
        grid=(D_HEADS, t // DF_TQ),
        in_specs=[pl.BlockSpec((DF_TQ, 2 * D_DK), lambda h, i: (i, q_col0 + h)),
                  _resident((t, 2 * D_DK), lambda h, i: (0, k_col0 + h)),
                  _resident((t, D_DV), lambda h, i: (0, v_col0 + h)),
                  vec, vec, vec, vec,
                  pl.BlockSpec((1, D_DV), lambda h, i: (0, 0))],
        out_specs=pl.BlockSpec((DF_TQ, D_DV), lambda h, i: (i, h)),
        out_shape=jax.ShapeDtypeStruct((t, D_VW), BF16),
        scratch_shapes=[pltpu.VMEM((2, DF_TQ, D_DK), BF16),
                        pltpu.VMEM((2, 2 * DF_TQ, DF_TK), F32),
                        pltpu.VMEM((2 * DF_TQ, LANES), F32),
                        pltpu.VMEM((2 * DF_TQ, LANES), F32),
                        pltpu.VMEM((2 * DF_TQ, D_DV), F32)],
        compiler_params=_cparams(("parallel", "arbitrary")),
    )(z_cd, z_cd, z_cd, lq1.reshape(1, D_DK), lk1.reshape(1, D_DK), lq2.reshape(1, D_DK),
      lk2.reshape(1, D_DK), subln_w.reshape(1, D_DV))


def _gate_weight(w_gate):
    d = w_gate.shape[0]
    per = 3 * B_HPG
    wg = w_gate.reshape(d, B_KV_GROUPS, per)
    wg = jnp.pad(wg, ((0, 0), (0, 0), (0, HEAD_DIM - per)))
    return wg.reshape(d, B_KV_GROUPS * HEAD_DIM)


def kernel(x, ln_w, ln_b, ffn_w_in, ffn_w_out, ab_w_in, ab_w_out, hgrn_lower_bounds, hgrn_norm_w,
           nsa_cmp_pos_k, nsa_cmp_k_w1, nsa_cmp_k_w2, nsa_cmp_pos_v, nsa_cmp_v_w1, nsa_cmp_v_w2,
           cd_w_in, cd_w_out, diff_lambda_q1, diff_lambda_k1, diff_lambda_q2, diff_lambda_k2, diff_subln_w):
    bsz, t, d = x.shape
    lb_all = jnp.cumsum(jax.nn.softmax(hgrn_lower_bounds.astype(F32), axis=0), axis=0)
    outs = []
    for bi in range(bsz):
        h = x[bi]
        h_in = h.astype(BF16)
        for layer in range(DEPTH):
            if layer % 2 == 0:
                e = layer // 2
                n_a = 4 * A_W
                n_b = B_QW + 6 * B_KVW
                z_a = _matmul(h_in, ab_w_in, e, 0, n_a, F32)
                z_b = _matmul(h_in, ab_w_in, e, n_a, n_b, BF16)
                w_gate = _gate_weight(ab_w_in[e, :, n_a + n_b:])
                z_g = _matmul(h_in, w_gate[None], 0, 0, w_gate.shape[1], F32)
                o_a = _hgrn2(z_a, lb_all[layer], hgrn_norm_w[e])
                o_b = _nsa(z_b, z_g, nsa_cmp_pos_k[e], nsa_cmp_k_w1[e], nsa_cmp_k_w2[e],
                           nsa_cmp_pos_v[e], nsa_cmp_v_w1[e], nsa_cmp_v_w2[e])
                mix_in = (o_a, o_b)
                w_out = ab_w_out[e]
            else:
                oi = layer // 2
                z_cd = _matmul(h_in, cd_w_in, oi, 0, cd_w_in.shape[2], BF16)
                o_c = _moba(z_cd)
                o_d = _diff_attn(z_cd, diff_lambda_q1[oi], diff_lambda_k1[oi], diff_lambda_q2[oi],
                                 diff_lambda_k2[oi], diff_subln_w[oi], layer)
                mix_in = (o_c, o_d)
                w_out = cd_w_out[oi]
            h, h_in = _mm_res_ln(mix_in, w_out.astype(BF16), h, ln_w[layer, 0], ln_b[layer, 0])
            act = _ffn_in(h_in, ffn_w_in, layer)
            h, h_in = _mm_res_ln((act,), ffn_w_out[layer].astype(BF16), h, ln_w[layer, 1], ln_b[layer, 1])
        outs.append(h)
    return jnp.stack(outs)
```

```python
import functools
import math

import numpy as np
import jax
import jax.numpy as jnp
from jax import lax
from jax.experimental import pallas as pl
from jax.experimental.pallas import tpu as pltpu

F32 = jnp.float32
BF16 = jnp.bfloat16

D_MODEL = 2048
DEPTH = 2
HEAD_DIM = 128
A_HEADS = 8
A_CHUNK = 64
B_HEADS = 8
B_KV_GROUPS = 2
B_HPG = B_HEADS // B_KV_GROUPS
CMP_LEN = 32
CMP_STRIDE = 16
SLC_LEN = 64
N_SLC = 16
N_FORCED = 3
WIN = 512
C_HEADS = 8
MOBA_BLOCK = 256
MOBA_TOPK = 3
D_HEADS = 4
D_DK = 128
D_DV = 2 * D_DK
D_FF = ((8 * D_MODEL + 3 * 256 - 1) // (3 * 256)) * 256

DEEPNORM_ALPHA = (2 * DEPTH) ** 0.25
NEG_INF = -1e30
FORCE_SCORE = 1e9
LN_EPS = 1e-5
RMS_EPS = 1e-6
ATT_SCALE = HEAD_DIM ** -0.5

A_W = A_HEADS * HEAD_DIM
B_QW = B_HEADS * HEAD_DIM
B_KVW = B_KV_GROUPS * HEAD_DIM
C_W = C_HEADS * HEAD_DIM
D_QW = D_HEADS * 2 * D_DK
D_VW = D_HEADS * D_DV

VMEM_LIMIT = 56 * 1024 * 1024
MM_SMALL_WEIGHT_BYTES = 8 * 1024 * 1024


def _cparams(sem):
    return pltpu.CompilerParams(dimension_semantics=sem, vmem_limit_bytes=VMEM_LIMIT)


def _dot(a, b):
    return jnp.dot(a, b, preferred_element_type=F32)


def _dot_nt(a, b):
    return lax.dot_general(a, b, (((1,), (1,)), ((), ())), preferred_element_type=F32)


def _dot_tn(a, b):
    return lax.dot_general(a, b, (((0,), (0,)), ((), ())), preferred_element_type=F32)


def _sigmoid(x):
    return 1.0 / (1.0 + jnp.exp(-x))


def _silu(x):
    return x * _sigmoid(x)


MM_TM = 1024
MM_TN = 512


def _mm_kernel(x_ref, w_ref, o_ref, wb_ref):
    @pl.when(pl.program_id(1) == 0)
    def _():
        wb_ref[...] = w_ref[0].astype(BF16)

    o_ref[...] = _dot(x_ref[...].astype(BF16), wb_ref[...]).astype(o_ref.dtype)


def _matmul(x, w, widx, col0, n, out_dtype):
    m, k = x.shape
    tn = min(MM_TN, n)
    j0 = col0 // tn
    return pl.pallas_call(
        _mm_kernel,
        grid=(n // tn, m // MM_TM),
        in_specs=[pl.BlockSpec((MM_TM, k), lambda j, i: (i, 0)),
                  pl.BlockSpec((1, k, tn), lambda j, i: (widx, 0, j0 + j))],
        out_specs=pl.BlockSpec((MM_TM, tn), lambda j, i: (i, j)),
        out_shape=jax.ShapeDtypeStruct((m, n), out_dtype),
        scratch_shapes=[pltpu.VMEM((k, tn), BF16)],
        compiler_params=_cparams(("arbitrary", "arbitrary")),
    )(x, w)


def _ffn_in_kernel(x_ref, wg_ref, wu_ref, o_ref, wgb_ref, wub_ref):
    @pl.when(pl.program_id(1) == 0)
    def _():
        wgb_ref[...] = wg_ref[0].astype(BF16)
        wub_ref[...] = wu_ref[0].astype(BF16)

    x = x_ref[...]
    g = _dot(x, wgb_ref[...])
    u = _dot(x, wub_ref[...])
    o_ref[...] = (_silu(g) * u).astype(o_ref.dtype)


def _ffn_in(x, w_in, layer):
    m, k = x.shape
    dff = w_in.shape[2] // 2
    nj = dff // MM_TN
    return pl.pallas_call(
        _ffn_in_kernel,
        grid=(nj, m // MM_TM),
        in_specs=[pl.BlockSpec((MM_TM, k), lambda j, i: (i, 0)),
                  pl.BlockSpec((1, k, MM_TN), lambda j, i: (layer, 0, j)),
                  pl.BlockSpec((1, k, MM_TN), lambda j, i: (layer, 0, j + nj))],
        out_specs=pl.BlockSpec((MM_TM, MM_TN), lambda j, i: (i, j)),
        out_shape=jax.ShapeDtypeStruct((m, dff), BF16),
        scratch_shapes=[pltpu.VMEM((k, MM_TN), BF16), pltpu.VMEM((k, MM_TN), BF16)],
        compiler_params=_cparams(("arbitrary", "arbitrary")),
    )(x, w_in, w_in)


def _layer_norm(y, w, b):
    mu = jnp.mean(y, axis=-1, keepdims=True)
    d = y - mu
    var = jnp.mean(d * d, axis=-1, keepdims=True)
    return d * lax.rsqrt(var + LN_EPS) * w + b


def _mm_res_ln_kernel(*refs, widths):
    part_refs = refs[:len(widths)]
    w_ref, h_ref, lnw_ref, lnb_ref, o_ref, obf_ref = refs[len(widths):]
    y = DEEPNORM_ALPHA * h_ref[...]
    k0 = 0
    for a_ref, width in zip(part_refs, widths):
        y = y + _dot(a_ref[...], w_ref[k0:k0 + width, :])
        k0 += width
    out = _layer_norm(y, lnw_ref[...], lnb_ref[...])
    o_ref[...] = out
    obf_ref[...] = out.astype(BF16)


def _mm_res_ln(parts, w, h, ln_w, ln_b):
    m = parts[0].shape[0]
    widths = tuple(a.shape[1] for a in parts)
    kdim, n = w.shape
    tm = 512 if kdim * n * 2 <= MM_SMALL_WEIGHT_BYTES else 256
    return pl.pallas_call(
        functools.partial(_mm_res_ln_kernel, widths=widths),
        grid=(m // tm,),
        in_specs=[pl.BlockSpec((tm, width), lambda i: (i, 0)) for width in widths]
        + [_resident((kdim, n), lambda i: (0, 0)),
           pl.BlockSpec((tm, n), lambda i: (i, 0)),
           pl.BlockSpec((1, n), lambda i: (0, 0)),
           pl.BlockSpec((1, n), lambda i: (0, 0))],
        out_specs=[pl.BlockSpec((tm, n), lambda i: (i, 0)),
                   pl.BlockSpec((tm, n), lambda i: (i, 0))],
        out_shape=[jax.ShapeDtypeStruct((m, n), F32), jax.ShapeDtypeStruct((m, n), BF16)],
        compiler_params=_cparams(("parallel",)),
    )(*parts, w, h, ln_w.reshape(1, n), ln_b.reshape(1, n))


LOG2E = math.log2(math.e)
LANES = 128
Q_SCALE = ATT_SCALE * LOG2E


def _flash_init(m_ref, l_ref, acc_ref):
    m_ref[...] = jnp.full(m_ref.shape, -jnp.inf, F32)
    if l_ref is not None:
        l_ref[...] = jnp.zeros(l_ref.shape, F32)
    acc_ref[...] = jnp.zeros(acc_ref.shape, F32)


def _flash_update(s_ref, v, m_ref, l_ref, acc_ref):
    m_prev = m_ref[...]
    m_new = jnp.maximum(m_prev, jnp.max(s_ref[...], axis=1, keepdims=True))
    m_ref[...] = m_new
    alpha = jnp.exp2(m_prev - m_new)
    p = jnp.exp2(s_ref[...] - _lane_tile(m_new, s_ref.shape[1]))
    if l_ref is not None:
        l_ref[...] = alpha * l_ref[...] + jnp.sum(p, axis=1, keepdims=True)
    acc_ref[...] = _lane_tile(alpha, acc_ref.shape[1]) * acc_ref[...] + _dot(p.astype(BF16), v)


def _mask_scores(s_ref, visible):
    s_ref[...] = jnp.where(visible, s_ref[...], NEG_INF)


def _lane_tile(x, width):
    reps = width // x.shape[1]
    return x if reps == 1 else jnp.concatenate([x] * reps, axis=1)


def _flash_sweep(n_tiles, qk, process):
    n_pairs = (n_tiles + 1) // 2
    qk(0, 0)

    def pair(jj, carry):
        j = 2 * jj
        qk(1, j + 1)
        process(0, j, False)
        qk(0, j + 2)
        process(1, j + 1, False)
        return carry

    def two_pairs(qq, carry):
        pair(2 * qq, carry)
        return pair(2 * qq + 1, carry)

    n_loop = n_pairs - 1
    lax.fori_loop(0, n_loop // 2, two_pairs, 0)
    lax.fori_loop(n_loop - n_loop % 2, n_loop, pair, 0)
    j_last = 2 * (n_pairs - 1)
    qk(1, j_last + 1)
    process(0, j_last, True)
    process(1, j_last + 1, True)


AUG_FILL_ROWS = 512


def _fill_augmented(k_ref, v_ref, ka_ref, va_ref, block_len):
    lane = lax.broadcasted_iota(jnp.int32, (AUG_FILL_ROWS, LANES), 1)
    ones_col = jnp.where(lane == 0, 1.0, 0.0).astype(BF16)

    def fill(c, carry):
        r0 = pl.multiple_of(c * AUG_FILL_ROWS, AUG_FILL_ROWS)
        rows = pl.ds(r0, AUG_FILL_ROWS)
        blk = (r0 + lax.broadcasted_iota(jnp.int32, (AUG_FILL_ROWS, LANES), 0)) // block_len
        ka_ref[rows, 0:HEAD_DIM] = k_ref[rows, :]
        ka_ref[rows, HEAD_DIM:] = jnp.where((blk & (LANES - 1)) == lane, 1.0, 0.0).astype(BF16)
        va_ref[rows, 0:HEAD_DIM] = v_ref[rows, :]
        va_ref[rows, HEAD_DIM:] = ones_col
        return carry

    lax.fori_loop(0, k_ref.shape[0] // AUG_FILL_ROWS, fill, 0)


def _resident(block_shape, index_map):
    return pl.BlockSpec(block_shape, index_map, pipeline_mode=pl.Buffered(1))


HG_TILE = 512
HG_HEADS = 4


def _hgrn_kernel(q_ref, f_ref, i_ref, g_ref, lb_ref, nw_ref, o_ref, st_ref):
    @pl.when(pl.program_id(1) == 0)
    def _():
        st_ref[...] = jnp.zeros_like(st_ref)

    lb = lb_ref[...]
    f = lb + (1.0 - lb) * _sigmoid(f_ref[...])
    logf = jnp.log(f)
    kk = 1.0 - f
    qf = _silu(q_ref[...])
    r64 = lax.broadcasted_iota(jnp.int32, logf.shape, 0) & (A_CHUNK - 1)
    b = logf
    step = 1
    while step < A_CHUNK:
        b = b + jnp.where(r64 >= step, pltpu.roll(b, step, 0), 0.0)
        step *= 2
    causal = (lax.broadcasted_iota(jnp.int32, (A_CHUNK, A_CHUNK), 0)
              >= lax.broadcasted_iota(jnp.int32, (A_CHUNK, A_CHUNK), 1))
    nw = nw_ref[...]
    for c in range(HG_TILE // A_CHUNK):
        sl = slice(c * A_CHUNK, (c + 1) * A_CHUNK)
        for hd in range(HG_HEADS):
            hs = slice(hd * HEAD_DIM, (hd + 1) * HEAD_DIM)
            bc = b[sl, hs]
            b_last = bc[A_CHUNK - 1:A_CHUNK, :]
            q_t = (qf[sl, hs] * jnp.exp(bc)).astype(BF16)
            k_t = (kk[sl, hs] * jnp.exp(-bc)).astype(BF16)
            vc = i_ref[sl, hs].astype(BF16)
            att = jnp.where(causal, _dot_nt(q_t, k_t), 0.0)
            st = st_ref[hd]
            o = _dot(att.astype(BF16), vc) + _dot_nt(q_t, st.astype(BF16))
            kdec = (kk[sl, hs] * jnp.exp(b_last - bc)).astype(BF16)
            st_ref[hd] = st * jnp.exp(b_last) + _dot_tn(vc, kdec)
            o = o * lax.rsqrt(jnp.mean(o * o, axis=-1, keepdims=True) + RMS_EPS) * nw
            o_ref[sl, hs] = (o * _silu(g_ref[sl, hs])).astype(o_ref.dtype)


def _hgrn2(z_a, lb, norm_w):
    t = z_a.shape[0]
    ngrp = A_HEADS // HG_HEADS
    gw = HG_HEADS * HEAD_DIM
    col = lambda base: (lambda h, i: (i, base + h))
    return pl.pallas_call(
        _hgrn_kernel,
        grid=(ngrp, t // HG_TILE),
        in_specs=[pl.BlockSpec((HG_TILE, gw), col(0)),
                  pl.BlockSpec((HG_TILE, gw), col(ngrp)),
                  pl.BlockSpec((HG_TILE, gw), col(2 * ngrp)),
                  pl.BlockSpec((HG_TILE, gw), col(3 * ngrp)),
                  pl.BlockSpec((1, gw), lambda h, i: (0, h)),
                  pl.BlockSpec((1, HEAD_DIM), lambda h, i: (0, 0))],
        out_specs=pl.BlockSpec((HG_TILE, gw), lambda h, i: (i, h)),
        out_shape=jax.ShapeDtypeStruct((t, A_W), BF16),
        scratch_shapes=[pltpu.VMEM((HG_HEADS, HEAD_DIM, HEAD_DIM), F32)],
        compiler_params=_cparams(("parallel", "arbitrary")),
    )(z_a, z_a, z_a, z_a, lb.reshape(1, A_W), norm_w.reshape(1, HEAD_DIM))


SEG_W = CMP_STRIDE * HEAD_DIM


def _nsa_compress_kernel(seg_ref, pos_ref, w1_ref, w2_ref, o_ref):
    seg = seg_ref[0, 0]
    nseg = seg.shape[0]
    a = _dot(seg, w1_ref[0, :SEG_W, :])
    b = _dot(seg, w1_ref[0, SEG_W:, :])
    b_next = pltpu.roll(b, nseg - 1, 0)
    pos = jnp.broadcast_to(pos_ref[0], (8, CMP_LEN * HEAD_DIM))
    c = _dot(pos, w1_ref[0])[0:1, :]
    hid = _silu(a + b_next + c)
    o_ref[0, 0] = _dot(hid.astype(BF16), w2_ref[0]).astype(o_ref.dtype)


def _nsa_compress(seg, pos, w1, w2):
    _, ng, nseg, _ = seg.shape
    return pl.pallas_call(
        _nsa_compress_kernel,
        grid=(2, ng),
        in_specs=[pl.BlockSpec((1, 1, nseg, SEG_W), lambda a, g: (a, g, 0, 0)),
                  pl.BlockSpec((1, 1, CMP_LEN * HEAD_DIM), lambda a, g: (a, 0, 0)),
                  pl.BlockSpec((1, CMP_LEN * HEAD_DIM, HEAD_DIM), lambda a, g: (a, 0, 0)),
                  pl.BlockSpec((1, HEAD_DIM, HEAD_DIM), lambda a, g: (a, 0, 0))],
        out_specs=pl.BlockSpec((1, 1, nseg, HEAD_DIM), lambda a, g: (a, g, 0, 0)),
        out_shape=jax.ShapeDtypeStruct((2, ng, nseg, HEAD_DIM), BF16),
        compiler_params=_cparams(("parallel", "parallel")),
    )(seg, pos, w1, w2)


CS_TQ = 256
CS_WIDTH_STEP = 256
SL_NBP = 128


def _split3(x):
    hi = x.astype(BF16)
    r = x - hi.astype(F32)
    mid = r.astype(BF16)
    lo = (r - mid.astype(F32)).astype(BF16)
    return hi, mid, lo


def _topk_mask(score, lane, k):
    sel = jnp.zeros(score.shape, F32)
    width = float(score.shape[1])
    for _ in range(k):
        mx = jnp.max(score, axis=1, keepdims=True)
        idx = jnp.min(jnp.where(score == mx, lane, width), axis=1, keepdims=True)
        hit = lane == idx
        sel = jnp.where(hit, 1.0, sel)
        score = jnp.where(hit, -jnp.inf, score)
    return sel


def _nsa_cmp_kernel(q_ref, kc_ref, vc_ref, map_ref, o_ref, sel_ref, *, n_sel):
    i = pl.program_id(1)
    tq = q_ref.shape[0]
    ncmp = kc_ref.shape[2]
    nbp = map_ref.shape[1]
    t_pos = i * tq + lax.broadcasted_iota(jnp.int32, (tq, 1), 0)
    row_has_keys = t_pos >= CMP_LEN - 1
    jb = lax.broadcasted_iota(jnp.int32, (1, nbp), 1)
    cur = t_pos // SLC_LEN
    valid = jb <= cur
    forced = (jb == 0) | (jb == cur) | (jb == cur - 1)

    def body(width):
        cmp_end = lax.broadcasted_iota(jnp.int32, (1, width), 1) * CMP_STRIDE + (CMP_LEN - 1)
        mask = cmp_end <= t_pos
        kc = kc_ref[0, 0, 0:width, :]
        vc = vc_ref[0, 0, 0:width, :]
        psum = jnp.zeros((tq, width), F32)
        for hh in range(B_HPG):
            hs = slice(hh * HEAD_DIM, (hh + 1) * HEAD_DIM)
            s = jnp.where(mask, _dot_nt(q_ref[:, hs], kc) * Q_SCALE, NEG_INF)
            e = jnp.exp2(s - jnp.max(s, axis=1, keepdims=True))
            p = e * jnp.where(row_has_keys, 1.0 / jnp.sum(e, axis=1, keepdims=True), 0.0)
            o_ref[:, hs] = _dot(p.astype(BF16), vc).astype(o_ref.dtype)
            psum = psum + p
        hi, mid, lo = _split3(psum)
        cmap = map_ref[0:width, :]
        p_slc = _dot(hi, cmap) + _dot(mid, cmap) + _dot(lo, cmap)
        score = jnp.where(valid & jnp.logical_not(forced), p_slc, NEG_INF)
        sel = _topk_mask(score, jb.astype(F32), n_sel - N_FORCED)
        sel_ref[0] = jnp.where(valid, jnp.where(forced, 1.0, sel), 0.0).astype(sel_ref.dtype)

    widths = list(range(CS_WIDTH_STEP, ncmp + 1, CS_WIDTH_STEP)) if ncmp % CS_WIDTH_STEP == 0 else [ncmp]
    if len(widths) == 1:
        body(widths[0])
    else:
        n_vis = ((i + 1) * tq - CMP_LEN) // CMP_STRIDE + 1
        case = jnp.clip((n_vis + CS_WIDTH_STEP - 1) // CS_WIDTH_STEP - 1, 0, len(widths) - 1)
        for idx, width in enumerate(widths):
            pl.when(case == idx)(functools.partial(body, width))


def _cmp_to_slc_matrix(ncmp_pad, n_cmp, nb, nbp):
    ratio = SLC_LEN // CMP_STRIDE
    n_over = CMP_LEN // CMP_STRIDE
    mat = np.zeros((ncmp_pad, nbp), np.float32)
    for j in range(nb):
        for m in range(ratio):
            for n in range(n_over):
                c = ratio * j + m - n
                if 0 <= c < n_cmp:
                    mat[c, j] += 1.0
    return mat


def _nsa_cmp_select(z_b, cmp_kv):
    t = z_b.shape[0]
    ng = B_KV_GROUPS
    nseg = cmp_kv.shape[2]
    nb = t // SLC_LEN
    n_cmp = (t - CMP_LEN) // CMP_STRIDE + 1
    nbp = -(-nb // SL_NBP) * SL_NBP
    cmap = jnp.asarray(_cmp_to_slc_matrix(nseg, n_cmp, nb, nbp), BF16)
    gw = B_HPG * HEAD_DIM
    return pl.pallas_call(
        functools.partial(_nsa_cmp_kernel, n_sel=min(N_SLC, nb)),
        grid=(ng, t // CS_TQ),
        in_specs=[pl.BlockSpec((CS_TQ, gw), lambda g, i: (i, g)),
                  pl.BlockSpec((1, 1, nseg, HEAD_DIM), lambda g, i: (0, g, 0, 0)),
                  pl.BlockSpec((1, 1, nseg, HEAD_DIM), lambda g, i: (1, g, 0, 0)),
                  pl.BlockSpec((nseg, nbp), lambda g, i: (0, 0))],
        out_specs=[pl.BlockSpec((CS_TQ, gw), lambda g, i: (i, g)),
                   pl.BlockSpec((1, CS_TQ, nbp), lambda g, i: (g, i, 0))],
        out_shape=[jax.ShapeDtypeStruct((t, B_QW), BF16),
                   jax.ShapeDtypeStruct((ng, t, nbp), BF16)],
        compiler_params=_cparams(("parallel", "parallel")),
    )(z_b, cmp_kv, cmp_kv, cmap)


SL_TQ = 512
SL_TK = SL_TQ
SL_PHASE_TILES = SL_NBP * SLC_LEN // SL_TK


def _nsa_slc_kernel(q_ref, sel_ref, k_ref, v_ref, o_ref, ka_ref, va_ref, qa_ref, s_ref, m_ref, acc_ref, *, n_phase):
    i = pl.program_id(1)
    tq = SL_TQ

    @pl.when(i == 0)
    def _():
        _fill_augmented(k_ref, v_ref, ka_ref, va_ref, SLC_LEN)

    bias = ((sel_ref[0].astype(F32) - 1.0) * (-NEG_INF)).astype(BF16)
    for hh in range(B_HPG):
        rows = slice(hh * tq, (hh + 1) * tq)
        q = (q_ref[:, hh * HEAD_DIM:(hh + 1) * HEAD_DIM].astype(F32) * Q_SCALE).astype(BF16)
        for ph in range(n_phase):
            qa_ref[ph, rows, 0:HEAD_DIM] = q
            qa_ref[ph, rows, HEAD_DIM:] = bias[:, ph * SL_NBP:(ph + 1) * SL_NBP]
    _flash_init(m_ref, None, acc_ref)
    t_pos = i * tq + (lax.broadcasted_iota(jnp.int32, (B_HPG * tq, 1), 0) & (tq - 1))

    def qk(slot, j):
        off = pl.multiple_of(j * SL_TK, SL_TK)
        qa = qa_ref[0] if n_phase == 1 else qa_ref[j // SL_PHASE_TILES]
        s_ref[slot] = _dot_nt(qa, ka_ref[pl.ds(off, SL_TK), :])

    def process(slot, j, causal):
        off = pl.multiple_of(j * SL_TK, SL_TK)
        if causal:
            kpos = off + lax.broadcasted_iota(jnp.int32, (1, SL_TK), 1)
            _mask_scores(s_ref.at[slot], kpos <= t_pos)
        _flash_update(s_ref.at[slot], va_ref[pl.ds(off, SL_TK), :], m_ref, None, acc_ref)

    _flash_sweep((i * tq + tq + SL_TK - 1) // SL_TK, qk, process)
    acc = acc_ref[...]
    o = acc[:, 0:HEAD_DIM] / acc[:, HEAD_DIM:HEAD_DIM + 1]
    for hh in range(B_HPG):
        o_ref[:, hh * HEAD_DIM:(hh + 1) * HEAD_DIM] = o[hh * tq:(hh + 1) * tq].astype(o_ref.dtype)


def _nsa_slc(z_b, sel):
    t = z_b.shape[0]
    nbp = sel.shape[2]
    n_phase = nbp // SL_NBP
    gw = B_HPG * HEAD_DIM
    rows = B_HPG * SL_TQ
    k_col0 = (B_QW + 2 * B_KVW) // HEAD_DIM
    v_col0 = (B_QW + 3 * B_KVW) // HEAD_DIM
    return pl.pallas_call(
        functools.partial(_nsa_slc_kernel, n_phase=n_phase),
        grid=(B_KV_GROUPS, t // SL_TQ),
        in_specs=[pl.BlockSpec((SL_TQ, gw), lambda g, i: (i, g)),
                  pl.BlockSpec((1, SL_TQ, nbp), lambda g, i: (g, i, 0)),
                  _resident((t, HEAD_DIM), lambda g, i: (0, k_col0 + g)),
                  _resident((t, HEAD_DIM), lambda g, i: (0, v_col0 + g))],
        out_specs=pl.BlockSpec((SL_TQ, gw), lambda g, i: (i, g)),
        out_shape=jax.ShapeDtypeStruct((t, B_QW), BF16),
        scratch_shapes=[pltpu.VMEM((t, HEAD_DIM + SL_NBP), BF16),
                        pltpu.VMEM((t, 2 * HEAD_DIM), BF16),
                        pltpu.VMEM((n_phase, rows, HEAD_DIM + SL_NBP), BF16),
                        pltpu.VMEM((2, rows, SL_TK), F32),
                        pltpu.VMEM((rows, LANES), F32),
                        pltpu.VMEM((rows, 2 * HEAD_DIM), F32)],
        compiler_params=_cparams(("arbitrary", "arbitrary")),
    )(z_b, sel, z_b, z_b)


WN_TQ = WIN


def _nsa_win_kernel(q_ref, ka_ref, kb_ref, va_ref, vb_ref, oc_ref, os_ref, gate_ref, o_ref, qs_ref):
    i = pl.program_id(1)
    tq = WN_TQ
    for hh in range(B_HPG):
        q = q_ref[:, hh * HEAD_DIM:(hh + 1) * HEAD_DIM].astype(F32) * Q_SCALE
        qs_ref[hh * tq:(hh + 1) * tq, :] = q.astype(BF16)
    qs = qs_ref[...]
    t_loc = lax.broadcasted_iota(jnp.int32, (B_HPG * tq, 1), 0) & (tq - 1)
    c_loc = lax.broadcasted_iota(jnp.int32, (1, tq), 1)
    t_prev = t_loc + jnp.where(i > 0, 0, tq)
    s_a = jnp.where(c_loc > t_prev, _dot_nt(qs, ka_ref[...]), NEG_INF)
    s_b = jnp.where(c_loc <= t_loc, _dot_nt(qs, kb_ref[...]), NEG_INF)
    m = jnp.maximum(jnp.max(s_a, axis=1, keepdims=True), jnp.max(s_b, axis=1, keepdims=True))
    p_a = jnp.exp2(s_a - m)
    p_b = jnp.exp2(s_b - m)
    l = jnp.sum(p_a, axis=1, keepdims=True) + jnp.sum(p_b, axis=1, keepdims=True)
    o_w = (_dot(p_a.astype(BF16), va_ref[...]) + _dot(p_b.astype(BF16), vb_ref[...])) / l
    gates = _sigmoid(gate_ref[...])
    for hh in range(B_HPG):
        hs = slice(hh * HEAD_DIM, (hh + 1) * HEAD_DIM)
        g_c, g_s, g_w = (gates[:, 3 * hh + c:3 * hh + c + 1] for c in range(3))
        o = (g_c * oc_ref[:, hs].astype(F32) + g_s * os_ref[:, hs].astype(F32)
             + g_w * o_w[hh * tq:(hh + 1) * tq])
        o_ref[:, hs] = o.astype(o_ref.dtype)


def _nsa_win_combine(z_b, o_c, o_s, z_gate):
    t = z_b.shape[0]
    gw = B_HPG * HEAD_DIM
    k_col0 = (B_QW + 4 * B_KVW) // HEAD_DIM
    v_col0 = (B_QW + 5 * B_KVW) // HEAD_DIM
    prev_tile = lambda col0: (lambda g, i: (jnp.maximum(i - 1, 0), col0 + g))
    this_tile = lambda col0: (lambda g, i: (i, col0 + g))
    kv_block = (WN_TQ, HEAD_DIM)
    return pl.pallas_call(
        _nsa_win_kernel,
        grid=(B_KV_GROUPS, t // WN_TQ),
        in_specs=[pl.BlockSpec((WN_TQ, gw), lambda g, i: (i, g)),
                  pl.BlockSpec(kv_block, prev_tile(k_col0)),
                  pl.BlockSpec(kv_block, this_tile(k_col0)),
                  pl.BlockSpec(kv_block, prev_tile(v_col0)),
                  pl.BlockSpec(kv_block, this_tile(v_col0)),
                  pl.BlockSpec((WN_TQ, gw), lambda g, i: (i, g)),
                  pl.BlockSpec((WN_TQ, gw), lambda g, i: (i, g)),
                  pl.BlockSpec((WN_TQ, HEAD_DIM), lambda g, i: (i, g))],
        out_specs=pl.BlockSpec((WN_TQ, gw), lambda g, i: (i, g)),
        out_shape=jax.ShapeDtypeStruct((t, B_QW), BF16),
        scratch_shapes=[pltpu.VMEM((B_HPG * WN_TQ, HEAD_DIM), BF16)],
        compiler_params=_cparams(("parallel", "parallel")),
    )(z_b, z_b, z_b, z_b, z_b, o_c, o_s, z_gate)


def _nsa(z_b, z_gate, pos_k, w1k, w2k, pos_v, w1v, w2v):
    t = z_b.shape[0]
    ng = B_KV_GROUPS

    def segs(col0):
        z = z_b[:, col0:col0 + B_KVW].reshape(t, ng, HEAD_DIM).transpose(1, 0, 2)
        return z.reshape(ng, t // CMP_STRIDE, SEG_W)

    seg = jnp.stack([segs(B_QW), segs(B_QW + B_KVW)])
    pos = jnp.stack([pos_k, pos_v]).reshape(2, 1, CMP_LEN * HEAD_DIM).astype(BF16)
    w1 = jnp.stack([w1k, w1v]).astype(BF16)
    w2 = jnp.stack([w2k, w2v]).astype(BF16)
    cmp_kv = _nsa_compress(seg, pos, w1, w2)
    o_c, sel = _nsa_cmp_select(z_b, cmp_kv)
    o_s = _nsa_slc(z_b, sel)
    return _nsa_win_combine(z_b, o_c, o_s, z_gate)


KM_ROWS = 8


def _kmean_kernel(k_ref, o_ref):
    x = k_ref[...].astype(F32)
    o_ref[...] = jnp.mean(x.reshape(KM_ROWS, MOBA_BLOCK, x.shape[1]), axis=1)


def _moba_kmean(z_cd):
    t = z_cd.shape[0]
    nbm = t // MOBA_BLOCK
    return pl.pallas_call(
        _kmean_kernel,
        grid=(nbm // KM_ROWS,),
        in_specs=[pl.BlockSpec((KM_ROWS * MOBA_BLOCK, C_W), lambda i: (i, 1))],
        out_specs=pl.BlockSpec((KM_ROWS, C_W), lambda i: (i, 0)),
        out_shape=jax.ShapeDtypeStruct((nbm, C_W), F32),
        compiler_params=_cparams(("parallel",)),
    )(z_cd)


MB_TQ = 1024
MB_TK = MB_TQ // 2
MB_NBP = 128


def _moba_kernel(q_ref, km_ref, k_ref, v_ref, o_ref, ka_ref, va_ref, qa_ref, s_ref, m_ref, acc_ref):
    i = pl.program_id(1)
    tq = MB_TQ

    @pl.when(i == 0)
    def _():
        _fill_augmented(k_ref, v_ref, ka_ref, va_ref, MOBA_BLOCK)

    q = q_ref[...]
    km = km_ref[...]
    km_hi = km.astype(BF16)
    km_lo = (km - km_hi.astype(F32)).astype(BF16)
    gate = _dot_nt(q, km_hi) + _dot_nt(q, km_lo)
    jb = lax.broadcasted_iota(jnp.int32, (1, MB_NBP), 1)
    t_pos = i * tq + lax.broadcasted_iota(jnp.int32, (tq, 1), 0)
    cur = t_pos // MOBA_BLOCK
    earlier = jb < cur
    sel = _topk_mask(jnp.where(earlier, gate, NEG_INF), jb.astype(F32), MOBA_TOPK)
    sel = jnp.where(jb == cur, 1.0, jnp.where(earlier, sel, 0.0))
    qa_ref[:, 0:HEAD_DIM] = (q.astype(F32) * Q_SCALE).astype(BF16)
    qa_ref[:, HEAD_DIM:] = ((sel - 1.0) * (-NEG_INF)).astype(BF16)
    _flash_init(m_ref, None, acc_ref)

    def qk(slot, j):
        off = pl.multiple_of(j * MB_TK, MB_TK)
        s_ref[slot] = _dot_nt(qa_ref[...], ka_ref[pl.ds(off, MB_TK), :])

    def process(slot, j, causal):
        off = pl.multiple_of(j * MB_TK, MB_TK)
        if causal:
            kpos = off + lax.broadcasted_iota(jnp.int32, (1, MB_TK), 1)
            _mask_scores(s_ref.at[slot], kpos <= t_pos)
        _flash_update(s_ref.at[slot], va_ref[pl.ds(off, MB_TK), :], m_ref, None, acc_ref)

    _flash_sweep((i * tq + tq + MB_TK - 1) // MB_TK, qk, process)
    acc = acc_ref[...]
    o_ref[...] = (acc[:, 0:HEAD_DIM] / acc[:, HEAD_DIM:HEAD_DIM + 1]).astype(o_ref.dtype)


def _moba(z_cd):
    t = z_cd.shape[0]
    nbm = t // MOBA_BLOCK
    nh = C_HEADS
    k_mean = _moba_kmean(z_cd)
    k_mean = jnp.pad(k_mean, ((0, MB_NBP - nbm), (0, 0)))
    return pl.pallas_call(
        _moba_kernel,
        grid=(nh, t // MB_TQ),
        in_specs=[pl.BlockSpec((MB_TQ, HEAD_DIM), lambda h, i: (i, h)),
                  pl.BlockSpec((MB_NBP, HEAD_DIM), lambda h, i: (0, h)),
                  _resident((t, HEAD_DIM), lambda h, i: (0, nh + h)),
                  _resident((t, HEAD_DIM), lambda h, i: (0, 2 * nh + h))],
        out_specs=pl.BlockSpec((MB_TQ, HEAD_DIM), lambda h, i: (i, h)),
        out_shape=jax.ShapeDtypeStruct((t, C_W), BF16),
        scratch_shapes=[pltpu.VMEM((t, HEAD_DIM + MB_NBP), BF16),
                        pltpu.VMEM((t, 2 * HEAD_DIM), BF16),
                        pltpu.VMEM((MB_TQ, HEAD_DIM + MB_NBP), BF16),
                        pltpu.VMEM((2, MB_TQ, MB_TK), F32),
                        pltpu.VMEM((MB_TQ, LANES), F32),
                        pltpu.VMEM((MB_TQ, 2 * HEAD_DIM), F32)],
        compiler_params=_cparams(("arbitrary", "arbitrary")),
    )(z_cd, k_mean, z_cd, z_cd)


DF_TQ = 1024
DF_TK = DF_TQ // 2


def _diff_kernel(q_ref, k_ref, v_ref, lq1_ref, lk1_ref, lq2_ref, lk2_ref, nw_ref, o_ref,
                 qs_ref, s_ref, m_ref, l_ref, acc_ref, *, lam_init):
    i = pl.program_id(1)
    tq = DF_TQ
    for mp in range(2):
        q = q_ref[:, mp * D_DK:(mp + 1) * D_DK].astype(F32) * (D_DK ** -0.5 * LOG2E)
        qs_ref[mp] = q.astype(BF16)
    _flash_init(m_ref, l_ref, acc_ref)
    t_pos = i * tq + (lax.broadcasted_iota(jnp.int32, (2 * tq, 1), 0) & (tq - 1))

    def qk(slot, j):
        off = pl.multiple_of(j * DF_TK, DF_TK)
        for mp in range(2):
            s_ref[slot, mp * tq:(mp + 1) * tq, :] = _dot_nt(
                qs_ref[mp], k_ref[pl.ds(off, DF_TK), mp * D_DK:(mp + 1) * D_DK])

    def process(slot, j, causal):
        off = pl.multiple_of(j * DF_TK, DF_TK)
        if causal:
            kpos = off + lax.broadcasted_iota(jnp.int32, (1, DF_TK), 1)
            _mask_scores(s_ref.at[slot], kpos <= t_pos)
        _flash_update(s_ref.at[slot], v_ref[pl.ds(off, DF_TK), :], m_ref, l_ref, acc_ref)

    _flash_sweep((i * tq + tq + DF_TK - 1) // DF_TK, qk, process)
    lam = (jnp.exp(jnp.sum(lq1_ref[...] * lk1_ref[...], axis=1, keepdims=True))
           - jnp.exp(jnp.sum(lq2_ref[...] * lk2_ref[...], axis=1, keepdims=True)) + lam_init)
    o = (acc_ref[0:tq, :] / _lane_tile(l_ref[0:tq, :], D_DV)
         - lam * (acc_ref[tq:, :] / _lane_tile(l_ref[tq:, :], D_DV)))
    o = o * lax.rsqrt(jnp.mean(o * o, axis=-1, keepdims=True) + RMS_EPS) * nw_ref[...]
    o_ref[...] = (o * (1.0 - lam_init)).astype(o_ref.dtype)


def _diff_attn(z_cd, lq1, lk1, lq2, lk2, subln_w, layer_idx):
    t = z_cd.shape[0]
    lam_init = 0.8 - 0.6 * math.exp(-0.3 * layer_idx)
    q_col0 = 3 * C_W // D_DV
    k_col0 = (3 * C_W + D_QW) // D_DV
    v_col0 = (3 * C_W + 2 * D_QW) // D_DV
    vec = pl.BlockSpec((1, D_DK), lambda h, i: (0, 0))
    return pl.pallas_call(
        functools.partial(_diff_kernel, lam_init=lam_init),
        grid=(D_HEADS, t // DF_TQ),
        in_specs=[pl.BlockSpec((DF_TQ, 2 * D_DK), lambda h, i: (i, q_col0 + h)),
                  _resident((t, 2 * D_DK), lambda h, i: (0, k_col0 + h)),
                  _resident((t, D_DV), lambda h, i: (0, v_col0 + h)),
                  vec, vec, vec, vec,
                  pl.BlockSpec((1, D_DV), lambda h, i: (0, 0))],
        out_specs=pl.BlockSpec((DF_TQ, D_DV), lambda h, i: (i, h)),
        out_shape=jax.ShapeDtypeStruct((t, D_VW), BF16),
        scratch_shapes=[pltpu.VMEM((2, DF_TQ, D_DK), BF16),
                        pltpu.VMEM((2, 2 * DF_TQ, DF_TK), F32),
                        pltpu.VMEM((2 * DF_TQ, LANES), F32),
                        pltpu.VMEM((2 * DF_TQ, LANES), F32),
                        pltpu.VMEM((2 * DF_TQ, D_DV), F32)],
        compiler_params=_cparams(("parallel", "arbitrary")),
    )(z_cd, z_cd, z_cd, lq1.reshape(1, D_DK), lk1.reshape(1, D_DK), lq2.reshape(1, D_DK),
      lk2.reshape(1, D_DK), subln_w.reshape(1, D_DV))


def _gate_weight(w_gate):
    d = w_gate.shape[0]
    per = 3 * B_HPG
    wg = w_gate.reshape(d, B_KV_GROUPS, per)
    wg = jnp.pad(wg, ((0, 0), (0, 0), (0, HEAD_DIM - per)))
    return wg.reshape(d, B_KV_GROUPS * HEAD_DIM)


def kernel(x, ln_w, ln_b, ffn_w_in, ffn_w_out, ab_w_in, ab_w_out, hgrn_lower_bounds, hgrn_norm_w,
           nsa_cmp_pos_k, nsa_cmp_k_w1, nsa_cmp_k_w2, nsa_cmp_pos_v, nsa_cmp_v_w1, nsa_cmp_v_w2,
           cd_w_in, cd_w_out, diff_lambda_q1, diff_lambda_k1, diff_lambda_q2, diff_lambda_k2, diff_subln_w):
    bsz, t, d = x.shape
    lb_all = jnp.cumsum(jax.nn.softmax(hgrn_lower_bounds.astype(F32), axis=0), axis=0)
    outs = []
    for bi in range(bsz):
        h = x[bi]
        h_in = h.astype(BF16)
        for layer in range(DEPTH):
            if layer % 2 == 0:
                e = layer // 2
                n_a = 4 * A_W
                n_b = B_QW + 6 * B_KVW
                z_a = _matmul(h_in, ab_w_in, e, 0, n_a, F32)
                z_b = _matmul(h_in, ab_w_in, e, n_a, n_b, BF16)
                w_gate = _gate_weight(ab_w_in[e, :, n_a + n_b:])
                z_g = _matmul(h_in, w_gate[None], 0, 0, w_gate.shape[1], F32)
                o_a = _hgrn2(z_a, lb_all[layer], hgrn_norm_w[e])
                o_b = _nsa(z_b, z_g, nsa_cmp_pos_k[e], nsa_cmp_k_w1[e], nsa_cmp_k_w2[e],
                           nsa_cmp_pos_v[e], nsa_cmp_v_w1[e], nsa_cmp_v_w2[e])
                mix_in = (o_a, o_b)
                w_out = ab_w_out[e]
            else:
                oi = layer // 2
                z_cd = _matmul(h_in, cd_w_in, oi, 0, cd_w_in.shape[2], BF16)
                o_c = _moba(z_cd)
                o_d = _diff_attn(z_cd, diff_lambda_q1[oi], diff_lambda_k1[oi], diff_lambda_q2[oi],
                                 diff_lambda_k2[oi], diff_subln_w[oi], layer)
                mix_in = (o_c, o_d)
                w_out = cd_w_out[oi]
            h, h_in = _mm_res_ln(mix_in, w_out.astype(BF16), h, ln_w[layer, 0], ln_b[layer, 0])
            act = _ffn_in(h_in, ffn_w_in, layer)
            h, h_in = _mm_res_ln((act,), ffn_w_out[layer].astype(BF16), h, ln_w[layer, 1], ln_b[layer, 1])
        outs.append(h.reshape(1, t, d))
    return outs[0] if bsz == 1 else jnp.concatenate(outs, axis=0)
```

```python
import functools
import math

import numpy as np
import jax
import jax.numpy as jnp
from jax import lax
from jax.experimental import pallas as pl
from jax.experimental.pallas import tpu as pltpu

F32 = jnp.float32
BF16 = jnp.bfloat16

DEPTH = 2
HEAD_DIM = 128
A_HEADS = 8
A_CHUNK = 64
B_HEADS = 8
B_KV_GROUPS = 2
B_HPG = B_HEADS // B_KV_GROUPS
CMP_LEN = 32
CMP_STRIDE = 16
SLC_LEN = 64
N_SLC = 16
N_FORCED = 3
WIN = 512
C_HEADS = 8
MOBA_BLOCK = 256
MOBA_TOPK = 3
D_HEADS = 4
D_DK = 128
D_DV = 2 * D_DK

DEEPNORM_ALPHA = (2 * DEPTH) ** 0.25
NEG_INF = -1e30
LN_EPS = 1e-5
RMS_EPS = 1e-6
ATT_SCALE = HEAD_DIM ** -0.5

A_W = A_HEADS * HEAD_DIM
B_QW = B_HEADS * HEAD_DIM
B_KVW = B_KV_GROUPS * HEAD_DIM
C_W = C_HEADS * HEAD_DIM
D_QW = D_HEADS * 2 * D_DK
D_VW = D_HEADS * D_DV

VMEM_LIMIT = 56 * 1024 * 1024
MM_SMALL_WEIGHT_BYTES = 8 * 1024 * 1024


def _cparams(sem):
    return pltpu.CompilerParams(dimension_semantics=sem, vmem_limit_bytes=VMEM_LIMIT)


def _dot(a, b):
    return jnp.dot(a, b, preferred_element_type=F32)


def _dot_nt(a, b):
    return lax.dot_general(a, b, (((1,), (1,)), ((), ())), preferred_element_type=F32)


def _dot_tn(a, b):
    return lax.dot_general(a, b, (((0,), (0,)), ((), ())), preferred_element_type=F32)


def _sigmoid(x):
    return 1.0 / (1.0 + jnp.exp(-x))


def _silu(x):
    return x * _sigmoid(x)


MM_TM = 1024
MM_TN = 512


def _mm_kernel(x_ref, w_ref, o_ref, wb_ref):
    @pl.when(pl.program_id(1) == 0)
    def _():
        wb_ref[...] = w_ref[0].astype(BF16)

    o_ref[...] = _dot(x_ref[...], wb_ref[...]).astype(o_ref.dtype)


def _matmul(x, w, widx, col0, n, out_dtype):
    m, k = x.shape
    tn = min(MM_TN, n)
    j0 = col0 // tn
    return pl.pallas_call(
        _mm_kernel,
        grid=(n // tn, m // MM_TM),
        in_specs=[pl.BlockSpec((MM_TM, k), lambda j, i: (i, 0)),
                  pl.BlockSpec((1, k, tn), lambda j, i: (widx, 0, j0 + j))],
        out_specs=pl.BlockSpec((MM_TM, tn), lambda j, i: (i, j)),
        out_shape=jax.ShapeDtypeStruct((m, n), out_dtype),
        scratch_shapes=[pltpu.VMEM((k, tn), BF16)],
        compiler_params=_cparams(("arbitrary", "arbitrary")),
    )(x, w)


def _ffn_in_kernel(x_ref, wg_ref, wu_ref, o_ref, wgb_ref, wub_ref):
    @pl.when(pl.program_id(1) == 0)
    def _():
        wgb_ref[...] = wg_ref[0].astype(BF16)
        wub_ref[...] = wu_ref[0].astype(BF16)

    x = x_ref[...]
    g = _dot(x, wgb_ref[...])
    u = _dot(x, wub_ref[...])
    o_ref[...] = (_silu(g) * u).astype(o_ref.dtype)


def _ffn_in(x, w_in, layer):
    m, k = x.shape
    dff = w_in.shape[2] // 2
    nj = dff // MM_TN
    return pl.pallas_call(
        _ffn_in_kernel,
        grid=(nj, m // MM_TM),
        in_specs=[pl.BlockSpec((MM_TM, k), lambda j, i: (i, 0)),
                  pl.BlockSpec((1, k, MM_TN), lambda j, i: (layer, 0, j)),
                  pl.BlockSpec((1, k, MM_TN), lambda j, i: (layer, 0, j + nj))],
        out_specs=pl.BlockSpec((MM_TM, MM_TN), lambda j, i: (i, j)),
        out_shape=jax.ShapeDtypeStruct((m, dff), BF16),
        scratch_shapes=[pltpu.VMEM((k, MM_TN), BF16), pltpu.VMEM((k, MM_TN), BF16)],
        compiler_params=_cparams(("arbitrary", "arbitrary")),
    )(x, w_in, w_in)


def _layer_norm(y, w, b):
    mu = jnp.mean(y, axis=-1, keepdims=True)
    d = y - mu
    var = jnp.mean(d * d, axis=-1, keepdims=True)
    return d * lax.rsqrt(var + LN_EPS) * w + b


def _mm_res_ln_kernel(*refs, widths):
    part_refs = refs[:len(widths)]
    w_ref, h_ref, lnw_ref, lnb_ref, o_ref, obf_ref = refs[len(widths):]
    y = DEEPNORM_ALPHA * h_ref[...]
    k0 = 0
    for a_ref, width in zip(part_refs, widths):
        y = y + _dot(a_ref[...], w_ref[k0:k0 + width, :])
        k0 += width
    out = _layer_norm(y, lnw_ref[...], lnb_ref[...])
    o_ref[...] = out
    obf_ref[...] = out.astype(BF16)


def _mm_res_ln(parts, w, h, ln_w, ln_b):
    m = parts[0].shape[0]
    widths = tuple(a.shape[1] for a in parts)
    kdim, n = w.shape
    tm = 512 if kdim * n * 2 <= MM_SMALL_WEIGHT_BYTES else 256
    return pl.pallas_call(
        functools.partial(_mm_res_ln_kernel, widths=widths),
        grid=(m // tm,),
        in_specs=[pl.BlockSpec((tm, width), lambda i: (i, 0)) for width in widths]
        + [_resident((kdim, n), lambda i: (0, 0)),
           pl.BlockSpec((tm, n), lambda i: (i, 0)),
           pl.BlockSpec((1, n), lambda i: (0, 0)),
           pl.BlockSpec((1, n), lambda i: (0, 0))],
        out_specs=[pl.BlockSpec((tm, n), lambda i: (i, 0)),
                   pl.BlockSpec((tm, n), lambda i: (i, 0))],
        out_shape=[jax.ShapeDtypeStruct((m, n), F32), jax.ShapeDtypeStruct((m, n), BF16)],
        compiler_params=_cparams(("parallel",)),
    )(*parts, w, h, ln_w.reshape(1, n), ln_b.reshape(1, n))


LOG2E = math.log2(math.e)
LANES = 128
Q_SCALE = ATT_SCALE * LOG2E


def _flash_init(m_ref, l_ref, acc_ref):
    m_ref[...] = jnp.full(m_ref.shape, -jnp.inf, F32)
    if l_ref is not None:
        l_ref[...] = jnp.zeros(l_ref.shape, F32)
    acc_ref[...] = jnp.zeros(acc_ref.shape, F32)


def _flash_update(s_ref, v, m_ref, l_ref, acc_ref):
    m_prev = m_ref[...]
    m_new = jnp.maximum(m_prev, jnp.max(s_ref[...], axis=1, keepdims=True))
    m_ref[...] = m_new
    alpha = jnp.exp2(m_prev - m_new)
    p = jnp.exp2(s_ref[...] - _lane_tile(m_new, s_ref.shape[1]))
    if l_ref is not None:
        l_ref[...] = alpha * l_ref[...] + jnp.sum(p, axis=1, keepdims=True)
    acc_ref[...] = _lane_tile(alpha, acc_ref.shape[1]) * acc_ref[...] + _dot(p.astype(BF16), v)


def _mask_scores(s_ref, visible):
    s_ref[...] = jnp.where(visible, s_ref[...], NEG_INF)


def _lane_tile(x, width):
    reps = width // x.shape[1]
    return x if reps == 1 else jnp.concatenate([x] * reps, axis=1)


SWEEP_UNROLL = 2


def _flash_sweep(n_tiles, qk, process):
    n_pairs = (n_tiles + 1) // 2
    qk(0, 0)

    def pair(jj, carry):
        j = 2 * jj
        qk(1, j + 1)
        process(0, j, False)
        qk(0, j + 2)
        process(1, j + 1, False)
        return carry

    def pair_group(gg, carry):
        for u in range(SWEEP_UNROLL):
            pair(SWEEP_UNROLL * gg + u, carry)
        return carry

    n_loop = n_pairs - 1
    lax.fori_loop(0, n_loop // SWEEP_UNROLL, pair_group, 0)
    lax.fori_loop(n_loop - n_loop % SWEEP_UNROLL, n_loop, pair, 0)
    j_last = 2 * (n_pairs - 1)
    qk(1, j_last + 1)
    process(0, j_last, True)
    process(1, j_last + 1, True)


AUG_FILL_ROWS = 512


def _fill_augmented(k_ref, v_ref, ka_ref, va_ref, block_len):
    lane = lax.broadcasted_iota(jnp.int32, (AUG_FILL_ROWS, LANES), 1)
    ones_col = jnp.where(lane == 0, 1.0, 0.0).astype(BF16)

    def fill(c, carry):
        r0 = pl.multiple_of(c * AUG_FILL_ROWS, AUG_FILL_ROWS)
        rows = pl.ds(r0, AUG_FILL_ROWS)
        blk = (r0 + lax.broadcasted_iota(jnp.int32, (AUG_FILL_ROWS, LANES), 0)) // block_len
        ka_ref[rows, 0:HEAD_DIM] = k_ref[rows, :]
        ka_ref[rows, HEAD_DIM:] = jnp.where((blk & (LANES - 1)) == lane, 1.0, 0.0).astype(BF16)
        va_ref[rows, 0:HEAD_DIM] = v_ref[rows, :]
        va_ref[rows, HEAD_DIM:] = ones_col
        return carry

    lax.fori_loop(0, k_ref.shape[0] // AUG_FILL_ROWS, fill, 0)


def _resident(block_shape, index_map):
    return pl.BlockSpec(block_shape, index_map, pipeline_mode=pl.Buffered(1))


HG_TILE = 512
HG_HEADS = 4


def _hgrn_kernel(q_ref, f_ref, i_ref, g_ref, lb_ref, nw_ref, o_ref, st_ref):
    @pl.when(pl.program_id(1) == 0)
    def _():
        st_ref[...] = jnp.zeros_like(st_ref)

    lb = lb_ref[...]
    f = lb + (1.0 - lb) * _sigmoid(f_ref[...])
    logf = jnp.log(f)
    kk = 1.0 - f
    qf = _silu(q_ref[...])
    r64 = lax.broadcasted_iota(jnp.int32, logf.shape, 0) & (A_CHUNK - 1)
    b = logf
    step = 1
    while step < A_CHUNK:
        b = b + jnp.where(r64 >= step, pltpu.roll(b, step, 0), 0.0)
        step *= 2
    causal = (lax.broadcasted_iota(jnp.int32, (A_CHUNK, A_CHUNK), 0)
              >= lax.broadcasted_iota(jnp.int32, (A_CHUNK, A_CHUNK), 1))
    nw = nw_ref[...]
    for c in range(HG_TILE // A_CHUNK):
        sl = slice(c * A_CHUNK, (c + 1) * A_CHUNK)
        for hd in range(HG_HEADS):
            hs = slice(hd * HEAD_DIM, (hd + 1) * HEAD_DIM)
            bc = b[sl, hs]
            b_last = bc[A_CHUNK - 1:A_CHUNK, :]
            q_t = (qf[sl, hs] * jnp.exp(bc)).astype(BF16)
            k_t = (kk[sl, hs] * jnp.exp(-bc)).astype(BF16)
            vc = i_ref[sl, hs].astype(BF16)
            att = jnp.where(causal, _dot_nt(q_t, k_t), 0.0)
            st = st_ref[hd]
            o = _dot(att.astype(BF16), vc) + _dot_nt(q_t, st.astype(BF16))
            kdec = (kk[sl, hs] * jnp.exp(b_last - bc)).astype(BF16)
            st_ref[hd] = st * jnp.exp(b_last) + _dot_tn(vc, kdec)
            o = o * lax.rsqrt(jnp.mean(o * o, axis=-1, keepdims=True) + RMS_EPS) * nw
            o_ref[sl, hs] = (o * _silu(g_ref[sl, hs])).astype(o_ref.dtype)


def _hgrn2(z_a, lb, norm_w):
    t = z_a.shape[0]
    ngrp = A_HEADS // HG_HEADS
    gw = HG_HEADS * HEAD_DIM
    col = lambda base: (lambda h, i: (i, base + h))
    return pl.pallas_call(
        _hgrn_kernel,
        grid=(ngrp, t // HG_TILE),
        in_specs=[pl.BlockSpec((HG_TILE, gw), col(0)),
                  pl.BlockSpec((HG_TILE, gw), col(ngrp)),
                  pl.BlockSpec((HG_TILE, gw), col(2 * ngrp)),
                  pl.BlockSpec((HG_TILE, gw), col(3 * ngrp)),
                  pl.BlockSpec((1, gw), lambda h, i: (0, h)),
                  pl.BlockSpec((1, HEAD_DIM), lambda h, i: (0, 0))],
        out_specs=pl.BlockSpec((HG_TILE, gw), lambda h, i: (i, h)),
        out_shape=jax.ShapeDtypeStruct((t, A_W), BF16),
        scratch_shapes=[pltpu.VMEM((HG_HEADS, HEAD_DIM, HEAD_DIM), F32)],
        compiler_params=_cparams(("parallel", "arbitrary")),
    )(z_a, z_a, z_a, z_a, lb.reshape(1, A_W), norm_w.reshape(1, HEAD_DIM))


SEG_W = CMP_STRIDE * HEAD_DIM


def _nsa_compress_kernel(seg_ref, pos_ref, w1_ref, w2_ref, o_ref):
    seg = seg_ref[0, 0]
    nseg = seg.shape[0]
    a = _dot(seg, w1_ref[0, :SEG_W, :])
    b = _dot(seg, w1_ref[0, SEG_W:, :])
    b_next = pltpu.roll(b, nseg - 1, 0)
    pos = jnp.broadcast_to(pos_ref[0], (8, CMP_LEN * HEAD_DIM))
    c = _dot(pos, w1_ref[0])[0:1, :]
    hid = _silu(a + b_next + c)
    o_ref[0, 0] = _dot(hid.astype(BF16), w2_ref[0]).astype(o_ref.dtype)


def _nsa_compress(seg, pos, w1, w2):
    _, ng, nseg, _ = seg.shape
    return pl.pallas_call(
        _nsa_compress_kernel,
        grid=(2, ng),
        in_specs=[pl.BlockSpec((1, 1, nseg, SEG_W), lambda a, g: (a, g, 0, 0)),
                  pl.BlockSpec((1, 1, CMP_LEN * HEAD_DIM), lambda a, g: (a, 0, 0)),
                  pl.BlockSpec((1, CMP_LEN * HEAD_DIM, HEAD_DIM), lambda a, g: (a, 0, 0)),
                  pl.BlockSpec((1, HEAD_DIM, HEAD_DIM), lambda a, g: (a, 0, 0))],
        out_specs=pl.BlockSpec((1, 1, nseg, HEAD_DIM), lambda a, g: (a, g, 0, 0)),
        out_shape=jax.ShapeDtypeStruct((2, ng, nseg, HEAD_DIM), BF16),
        compiler_params=_cparams(("parallel", "parallel")),
    )(seg, pos, w1, w2)


CS_TQ = 256
CS_WIDTH_STEP = 256
SL_NBP = 128


def _split3(x):
    hi = x.astype(BF16)
    r = x - hi.astype(F32)
    mid = r.astype(BF16)
    lo = (r - mid.astype(F32)).astype(BF16)
    return hi, mid, lo


def _topk_mask(score, lane, k):
    sel = jnp.zeros(score.shape, F32)
    width = float(score.shape[1])
    for _ in range(k):
        mx = jnp.max(score, axis=1, keepdims=True)
        idx = jnp.min(jnp.where(score == mx, lane, width), axis=1, keepdims=True)
        hit = lane == idx
        sel = jnp.where(hit, 1.0, sel)
        score = jnp.where(hit, -jnp.inf, score)
    return sel


def _nsa_cmp_kernel(q_ref, kc_ref, vc_ref, map_ref, o_ref, sel_ref, *, n_sel):
    i = pl.program_id(1)
    tq = q_ref.shape[0]
    ncmp = kc_ref.shape[2]
    nbp = map_ref.shape[1]
    t_pos = i * tq + lax.broadcasted_iota(jnp.int32, (tq, 1), 0)
    row_has_keys = t_pos >= CMP_LEN - 1
    jb = lax.broadcasted_iota(jnp.int32, (1, nbp), 1)
    cur = t_pos // SLC_LEN
    valid = jb <= cur
    forced = (jb == 0) | (jb == cur) | (jb == cur - 1)

    def body(width):
        cmp_end = lax.broadcasted_iota(jnp.int32, (1, width), 1) * CMP_STRIDE + (CMP_LEN - 1)
        mask = cmp_end <= t_pos
        kc = kc_ref[0, 0, 0:width, :]
        vc = vc_ref[0, 0, 0:width, :]
        psum = jnp.zeros((tq, width), F32)
        for hh in range(B_HPG):
            hs = slice(hh * HEAD_DIM, (hh + 1) * HEAD_DIM)
            s = jnp.where(mask, _dot_nt(q_ref[:, hs], kc) * Q_SCALE, NEG_INF)
            e = jnp.exp2(s - jnp.max(s, axis=1, keepdims=True))
            p = e * jnp.where(row_has_keys, 1.0 / jnp.sum(e, axis=1, keepdims=True), 0.0)
            o_ref[:, hs] = _dot(p.astype(BF16), vc).astype(o_ref.dtype)
            psum = psum + p
        hi, mid, lo = _split3(psum)
        cmap = map_ref[0:width, :]
        p_slc = _dot(hi, cmap) + _dot(mid, cmap) + _dot(lo, cmap)
        score = jnp.where(valid & jnp.logical_not(forced), p_slc, NEG_INF)
        sel = _topk_mask(score, jb.astype(F32), n_sel - N_FORCED)
        sel_ref[0] = jnp.where(valid, jnp.where(forced, 1.0, sel), 0.0).astype(sel_ref.dtype)

    widths = list(range(CS_WIDTH_STEP, ncmp + 1, CS_WIDTH_STEP)) if ncmp % CS_WIDTH_STEP == 0 else [ncmp]
    if len(widths) == 1:
        body(widths[0])
    else:
        n_vis = ((i + 1) * tq - CMP_LEN) // CMP_STRIDE + 1
        case = jnp.clip((n_vis + CS_WIDTH_STEP - 1) // CS_WIDTH_STEP - 1, 0, len(widths) - 1)
        for idx, width in enumerate(widths):
            pl.when(case == idx)(functools.partial(body, width))


def _cmp_to_slc_matrix(ncmp_pad, n_cmp, nb, nbp):
    ratio = SLC_LEN // CMP_STRIDE
    n_over = CMP_LEN // CMP_STRIDE
    mat = np.zeros((ncmp_pad, nbp), np.float32)
    for j in range(nb):
        for m in range(ratio):
            for n in range(n_over):
                c = ratio * j + m - n
                if 0 <= c < n_cmp:
                    mat[c, j] += 1.0
    return mat


def _nsa_cmp_select(z_b, cmp_kv):
    t = z_b.shape[0]
    ng = B_KV_GROUPS
    nseg = cmp_kv.shape[2]
    nb = t // SLC_LEN
    n_cmp = (t - CMP_LEN) // CMP_STRIDE + 1
    nbp = -(-nb // SL_NBP) * SL_NBP
    cmap = jnp.asarray(_cmp_to_slc_matrix(nseg, n_cmp, nb, nbp), BF16)
    gw = B_HPG * HEAD_DIM
    return pl.pallas_call(
        functools.partial(_nsa_cmp_kernel, n_sel=min(N_SLC, nb)),
        grid=(ng, t // CS_TQ),
        in_specs=[pl.BlockSpec((CS_TQ, gw), lambda g, i: (i, g)),
                  pl.BlockSpec((1, 1, nseg, HEAD_DIM), lambda g, i: (0, g, 0, 0)),
                  pl.BlockSpec((1, 1, nseg, HEAD_DIM), lambda g, i: (1, g, 0, 0)),
                  pl.BlockSpec((nseg, nbp), lambda g, i: (0, 0))],
        out_specs=[pl.BlockSpec((CS_TQ, gw), lambda g, i: (i, g)),
                   pl.BlockSpec((1, CS_TQ, nbp), lambda g, i: (g, i, 0))],
        out_shape=[jax.ShapeDtypeStruct((t, B_QW), BF16),
                   jax.ShapeDtypeStruct((ng, t, nbp), BF16)],
        compiler_params=_cparams(("parallel", "parallel")),
    )(z_b, cmp_kv, cmp_kv, cmap)


SL_TQ = 512
SL_TK = SL_TQ
SL_PHASE_TILES = SL_NBP * SLC_LEN // SL_TK


def _nsa_slc_kernel(q_ref, sel_ref, k_ref, v_ref, o_ref, ka_ref, va_ref, qa_ref, s_ref, m_ref, acc_ref, *, n_phase):
    i = pl.program_id(1)
    tq = SL_TQ

    @pl.when(i == 0)
    def _():
        _fill_augmented(k_ref, v_ref, ka_ref, va_ref, SLC_LEN)

    bias = ((sel_ref[0].astype(F32) - 1.0) * (-NEG_INF)).astype(BF16)
    for hh in range(B_HPG):
        rows = slice(hh * tq, (hh + 1) * tq)
        q = (q_ref[:, hh * HEAD_DIM:(hh + 1) * HEAD_DIM].astype(F32) * Q_SCALE).astype(BF16)
        for ph in range(n_phase):
            qa_ref[ph, rows, 0:HEAD_DIM] = q
            qa_ref[ph, rows, HEAD_DIM:] = bias[:, ph * SL_NBP:(ph + 1) * SL_NBP]
    _flash_init(m_ref, None, acc_ref)
    t_pos = i * tq + (lax.broadcasted_iota(jnp.int32, (B_HPG * tq, 1), 0) & (tq - 1))

    def qk(slot, j):
        off = pl.multiple_of(j * SL_TK, SL_TK)
        qa = qa_ref[0] if n_phase == 1 else qa_ref[j // SL_PHASE_TILES]
        s_ref[slot] = _dot_nt(qa, ka_ref[pl.ds(off, SL_TK), :])

    def process(slot, j, causal):
        off = pl.multiple_of(j * SL_TK, SL_TK)
        if causal:
            kpos = off + lax.broadcasted_iota(jnp.int32, (1, SL_TK), 1)
            _mask_scores(s_ref.at[slot], kpos <= t_pos)
        _flash_update(s_ref.at[slot], va_ref[pl.ds(off, SL_TK), :], m_ref, None, acc_ref)

    _flash_sweep((i * tq + tq + SL_TK - 1) // SL_TK, qk, process)
    acc = acc_ref[...]
    o = acc[:, 0:HEAD_DIM] / acc[:, HEAD_DIM:HEAD_DIM + 1]
    for hh in range(B_HPG):
        o_ref[:, hh * HEAD_DIM:(hh + 1) * HEAD_DIM] = o[hh * tq:(hh + 1) * tq].astype(o_ref.dtype)


def _nsa_slc(z_b, sel):
    t = z_b.shape[0]
    nbp = sel.shape[2]
    n_phase = nbp // SL_NBP
    gw = B_HPG * HEAD_DIM
    rows = B_HPG * SL_TQ
    k_col0 = (B_QW + 2 * B_KVW) // HEAD_DIM
    v_col0 = (B_QW + 3 * B_KVW) // HEAD_DIM
    return pl.pallas_call(
        functools.partial(_nsa_slc_kernel, n_phase=n_phase),
        grid=(B_KV_GROUPS, t // SL_TQ),
        in_specs=[pl.BlockSpec((SL_TQ, gw), lambda g, i: (i, g)),
                  pl.BlockSpec((1, SL_TQ, nbp), lambda g, i: (g, i, 0)),
                  _resident((t, HEAD_DIM), lambda g, i: (0, k_col0 + g)),
                  _resident((t, HEAD_DIM), lambda g, i: (0, v_col0 + g))],
        out_specs=pl.BlockSpec((SL_TQ, gw), lambda g, i: (i, g)),
        out_shape=jax.ShapeDtypeStruct((t, B_QW), BF16),
        scratch_shapes=[pltpu.VMEM((t, HEAD_DIM + SL_NBP), BF16),
                        pltpu.VMEM((t, 2 * HEAD_DIM), BF16),
                        pltpu.VMEM((n_phase, rows, HEAD_DIM + SL_NBP), BF16),
                        pltpu.VMEM((2, rows, SL_TK), F32),
                        pltpu.VMEM((rows, LANES), F32),
                        pltpu.VMEM((rows, 2 * HEAD_DIM), F32)],
        compiler_params=_cparams(("arbitrary", "arbitrary")),
    )(z_b, sel, z_b, z_b)


WN_TQ = WIN


def _nsa_win_kernel(q_ref, ka_ref, kb_ref, va_ref, vb_ref, oc_ref, os_ref, gate_ref, o_ref, qs_ref):
    i = pl.program_id(1)
    tq = WN_TQ
    for hh in range(B_HPG):
        q = q_ref[:, hh * HEAD_DIM:(hh + 1) * HEAD_DIM].astype(F32) * Q_SCALE
        qs_ref[hh * tq:(hh + 1) * tq, :] = q.astype(BF16)
    qs = qs_ref[...]
    t_loc = lax.broadcasted_iota(jnp.int32, (B_HPG * tq, 1), 0) & (tq - 1)
    c_loc = lax.broadcasted_iota(jnp.int32, (1, tq), 1)
    t_prev = t_loc + jnp.where(i > 0, 0, tq)
    s_a = jnp.where(c_loc > t_prev, _dot_nt(qs, ka_ref[...]), NEG_INF)
    s_b = jnp.where(c_loc <= t_loc, _dot_nt(qs, kb_ref[...]), NEG_INF)
    m = jnp.maximum(jnp.max(s_a, axis=1, keepdims=True), jnp.max(s_b, axis=1, keepdims=True))
    p_a = jnp.exp2(s_a - m)
    p_b = jnp.exp2(s_b - m)
    l = jnp.sum(p_a, axis=1, keepdims=True) + jnp.sum(p_b, axis=1, keepdims=True)
    o_w = (_dot(p_a.astype(BF16), va_ref[...]) + _dot(p_b.astype(BF16), vb_ref[...])) / l
    gates = _sigmoid(gate_ref[...])
    for hh in range(B_HPG):
        hs = slice(hh * HEAD_DIM, (hh + 1) * HEAD_DIM)
        g_c, g_s, g_w = (gates[:, 3 * hh + c:3 * hh + c + 1] for c in range(3))
        o = (g_c * oc_ref[:, hs].astype(F32) + g_s * os_ref[:, hs].astype(F32)
             + g_w * o_w[hh * tq:(hh + 1) * tq])
        o_ref[:, hs] = o.astype(o_ref.dtype)


def _nsa_win_combine(z_b, o_c, o_s, z_gate):
    t = z_b.shape[0]
    gw = B_HPG * HEAD_DIM
    k_col0 = (B_QW + 4 * B_KVW) // HEAD_DIM
    v_col0 = (B_QW + 5 * B_KVW) // HEAD_DIM
    prev_tile = lambda col0: (lambda g, i: (jnp.maximum(i - 1, 0), col0 + g))
    this_tile = lambda col0: (lambda g, i: (i, col0 + g))
    kv_block = (WN_TQ, HEAD_DIM)
    return pl.pallas_call(
        _nsa_win_kernel,
        grid=(B_KV_GROUPS, t // WN_TQ),
        in_specs=[pl.BlockSpec((WN_TQ, gw), lambda g, i: (i, g)),
                  pl.BlockSpec(kv_block, prev_tile(k_col0)),
                  pl.BlockSpec(kv_block, this_tile(k_col0)),
                  pl.BlockSpec(kv_block, prev_tile(v_col0)),
                  pl.BlockSpec(kv_block, this_tile(v_col0)),
                  pl.BlockSpec((WN_TQ, gw), lambda g, i: (i, g)),
                  pl.BlockSpec((WN_TQ, gw), lambda g, i: (i, g)),
                  pl.BlockSpec((WN_TQ, HEAD_DIM), lambda g, i: (i, g))],
        out_specs=pl.BlockSpec((WN_TQ, gw), lambda g, i: (i, g)),
        out_shape=jax.ShapeDtypeStruct((t, B_QW), BF16),
        scratch_shapes=[pltpu.VMEM((B_HPG * WN_TQ, HEAD_DIM), BF16)],
        compiler_params=_cparams(("parallel", "parallel")),
    )(z_b, z_b, z_b, z_b, z_b, o_c, o_s, z_gate)


def _nsa(z_b, z_gate, pos_k, w1k, w2k, pos_v, w1v, w2v):
    t = z_b.shape[0]
    ng = B_KV_GROUPS

    def segs(col0):
        z = z_b[:, col0:col0 + B_KVW].reshape(t, ng, HEAD_DIM).transpose(1, 0, 2)
        return z.reshape(ng, t // CMP_STRIDE, SEG_W)

    seg = jnp.stack([segs(B_QW), segs(B_QW + B_KVW)])
    pos = jnp.stack([pos_k, pos_v]).reshape(2, 1, CMP_LEN * HEAD_DIM).astype(BF16)
    w1 = jnp.stack([w1k, w1v]).astype(BF16)
    w2 = jnp.stack([w2k, w2v]).astype(BF16)
    cmp_kv = _nsa_compress(seg, pos, w1, w2)
    o_c, sel = _nsa_cmp_select(z_b, cmp_kv)
    o_s = _nsa_slc(z_b, sel)
    return _nsa_win_combine(z_b, o_c, o_s, z_gate)


KM_ROWS = 8


def _kmean_kernel(k_ref, o_ref):
    x = k_ref[...].astype(F32)
    o_ref[...] = jnp.mean(x.reshape(KM_ROWS, MOBA_BLOCK, x.shape[1]), axis=1)


def _moba_kmean(z_cd):
    t = z_cd.shape[0]
    nbm = t // MOBA_BLOCK
    return pl.pallas_call(
        _kmean_kernel,
        grid=(nbm // KM_ROWS,),
        in_specs=[pl.BlockSpec((KM_ROWS * MOBA_BLOCK, C_W), lambda i: (i, 1))],
        out_specs=pl.BlockSpec((KM_ROWS, C_W), lambda i: (i, 0)),
        out_shape=jax.ShapeDtypeStruct((nbm, C_W), F32),
        compiler_params=_cparams(("parallel",)),
    )(z_cd)


MB_TQ = 1024
MB_TK = MB_TQ // 2
MB_NBP = 128


def _moba_kernel(q_ref, km_ref, k_ref, v_ref, o_ref, ka_ref, va_ref, qa_ref, s_ref, m_ref, acc_ref):
    i = pl.program_id(1)
    tq = MB_TQ

    @pl.when(i == 0)
    def _():
        _fill_augmented(k_ref, v_ref, ka_ref, va_ref, MOBA_BLOCK)

    q = q_ref[...]
    km = km_ref[...]
    km_hi = km.astype(BF16)
    km_lo = (km - km_hi.astype(F32)).astype(BF16)
    gate = _dot_nt(q, km_hi) + _dot_nt(q, km_lo)
    jb = lax.broadcasted_iota(jnp.int32, (1, MB_NBP), 1)
    t_pos = i * tq + lax.broadcasted_iota(jnp.int32, (tq, 1), 0)
    cur = t_pos // MOBA_BLOCK
    earlier = jb < cur
    sel = _topk_mask(jnp.where(earlier, gate, NEG_INF), jb.astype(F32), MOBA_TOPK)
    sel = jnp.where(jb == cur, 1.0, jnp.where(earlier, sel, 0.0))
    qa_ref[:, 0:HEAD_DIM] = (q.astype(F32) * Q_SCALE).astype(BF16)
    qa_ref[:, HEAD_DIM:] = ((sel - 1.0) * (-NEG_INF)).astype(BF16)
    _flash_init(m_ref, None, acc_ref)

    def qk(slot, j):
        off = pl.multiple_of(j * MB_TK, MB_TK)
        s_ref[slot] = _dot_nt(qa_ref[...], ka_ref[pl.ds(off, MB_TK), :])

    def process(slot, j, causal):
        off = pl.multiple_of(j * MB_TK, MB_TK)
        if causal:
            kpos = off + lax.broadcasted_iota(jnp.int32, (1, MB_TK), 1)
            _mask_scores(s_ref.at[slot], kpos <= t_pos)
        _flash_update(s_ref.at[slot], va_ref[pl.ds(off, MB_TK), :], m_ref, None, acc_ref)

    _flash_sweep((i * tq + tq + MB_TK - 1) // MB_TK, qk, process)
    acc = acc_ref[...]
    o_ref[...] = (acc[:, 0:HEAD_DIM] / acc[:, HEAD_DIM:HEAD_DIM + 1]).astype(o_ref.dtype)


def _moba(z_cd):
    t = z_cd.shape[0]
    nbm = t // MOBA_BLOCK
    nh = C_HEADS
    k_mean = _moba_kmean(z_cd)
    k_mean = jnp.pad(k_mean, ((0, MB_NBP - nbm), (0, 0)))
    return pl.pallas_call(
        _moba_kernel,
        grid=(nh, t // MB_TQ),
        in_specs=[pl.BlockSpec((MB_TQ, HEAD_DIM), lambda h, i: (i, h)),
                  pl.BlockSpec((MB_NBP, HEAD_DIM), lambda h, i: (0, h)),
                  _resident((t, HEAD_DIM), lambda h, i: (0, nh + h)),
                  _resident((t, HEAD_DIM), lambda h, i: (0, 2 * nh + h))],
        out_specs=pl.BlockSpec((MB_TQ, HEAD_DIM), lambda h, i: (i, h)),
        out_shape=jax.ShapeDtypeStruct((t, C_W), BF16),
        scratch_shapes=[pltpu.VMEM((t, HEAD_DIM + MB_NBP), BF16),
                        pltpu.VMEM((t, 2 * HEAD_DIM), BF16),
                        pltpu.VMEM((MB_TQ, HEAD_DIM + MB_NBP), BF16),
                        pltpu.VMEM((2, MB_TQ, MB_TK), F32),
                        pltpu.VMEM((MB_TQ, LANES), F32),
                        pltpu.VMEM((MB_TQ, 2 * HEAD_DIM), F32)],
        compiler_params=_cparams(("arbitrary", "arbitrary")),
    )(z_cd, k_mean, z_cd, z_cd)


DF_TQ = 1024
DF_TK = DF_TQ // 2


def _diff_kernel(q_ref, k_ref, v_ref, lq1_ref, lk1_ref, lq2_ref, lk2_ref, nw_ref, o_ref,
                 qs_ref, s_ref, m_ref, l_ref, acc_ref, *, lam_init):
    i = pl.program_id(1)
    tq = DF_TQ
    for mp in range(2):
        q = q_ref[:, mp * D_DK:(mp + 1) * D_DK].astype(F32) * (D_DK ** -0.5 * LOG2E)
        qs_ref[mp] = q.astype(BF16)
    _flash_init(m_ref, l_ref, acc_ref)
    t_pos = i * tq + (lax.broadcasted_iota(jnp.int32, (2 * tq, 1), 0) & (tq - 1))

    def qk(slot, j):
        off = pl.multiple_of(j * DF_TK, DF_TK)
        for mp in range(2):
            s_ref[slot, mp * tq:(mp + 1) * tq, :] = _dot_nt(
                qs_ref[mp], k_ref[pl.ds(off, DF_TK), mp * D_DK:(mp + 1) * D_DK])

    def process(slot, j, causal):
        off = pl.multiple_of(j * DF_TK, DF_TK)
        if causal:
            kpos = off + lax.broadcasted_iota(jnp.int32, (1, DF_TK), 1)
            _mask_scores(s_ref.at[slot], kpos <= t_pos)
        _flash_update(s_ref.at[slot], v_ref[pl.ds(off, DF_TK), :], m_ref, l_ref, acc_ref)

    _flash_sweep((i * tq + tq + DF_TK - 1) // DF_TK, qk, process)
    lam = (jnp.exp(jnp.sum(lq1_ref[...] * lk1_ref[...], axis=1, keepdims=True))
           - jnp.exp(jnp.sum(lq2_ref[...] * lk2_ref[...], axis=1, keepdims=True)) + lam_init)
    o = (acc_ref[0:tq, :] / _lane_tile(l_ref[0:tq, :], D_DV)
         - lam * (acc_ref[tq:, :] / _lane_tile(l_ref[tq:, :], D_DV)))
    o = o * lax.rsqrt(jnp.mean(o * o, axis=-1, keepdims=True) + RMS_EPS) * nw_ref[...]
    o_ref[...] = (o * (1.0 - lam_init)).astype(o_ref.dtype)


def _diff_attn(z_cd, lq1, lk1, lq2, lk2, subln_w, layer_idx):
    t = z_cd.shape[0]
    lam_init = 0.8 - 0.6 * math.exp(-0.3 * layer_idx)
    q_col0 = 3 * C_W // D_DV
    k_col0 = (3 * C_W + D_QW) // D_DV
    v_col0 = (3 * C_W + 2 * D_QW) // D_DV
    vec = pl.BlockSpec((1, D_DK), lambda h, i: (0, 0))
    return pl.pallas_call(
        functools.partial(_diff_kernel, lam_init=lam_init),
        grid=(D_HEADS, t // DF_TQ),
        in_specs=[pl.BlockSpec((DF_TQ, 2 * D_DK), lambda h, i: (i, q_col0 + h)),
                  _resident((t, 2 * D_DK), lambda h, i: (0, k_col0 + h)),
                  _resident((t, D_DV), lambda h, i: (0, v_col0 + h)),
                  vec, vec, vec, vec,
                  pl.BlockSpec((1, D_DV), lambda h, i: (0, 0))],
        out_specs=pl.BlockSpec((DF_TQ, D_DV), lambda h, i: (i, h)),
        out_shape=jax.ShapeDtypeStruct((t, D_VW), BF16),
        scratch_shapes=[pltpu.VMEM((2, DF_TQ, D_DK), BF16),
                        pltpu.VMEM((2, 2 * DF_TQ, DF_TK), F32),
                        pltpu.VMEM((2 * DF_TQ, LANES), F32),
                        pltpu.VMEM((2 * DF_TQ, LANES), F32),
                        pltpu.VMEM((2 * DF_TQ, D_DV), F32)],
        compiler_params=_cparams(("parallel", "arbitrary")),
    )(z_cd, z_cd, z_cd, lq1.reshape(1, D_DK), lk1.reshape(1, D_DK), lq2.reshape(1, D_DK),
      lk2.reshape(1, D_DK), subln_w.reshape(1, D_DV))


def _gate_weight(w_gate):
    d = w_gate.shape[0]
    per = 3 * B_HPG
    wg = w_gate.reshape(d, B_KV_GROUPS, per)
    wg = jnp.pad(wg, ((0, 0), (0, 0), (0, HEAD_DIM - per)))
    return wg.reshape(d, B_KV_GROUPS * HEAD_DIM)


def kernel(x, ln_w, ln_b, ffn_w_in, ffn_w_out, ab_w_in, ab_w_out, hgrn_lower_bounds, hgrn_norm_w,
           nsa_cmp_pos_k, nsa_cmp_k_w1, nsa_cmp_k_w2, nsa_cmp_pos_v, nsa_cmp_v_w1, nsa_cmp_v_w2,
           cd_w_in, cd_w_out, diff_lambda_q1, diff_lambda_k1, diff_lambda_q2, diff_lambda_k2, diff_subln_w):
    bsz, t, d = x.shape
    lb_all = jnp.cumsum(jax.nn.softmax(hgrn_lower_bounds.astype(F32), axis=0), axis=0)
    outs = []
    for bi in range(bsz):
        h = x[bi]
        h_in = h.astype(BF16)
        for layer in range(DEPTH):
            if layer % 2 == 0:
                e = layer // 2
                n_a = 4 * A_W
                n_b = B_QW + 6 * B_KVW
                z_a = _matmul(h_in, ab_w_in, e, 0, n_a, F32)
                z_b = _matmul(h_in, ab_w_in, e, n_a, n_b, BF16)
                w_gate = _gate_weight(ab_w_in[e, :, n_a + n_b:])
                z_g = _matmul(h_in, w_gate[None], 0, 0, w_gate.shape[1], F32)
                o_a = _hgrn2(z_a, lb_all[layer], hgrn_norm_w[e])
                o_b = _nsa(z_b, z_g, nsa_cmp_pos_k[e], nsa_cmp_k_w1[e], nsa_cmp_k_w2[e],
                           nsa_cmp_pos_v[e], nsa_cmp_v_w1[e], nsa_cmp_v_w2[e])
                mix_in = (o_a, o_b)
                w_out = ab_w_out[e]
            else:
                oi = layer // 2
                z_cd = _matmul(h_in, cd_w_in, oi, 0, cd_w_in.shape[2], BF16)
                o_c = _moba(z_cd)
                o_d = _diff_attn(z_cd, diff_lambda_q1[oi], diff_lambda_k1[oi], diff_lambda_q2[oi],
                                 diff_lambda_k2[oi], diff_subln_w[oi], layer)
                mix_in = (o_c, o_d)
                w_out = cd_w_out[oi]
            h, h_in = _mm_res_ln(mix_in, w_out.astype(BF16), h, ln_w[layer, 0], ln_b[layer, 0])
            act = _ffn_in(h_in, ffn_w_in, layer)
            h, h_in = _mm_res_ln((act,), ffn_w_out[layer].astype(BF16), h, ln_w[layer, 1], ln_b[layer, 1])
        outs.append(h.reshape(1, t, d))
    return outs[0] if bsz == 1 else jnp.concatenate(outs, axis=0)
```

```python
import functools
import math

import numpy as np
import jax
import jax.numpy as jnp
from jax import lax
from jax.experimental import pallas as pl
from jax.experimental.pallas import tpu as pltpu

F32 = jnp.float32
BF16 = jnp.bfloat16

DEPTH = 2
HEAD_DIM = 128
A_HEADS = 8
A_CHUNK = 64
B_HEADS = 8
B_KV_GROUPS = 2
B_HPG = B_HEADS // B_KV_GROUPS
CMP_LEN = 32
CMP_STRIDE = 16
SLC_LEN = 64
N_SLC = 16
N_FORCED = 3
WIN = 512
C_HEADS = 8
MOBA_BLOCK = 256
MOBA_TOPK = 3
D_HEADS = 4
D_DK = 128
D_DV = 2 * D_DK

DEEPNORM_ALPHA = (2 * DEPTH) ** 0.25
NEG_INF = -1e30
LN_EPS = 1e-5
RMS_EPS = 1e-6
ATT_SCALE = HEAD_DIM ** -0.5

A_W = A_HEADS * HEAD_DIM
B_QW = B_HEADS * HEAD_DIM
B_KVW = B_KV_GROUPS * HEAD_DIM
C_W = C_HEADS * HEAD_DIM
D_QW = D_HEADS * 2 * D_DK
D_VW = D_HEADS * D_DV

VMEM_LIMIT = 56 * 1024 * 1024
MM_SMALL_WEIGHT_BYTES = 8 * 1024 * 1024


def _cparams(sem):
    return pltpu.CompilerParams(dimension_semantics=sem, vmem_limit_bytes=VMEM_LIMIT)


def _dot(a, b):
    return jnp.dot(a, b, preferred_element_type=F32)


def _dot_nt(a, b):
    return lax.dot_general(a, b, (((1,), (1,)), ((), ())), preferred_element_type=F32)


def _dot_tn(a, b):
    return lax.dot_general(a, b, (((0,), (0,)), ((), ())), preferred_element_type=F32)


def _sigmoid(x):
    return 1.0 / (1.0 + jnp.exp(-x))


def _silu(x):
    return x * _sigmoid(x)


MM_TM = 1024
MM_TN = 512


def _mm_kernel(x_ref, w_ref, o_ref, wb_ref):
    @pl.when(pl.program_id(1) == 0)
    def _():
        wb_ref[...] = w_ref[0].astype(BF16)

    o_ref[...] = _dot(x_ref[...], wb_ref[...]).astype(o_ref.dtype)


def _matmul(x, w, widx, col0, n, out_dtype):
    m, k = x.shape
    tn = min(MM_TN, n)
    j0 = col0 // tn
    return pl.pallas_call(
        _mm_kernel,
        grid=(n // tn, m // MM_TM),
        in_specs=[pl.BlockSpec((MM_TM, k), lambda j, i: (i, 0)),
                  pl.BlockSpec((1, k, tn), lambda j, i: (widx, 0, j0 + j))],
        out_specs=pl.BlockSpec((MM_TM, tn), lambda j, i: (i, j)),
        out_shape=jax.ShapeDtypeStruct((m, n), out_dtype),
        scratch_shapes=[pltpu.VMEM((k, tn), BF16)],
        compiler_params=_cparams(("arbitrary", "arbitrary")),
    )(x, w)


def _ffn_in_kernel(x_ref, wg_ref, wu_ref, o_ref, wgb_ref, wub_ref):
    @pl.when(pl.program_id(1) == 0)
    def _():
        wgb_ref[...] = wg_ref[0].astype(BF16)
        wub_ref[...] = wu_ref[0].astype(BF16)

    x = x_ref[...]
    g = _dot(x, wgb_ref[...])
    u = _dot(x, wub_ref[...])
    o_ref[...] = (_silu(g) * u).astype(o_ref.dtype)


def _ffn_in(x, w_in, layer):
    m, k = x.shape
    dff = w_in.shape[2] // 2
    nj = dff // MM_TN
    return pl.pallas_call(
        _ffn_in_kernel,
        grid=(nj, m // MM_TM),
        in_specs=[pl.BlockSpec((MM_TM, k), lambda j, i: (i, 0)),
                  pl.BlockSpec((1, k, MM_TN), lambda j, i: (layer, 0, j)),
                  pl.BlockSpec((1, k, MM_TN), lambda j, i: (layer, 0, j + nj))],
        out_specs=pl.BlockSpec((MM_TM, MM_TN), lambda j, i: (i, j)),
        out_shape=jax.ShapeDtypeStruct((m, dff), BF16),
        scratch_shapes=[pltpu.VMEM((k, MM_TN), BF16), pltpu.VMEM((k, MM_TN), BF16)],
        compiler_params=_cparams(("arbitrary", "arbitrary")),
    )(x, w_in, w_in)


def _layer_norm(y, w, b):
    mu = jnp.mean(y, axis=-1, keepdims=True)
    d = y - mu
    var = jnp.mean(d * d, axis=-1, keepdims=True)
    return d * lax.rsqrt(var + LN_EPS) * w + b


def _mm_res_ln_kernel(*refs, widths):
    part_refs = refs[:len(widths)]
    w_ref, h_ref, lnw_ref, lnb_ref, o_ref, obf_ref = refs[len(widths):]
    y = DEEPNORM_ALPHA * h_ref[...]
    k0 = 0
    for a_ref, width in zip(part_refs, widths):
        y = y + _dot(a_ref[...], w_ref[0, k0:k0 + width, :])
        k0 += width
    out = _layer_norm(y, lnw_ref[...], lnb_ref[...])
    o_ref[...] = out
    obf_ref[...] = out.astype(BF16)


def _mm_res_ln(parts, w, widx, h, ln_w, ln_b):
    m = parts[0].shape[0]
    widths = tuple(a.shape[1] for a in parts)
    _, kdim, n = w.shape
    tm = 512 if kdim * n * 2 <= MM_SMALL_WEIGHT_BYTES else 256
    return pl.pallas_call(
        functools.partial(_mm_res_ln_kernel, widths=widths),
        grid=(m // tm,),
        in_specs=[pl.BlockSpec((tm, width), lambda i: (i, 0)) for width in widths]
        + [_resident((1, kdim, n), lambda i: (widx, 0, 0)),
           pl.BlockSpec((tm, n), lambda i: (i, 0)),
           pl.BlockSpec((1, n), lambda i: (0, 0)),
           pl.BlockSpec((1, n), lambda i: (0, 0))],
        out_specs=[pl.BlockSpec((tm, n), lambda i: (i, 0)),
                   pl.BlockSpec((tm, n), lambda i: (i, 0))],
        out_shape=[jax.ShapeDtypeStruct((m, n), F32), jax.ShapeDtypeStruct((m, n), BF16)],
        compiler_params=_cparams(("parallel",)),
    )(*parts, w, h, ln_w.reshape(1, n), ln_b.reshape(1, n))


LOG2E = math.log2(math.e)
LANES = 128
Q_SCALE = ATT_SCALE * LOG2E


def _flash_init(m_ref, l_ref, acc_ref):
    m_ref[...] = jnp.full(m_ref.shape, -jnp.inf, F32)
    if l_ref is not None:
        l_ref[...] = jnp.zeros(l_ref.shape, F32)
    acc_ref[...] = jnp.zeros(acc_ref.shape, F32)


def _flash_update(s_ref, v, m_ref, l_ref, acc_ref):
    m_prev = m_ref[...]
    m_new = jnp.maximum(m_prev, jnp.max(s_ref[...], axis=1, keepdims=True))
    m_ref[...] = m_new
    alpha = jnp.exp2(m_prev - m_new)
    p = jnp.exp2(s_ref[...] - _lane_tile(m_new, s_ref.shape[1]))
    if l_ref is not None:
        l_ref[...] = alpha * l_ref[...] + jnp.sum(p, axis=1, keepdims=True)
    acc_ref[...] = _lane_tile(alpha, acc_ref.shape[1]) * acc_ref[...] + _dot(p.astype(BF16), v)


def _mask_scores(s_ref, visible):
    s_ref[...] = jnp.where(visible, s_ref[...], NEG_INF)


def _lane_tile(x, width):
    reps = width // x.shape[1]
    return x if reps == 1 else jnp.concatenate([x] * reps, axis=1)


SWEEP_UNROLL = 2


def _flash_sweep(n_tiles, qk, process, unroll=SWEEP_UNROLL):
    n_pairs = (n_tiles + 1) // 2
    qk(0, 0)

    def pair(jj, carry):
        j = 2 * jj
        qk(1, j + 1)
        process(0, j, False)
        qk(0, j + 2)
        process(1, j + 1, False)
        return carry

    def pair_group(gg, carry):
        for u in range(unroll):
            pair(unroll * gg + u, carry)
        return carry

    n_loop = n_pairs - 1
    lax.fori_loop(0, n_loop // unroll, pair_group, 0)
    lax.fori_loop(n_loop - n_loop % unroll, n_loop, pair, 0)
    j_last = 2 * (n_pairs - 1)
    qk(1, j_last + 1)
    process(0, j_last, True)
    process(1, j_last + 1, True)


AUG_FILL_ROWS = 512


def _fill_augmented(k_ref, v_ref, ka_ref, va_ref, block_len):
    lane = lax.broadcasted_iota(jnp.int32, (AUG_FILL_ROWS, LANES), 1)
    ones_col = jnp.where(lane == 0, 1.0, 0.0).astype(BF16)

    def fill(c, carry):
        r0 = pl.multiple_of(c * AUG_FILL_ROWS, AUG_FILL_ROWS)
        rows = pl.ds(r0, AUG_FILL_ROWS)
        blk = (r0 + lax.broadcasted_iota(jnp.int32, (AUG_FILL_ROWS, LANES), 0)) // block_len
        ka_ref[rows, 0:HEAD_DIM] = k_ref[rows, :]
        ka_ref[rows, HEAD_DIM:] = jnp.where((blk & (LANES - 1)) == lane, 1.0, 0.0).astype(BF16)
        va_ref[rows, 0:HEAD_DIM] = v_ref[rows, :]
        va_ref[rows, HEAD_DIM:] = ones_col
        return carry

    lax.fori_loop(0, k_ref.shape[0] // AUG_FILL_ROWS, fill, 0)


def _resident(block_shape, index_map):
    return pl.BlockSpec(block_shape, index_map, pipeline_mode=pl.Buffered(1))


HG_TILE = 512
HG_HEADS = 4


def _hgrn_kernel(q_ref, f_ref, i_ref, g_ref, lb_ref, nw_ref, o_ref, st_ref):
    @pl.when(pl.program_id(1) == 0)
    def _():
        st_ref[...] = jnp.zeros_like(st_ref)

    lb = lb_ref[...]
    f = lb + (1.0 - lb) * _sigmoid(f_ref[...])
    logf = jnp.log(f)
    kk = 1.0 - f
    qf = _silu(q_ref[...])
    r64 = lax.broadcasted_iota(jnp.int32, logf.shape, 0) & (A_CHUNK - 1)
    b = logf
    step = 1
    while step < A_CHUNK:
        b = b + jnp.where(r64 >= step, pltpu.roll(b, step, 0), 0.0)
        step *= 2
    causal = (lax.broadcasted_iota(jnp.int32, (A_CHUNK, A_CHUNK), 0)
              >= lax.broadcasted_iota(jnp.int32, (A_CHUNK, A_CHUNK), 1))
    nw = nw_ref[...]
    for c in range(HG_TILE // A_CHUNK):
        sl = slice(c * A_CHUNK, (c + 1) * A_CHUNK)
        for hd in range(HG_HEADS):
            hs = slice(hd * HEAD_DIM, (hd + 1) * HEAD_DIM)
            bc = b[sl, hs]
            b_last = bc[A_CHUNK - 1:A_CHUNK, :]
            q_t = (qf[sl, hs] * jnp.exp(bc)).astype(BF16)
            k_t = (kk[sl, hs] * jnp.exp(-bc)).astype(BF16)
            vc = i_ref[sl, hs].astype(BF16)
            att = jnp.where(causal, _dot_nt(q_t, k_t), 0.0)
            st = st_ref[hd]
            o = _dot(att.astype(BF16), vc) + _dot_nt(q_t, st.astype(BF16))
            kdec = (kk[sl, hs] * jnp.exp(b_last - bc)).astype(BF16)
            st_ref[hd] = st * jnp.exp(b_last) + _dot_tn(vc, kdec)
            o = o * lax.rsqrt(jnp.mean(o * o, axis=-1, keepdims=True) + RMS_EPS) * nw
            o_ref[sl, hs] = (o * _silu(g_ref[sl, hs])).astype(o_ref.dtype)


def _hgrn2(z_a, lb, norm_w):
    t = z_a.shape[0]
    ngrp = A_HEADS // HG_HEADS
    gw = HG_HEADS * HEAD_DIM
    col = lambda base: (lambda h, i: (i, base + h))
    return pl.pallas_call(
        _hgrn_kernel,
        grid=(ngrp, t // HG_TILE),
        in_specs=[pl.BlockSpec((HG_TILE, gw), col(0)),
                  pl.BlockSpec((HG_TILE, gw), col(ngrp)),
                  pl.BlockSpec((HG_TILE, gw), col(2 * ngrp)),
                  pl.BlockSpec((HG_TILE, gw), col(3 * ngrp)),
                  pl.BlockSpec((1, gw), lambda h, i: (0, h)),
                  pl.BlockSpec((1, HEAD_DIM), lambda h, i: (0, 0))],
        out_specs=pl.BlockSpec((HG_TILE, gw), lambda h, i: (i, h)),
        out_shape=jax.ShapeDtypeStruct((t, A_W), BF16),
        scratch_shapes=[pltpu.VMEM((HG_HEADS, HEAD_DIM, HEAD_DIM), F32)],
        compiler_params=_cparams(("parallel", "arbitrary")),
    )(z_a, z_a, z_a, z_a, lb.reshape(1, A_W), norm_w.reshape(1, HEAD_DIM))


SEG_W = CMP_STRIDE * HEAD_DIM


def _nsa_compress_kernel(seg_ref, pos_ref, w1_ref, w2_ref, o_ref):
    seg = seg_ref[0, 0]
    nseg = seg.shape[0]
    a = _dot(seg, w1_ref[0, :SEG_W, :])
    b = _dot(seg, w1_ref[0, SEG_W:, :])
    b_next = pltpu.roll(b, nseg - 1, 0)
    pos = jnp.broadcast_to(pos_ref[0], (8, CMP_LEN * HEAD_DIM))
    c = _dot(pos, w1_ref[0])[0:1, :]
    hid = _silu(a + b_next + c)
    o_ref[0, 0] = _dot(hid.astype(BF16), w2_ref[0]).astype(o_ref.dtype)


def _nsa_compress(seg, pos, w1, w2):
    _, ng, nseg, _ = seg.shape
    return pl.pallas_call(
        _nsa_compress_kernel,
        grid=(2, ng),
        in_specs=[pl.BlockSpec((1, 1, nseg, SEG_W), lambda a, g: (a, g, 0, 0)),
                  pl.BlockSpec((1, 1, CMP_LEN * HEAD_DIM), lambda a, g: (a, 0, 0)),
                  pl.BlockSpec((1, CMP_LEN * HEAD_DIM, HEAD_DIM), lambda a, g: (a, 0, 0)),
                  pl.BlockSpec((1, HEAD_DIM, HEAD_DIM), lambda a, g: (a, 0, 0))],
        out_specs=pl.BlockSpec((1, 1, nseg, HEAD_DIM), lambda a, g: (a, g, 0, 0)),
        out_shape=jax.ShapeDtypeStruct((2, ng, nseg, HEAD_DIM), BF16),
        compiler_params=_cparams(("parallel", "parallel")),
    )(seg, pos, w1, w2)


CS_TQ = 256
CS_WIDTH_STEP = 256
SL_NBP = 128


def _split3(x):
    hi = x.astype(BF16)
    r = x - hi.astype(F32)
    mid = r.astype(BF16)
    lo = (r - mid.astype(F32)).astype(BF16)
    return hi, mid, lo


def _topk_mask(score, lane, k):
    sel = jnp.zeros(score.shape, F32)
    width = float(score.shape[1])
    for _ in range(k):
        mx = jnp.max(score, axis=1, keepdims=True)
        idx = jnp.min(jnp.where(score == mx, lane, width), axis=1, keepdims=True)
        hit = lane == idx
        sel = jnp.where(hit, 1.0, sel)
        score = jnp.where(hit, -jnp.inf, score)
    return sel


def _nsa_cmp_kernel(q_ref, kc_ref, vc_ref, map_ref, o_ref, sel_ref, *, n_sel):
    i = pl.program_id(1)
    tq = q_ref.shape[0]
    ncmp = kc_ref.shape[2]
    nbp = map_ref.shape[1]
    t_pos = i * tq + lax.broadcasted_iota(jnp.int32, (tq, 1), 0)
    row_has_keys = t_pos >= CMP_LEN - 1
    jb = lax.broadcasted_iota(jnp.int32, (1, nbp), 1)
    cur = t_pos // SLC_LEN
    valid = jb <= cur
    forced = (jb == 0) | (jb == cur) | (jb == cur - 1)

    def body(width):
        cmp_end = lax.broadcasted_iota(jnp.int32, (1, width), 1) * CMP_STRIDE + (CMP_LEN - 1)
        mask = cmp_end <= t_pos
        kc = kc_ref[0, 0, 0:width, :]
        vc = vc_ref[0, 0, 0:width, :]
        psum = jnp.zeros((tq, width), F32)
        for hh in range(B_HPG):
            hs = slice(hh * HEAD_DIM, (hh + 1) * HEAD_DIM)
            s = jnp.where(mask, _dot_nt(q_ref[:, hs], kc) * Q_SCALE, NEG_INF)
            e = jnp.exp2(s - jnp.max(s, axis=1, keepdims=True))
            p = e * jnp.where(row_has_keys, 1.0 / jnp.sum(e, axis=1, keepdims=True), 0.0)
            o_ref[:, hs] = _dot(p.astype(BF16), vc).astype(o_ref.dtype)
            psum = psum + p
        hi, mid, lo = _split3(psum)
        cmap = map_ref[0:width, :]
        p_slc = _dot(hi, cmap) + _dot(mid, cmap) + _dot(lo, cmap)
        score = jnp.where(valid & jnp.logical_not(forced), p_slc, NEG_INF)
        sel = _topk_mask(score, jb.astype(F32), n_sel - N_FORCED)
        sel_ref[0] = jnp.where(valid, jnp.where(forced, 1.0, sel), 0.0).astype(sel_ref.dtype)

    widths = list(range(CS_WIDTH_STEP, ncmp + 1, CS_WIDTH_STEP)) if ncmp % CS_WIDTH_STEP == 0 else [ncmp]
    if len(widths) == 1:
        body(widths[0])
    else:
        n_vis = ((i + 1) * tq - CMP_LEN) // CMP_STRIDE + 1
        case = jnp.clip((n_vis + CS_WIDTH_STEP - 1) // CS_WIDTH_STEP - 1, 0, len(widths) - 1)
        for idx, width in enumerate(widths):
            pl.when(case == idx)(functools.partial(body, width))


def _cmp_to_slc_matrix(ncmp_pad, n_cmp, nb, nbp):
    ratio = SLC_LEN // CMP_STRIDE
    n_over = CMP_LEN // CMP_STRIDE
    mat = np.zeros((ncmp_pad, nbp), np.float32)
    for j in range(nb):
        for m in range(ratio):
            for n in range(n_over):
                c = ratio * j + m - n
                if 0 <= c < n_cmp:
                    mat[c, j] += 1.0
    return mat


def _nsa_cmp_select(z_b, cmp_kv):
    t = z_b.shape[0]
    ng = B_KV_GROUPS
    nseg = cmp_kv.shape[2]
    nb = t // SLC_LEN
    n_cmp = (t - CMP_LEN) // CMP_STRIDE + 1
    nbp = -(-nb // SL_NBP) * SL_NBP
    cmap = jnp.asarray(_cmp_to_slc_matrix(nseg, n_cmp, nb, nbp), BF16)
    gw = B_HPG * HEAD_DIM
    return pl.pallas_call(
        functools.partial(_nsa_cmp_kernel, n_sel=min(N_SLC, nb)),
        grid=(ng, t // CS_TQ),
        in_specs=[pl.BlockSpec((CS_TQ, gw), lambda g, i: (i, g)),
                  pl.BlockSpec((1, 1, nseg, HEAD_DIM), lambda g, i: (0, g, 0, 0)),
                  pl.BlockSpec((1, 1, nseg, HEAD_DIM), lambda g, i: (1, g, 0, 0)),
                  pl.BlockSpec((nseg, nbp), lambda g, i: (0, 0))],
        out_specs=[pl.BlockSpec((CS_TQ, gw), lambda g, i: (i, g)),
                   pl.BlockSpec((1, CS_TQ, nbp), lambda g, i: (g, i, 0))],
        out_shape=[jax.ShapeDtypeStruct((t, B_QW), BF16),
                   jax.ShapeDtypeStruct((ng, t, nbp), BF16)],
        compiler_params=_cparams(("parallel", "parallel")),
    )(z_b, cmp_kv, cmp_kv, cmap)


SL_TQ = 512
SL_TK = SL_TQ
SL_PHASE_TILES = SL_NBP * SLC_LEN // SL_TK


def _nsa_slc_kernel(q_ref, sel_ref, k_ref, v_ref, o_ref, ka_ref, va_ref, qa_ref, s_ref, m_ref, acc_ref, *, n_phase):
    i = pl.program_id(1)
    tq = SL_TQ

    @pl.when(i == 0)
    def _():
        _fill_augmented(k_ref, v_ref, ka_ref, va_ref, SLC_LEN)

    bias = ((sel_ref[0].astype(F32) - 1.0) * (-NEG_INF)).astype(BF16)
    for hh in range(B_HPG):
        rows = slice(hh * tq, (hh + 1) * tq)
        q = (q_ref[:, hh * HEAD_DIM:(hh + 1) * HEAD_DIM].astype(F32) * Q_SCALE).astype(BF16)
        for ph in range(n_phase):
            qa_ref[ph, rows, 0:HEAD_DIM] = q
            qa_ref[ph, rows, HEAD_DIM:] = bias[:, ph * SL_NBP:(ph + 1) * SL_NBP]
    _flash_init(m_ref, None, acc_ref)
    t_pos = i * tq + (lax.broadcasted_iota(jnp.int32, (B_HPG * tq, 1), 0) & (tq - 1))

    def qk(slot, j):
        off = pl.multiple_of(j * SL_TK, SL_TK)
        qa = qa_ref[0] if n_phase == 1 else qa_ref[j // SL_PHASE_TILES]
        s_ref[slot] = _dot_nt(qa, ka_ref[pl.ds(off, SL_TK), :])

    def process(slot, j, causal):
        off = pl.multiple_of(j * SL_TK, SL_TK)
        if causal:
            kpos = off + lax.broadcasted_iota(jnp.int32, (1, SL_TK), 1)
            _mask_scores(s_ref.at[slot], kpos <= t_pos)
        _flash_update(s_ref.at[slot], va_ref[pl.ds(off, SL_TK), :], m_ref, None, acc_ref)

    _flash_sweep((i * tq + tq + SL_TK - 1) // SL_TK, qk, process)
    acc = acc_ref[...]
    o = acc[:, 0:HEAD_DIM] / acc[:, HEAD_DIM:HEAD_DIM + 1]
    for hh in range(B_HPG):
        o_ref[:, hh * HEAD_DIM:(hh + 1) * HEAD_DIM] = o[hh * tq:(hh + 1) * tq].astype(o_ref.dtype)


def _nsa_slc(z_b, sel):
    t = z_b.shape[0]
    nbp = sel.shape[2]
    n_phase = nbp // SL_NBP
    gw = B_HPG * HEAD_DIM
    rows = B_HPG * SL_TQ
    k_col0 = (B_QW + 2 * B_KVW) // HEAD_DIM
    v_col0 = (B_QW + 3 * B_KVW) // HEAD_DIM
    return pl.pallas_call(
        functools.partial(_nsa_slc_kernel, n_phase=n_phase),
        grid=(B_KV_GROUPS, t // SL_TQ),
        in_specs=[pl.BlockSpec((SL_TQ, gw), lambda g, i: (i, g)),
                  pl.BlockSpec((1, SL_TQ, nbp), lambda g, i: (g, i, 0)),
                  _resident((t, HEAD_DIM), lambda g, i: (0, k_col0 + g)),
                  _resident((t, HEAD_DIM), lambda g, i: (0, v_col0 + g))],
        out_specs=pl.BlockSpec((SL_TQ, gw), lambda g, i: (i, g)),
        out_shape=jax.ShapeDtypeStruct((t, B_QW), BF16),
        scratch_shapes=[pltpu.VMEM((t, HEAD_DIM + SL_NBP), BF16),
                        pltpu.VMEM((t, 2 * HEAD_DIM), BF16),
                        pltpu.VMEM((n_phase, rows, HEAD_DIM + SL_NBP), BF16),
                        pltpu.VMEM((2, rows, SL_TK), F32),
                        pltpu.VMEM((rows, LANES), F32),
                        pltpu.VMEM((rows, 2 * HEAD_DIM), F32)],
        compiler_params=_cparams(("arbitrary", "arbitrary")),
    )(z_b, sel, z_b, z_b)


WN_TQ = WIN


def _nsa_win_kernel(q_ref, ka_ref, kb_ref, va_ref, vb_ref, oc_ref, os_ref, gate_ref, o_ref, qs_ref):
    i = pl.program_id(1)
    tq = WN_TQ
    for hh in range(B_HPG):
        q = q_ref[:, hh * HEAD_DIM:(hh + 1) * HEAD_DIM].astype(F32) * Q_SCALE
        qs_ref[hh * tq:(hh + 1) * tq, :] = q.astype(BF16)
    qs = qs_ref[...]
    t_loc = lax.broadcasted_iota(jnp.int32, (B_HPG * tq, 1), 0) & (tq - 1)
    c_loc = lax.broadcasted_iota(jnp.int32, (1, tq), 1)
    t_prev = t_loc + jnp.where(i > 0, 0, tq)
    s_a = jnp.where(c_loc > t_prev, _dot_nt(qs, ka_ref[...]), NEG_INF)
    s_b = jnp.where(c_loc <= t_loc, _dot_nt(qs, kb_ref[...]), NEG_INF)
    m = jnp.maximum(jnp.max(s_a, axis=1, keepdims=True), jnp.max(s_b, axis=1, keepdims=True))
    p_a = jnp.exp2(s_a - m)
    p_b = jnp.exp2(s_b - m)
    l = jnp.sum(p_a, axis=1, keepdims=True) + jnp.sum(p_b, axis=1, keepdims=True)
    o_w = (_dot(p_a.astype(BF16), va_ref[...]) + _dot(p_b.astype(BF16), vb_ref[...])) / l
    gates = _sigmoid(gate_ref[...])
    for hh in range(B_HPG):
        hs = slice(hh * HEAD_DIM, (hh + 1) * HEAD_DIM)
        g_c, g_s, g_w = (gates[:, 3 * hh + c:3 * hh + c + 1] for c in range(3))
        o = (g_c * oc_ref[:, hs].astype(F32) + g_s * os_ref[:, hs].astype(F32)
             + g_w * o_w[hh * tq:(hh + 1) * tq])
        o_ref[:, hs] = o.astype(o_ref.dtype)


def _nsa_win_combine(z_b, o_c, o_s, z_gate):
    t = z_b.shape[0]
    gw = B_HPG * HEAD_DIM
    k_col0 = (B_QW + 4 * B_KVW) // HEAD_DIM
    v_col0 = (B_QW + 5 * B_KVW) // HEAD_DIM
    prev_tile = lambda col0: (lambda g, i: (jnp.maximum(i - 1, 0), col0 + g))
    this_tile = lambda col0: (lambda g, i: (i, col0 + g))
    kv_block = (WN_TQ, HEAD_DIM)
    return pl.pallas_call(
        _nsa_win_kernel,
        grid=(B_KV_GROUPS, t // WN_TQ),
        in_specs=[pl.BlockSpec((WN_TQ, gw), lambda g, i: (i, g)),
                  pl.BlockSpec(kv_block, prev_tile(k_col0)),
                  pl.BlockSpec(kv_block, this_tile(k_col0)),
                  pl.BlockSpec(kv_block, prev_tile(v_col0)),
                  pl.BlockSpec(kv_block, this_tile(v_col0)),
                  pl.BlockSpec((WN_TQ, gw), lambda g, i: (i, g)),
                  pl.BlockSpec((WN_TQ, gw), lambda g, i: (i, g)),
                  pl.BlockSpec((WN_TQ, HEAD_DIM), lambda g, i: (i, g))],
        out_specs=pl.BlockSpec((WN_TQ, gw), lambda g, i: (i, g)),
        out_shape=jax.ShapeDtypeStruct((t, B_QW), BF16),
        scratch_shapes=[pltpu.VMEM((B_HPG * WN_TQ, HEAD_DIM), BF16)],
        compiler_params=_cparams(("parallel", "parallel")),
    )(z_b, z_b, z_b, z_b, z_b, o_c, o_s, z_gate)


def _nsa(z_b, z_gate, pos_k, w1k, w2k, pos_v, w1v, w2v):
    t = z_b.shape[0]
    ng = B_KV_GROUPS

    def segs(col0):
        z = z_b[:, col0:col0 + B_KVW].reshape(t, ng, HEAD_DIM).transpose(1, 0, 2)
        return z.reshape(ng, t // CMP_STRIDE, SEG_W)

    seg = jnp.stack([segs(B_QW), segs(B_QW + B_KVW)])
    pos = jnp.stack([pos_k, pos_v]).reshape(2, 1, CMP_LEN * HEAD_DIM).astype(BF16)
    w1 = jnp.stack([w1k, w1v]).astype(BF16)
    w2 = jnp.stack([w2k, w2v]).astype(BF16)
    cmp_kv = _nsa_compress(seg, pos, w1, w2)
    o_c, sel = _nsa_cmp_select(z_b, cmp_kv)
    o_s = _nsa_slc(z_b, sel)
    return _nsa_win_combine(z_b, o_c, o_s, z_gate)


KM_ROWS = 8


def _kmean_kernel(k_ref, o_ref):
    x = k_ref[...].astype(F32)
    o_ref[...] = jnp.mean(x.reshape(KM_ROWS, MOBA_BLOCK, x.shape[1]), axis=1)


def _moba_kmean(z_cd):
    t = z_cd.shape[0]
    nbm = t // MOBA_BLOCK
    return pl.pallas_call(
        _kmean_kernel,
        grid=(nbm // KM_ROWS,),
        in_specs=[pl.BlockSpec((KM_ROWS * MOBA_BLOCK, C_W), lambda i: (i, 1))],
        out_specs=pl.BlockSpec((KM_ROWS, C_W), lambda i: (i, 0)),
        out_shape=jax.ShapeDtypeStruct((nbm, C_W), F32),
        compiler_params=_cparams(("parallel",)),
    )(z_cd)


MB_TQ = 1024
MB_TK = MB_TQ // 2
MB_UNROLL = 4
MB_NBP = 128


def _moba_kernel(q_ref, km_ref, k_ref, v_ref, o_ref, ka_ref, va_ref, qa_ref, s_ref, m_ref, acc_ref):
    i = pl.program_id(1)
    tq = MB_TQ

    @pl.when(i == 0)
    def _():
        _fill_augmented(k_ref, v_ref, ka_ref, va_ref, MOBA_BLOCK)

    q = q_ref[...]
    km = km_ref[...]
    km_hi = km.astype(BF16)
    km_lo = (km - km_hi.astype(F32)).astype(BF16)
    gate = _dot_nt(q, km_hi) + _dot_nt(q, km_lo)
    jb = lax.broadcasted_iota(jnp.int32, (1, MB_NBP), 1)
    t_pos = i * tq + lax.broadcasted_iota(jnp.int32, (tq, 1), 0)
    cur = t_pos // MOBA_BLOCK
    earlier = jb < cur
    sel = _topk_mask(jnp.where(earlier, gate, NEG_INF), jb.astype(F32), MOBA_TOPK)
    sel = jnp.where(jb == cur, 1.0, jnp.where(earlier, sel, 0.0))
    qa_ref[:, 0:HEAD_DIM] = (q.astype(F32) * Q_SCALE).astype(BF16)
    qa_ref[:, HEAD_DIM:] = ((sel - 1.0) * (-NEG_INF)).astype(BF16)
    _flash_init(m_ref, None, acc_ref)

    def qk(slot, j):
        off = pl.multiple_of(j * MB_TK, MB_TK)
        s_ref[slot] = _dot_nt(qa_ref[...], ka_ref[pl.ds(off, MB_TK), :])

    def process(slot, j, causal):
        off = pl.multiple_of(j * MB_TK, MB_TK)
        if causal:
            kpos = off + lax.broadcasted_iota(jnp.int32, (1, MB_TK), 1)
            _mask_scores(s_ref.at[slot], kpos <= t_pos)
        _flash_update(s_ref.at[slot], va_ref[pl.ds(off, MB_TK), :], m_ref, None, acc_ref)

    _flash_sweep((i * tq + tq + MB_TK - 1) // MB_TK, qk, process, MB_UNROLL)
    acc = acc_ref[...]
    o_ref[...] = (acc[:, 0:HEAD_DIM] / acc[:, HEAD_DIM:HEAD_DIM + 1]).astype(o_ref.dtype)


def _moba(z_cd):
    t = z_cd.shape[0]
    nbm = t // MOBA_BLOCK
    nh = C_HEADS
    k_mean = _moba_kmean(z_cd)
    k_mean = jnp.pad(k_mean, ((0, MB_NBP - nbm), (0, 0)))
    return pl.pallas_call(
        _moba_kernel,
        grid=(nh, t // MB_TQ),
        in_specs=[pl.BlockSpec((MB_TQ, HEAD_DIM), lambda h, i: (i, h)),
                  pl.BlockSpec((MB_NBP, HEAD_DIM), lambda h, i: (0, h)),
                  _resident((t, HEAD_DIM), lambda h, i: (0, nh + h)),
                  _resident((t, HEAD_DIM), lambda h, i: (0, 2 * nh + h))],
        out_specs=pl.BlockSpec((MB_TQ, HEAD_DIM), lambda h, i: (i, h)),
        out_shape=jax.ShapeDtypeStruct((t, C_W), BF16),
        scratch_shapes=[pltpu.VMEM((t, HEAD_DIM + MB_NBP), BF16),
                        pltpu.VMEM((t, 2 * HEAD_DIM), BF16),
                        pltpu.VMEM((MB_TQ, HEAD_DIM + MB_NBP), BF16),
                        pltpu.VMEM((2, MB_TQ, MB_TK), F32),
                        pltpu.VMEM((MB_TQ, LANES), F32),
                        pltpu.VMEM((MB_TQ, 2 * HEAD_DIM), F32)],
        compiler_params=_cparams(("arbitrary", "arbitrary")),
    )(z_cd, k_mean, z_cd, z_cd)


DF_TQ = 1024
DF_TK = DF_TQ // 2


def _diff_kernel(q_ref, k_ref, v_ref, lq1_ref, lk1_ref, lq2_ref, lk2_ref, nw_ref, o_ref,
                 qs_ref, s_ref, m_ref, l_ref, acc_ref, *, lam_init):
    i = pl.program_id(1)
    tq = DF_TQ
    for mp in range(2):
        q = q_ref[:, mp * D_DK:(mp + 1) * D_DK].astype(F32) * (D_DK ** -0.5 * LOG2E)
        qs_ref[mp] = q.astype(BF16)
    _flash_init(m_ref, l_ref, acc_ref)
    t_pos = i * tq + (lax.broadcasted_iota(jnp.int32, (2 * tq, 1), 0) & (tq - 1))

    def qk(slot, j):
        off = pl.multiple_of(j * DF_TK, DF_TK)
        for mp in range(2):
            s_ref[slot, mp * tq:(mp + 1) * tq, :] = _dot_nt(
                qs_ref[mp], k_ref[pl.ds(off, DF_TK), mp * D_DK:(mp + 1) * D_DK])

    def process(slot, j, causal):
        off = pl.multiple_of(j * DF_TK, DF_TK)
        if causal:
            kpos = off + lax.broadcasted_iota(jnp.int32, (1, DF_TK), 1)
            _mask_scores(s_ref.at[slot], kpos <= t_pos)
        _flash_update(s_ref.at[slot], v_ref[pl.ds(off, DF_TK), :], m_ref, l_ref, acc_ref)

    _flash_sweep((i * tq + tq + DF_TK - 1) // DF_TK, qk, process)
    lam = (jnp.exp(jnp.sum(lq1_ref[...] * lk1_ref[...], axis=1, keepdims=True))
           - jnp.exp(jnp.sum(lq2_ref[...] * lk2_ref[...], axis=1, keepdims=True)) + lam_init)
    o = (acc_ref[0:tq, :] / _lane_tile(l_ref[0:tq, :], D_DV)
         - lam * (acc_ref[tq:, :] / _lane_tile(l_ref[tq:, :], D_DV)))
    o = o * lax.rsqrt(jnp.mean(o * o, axis=-1, keepdims=True) + RMS_EPS) * nw_ref[...]
    o_ref[...] = (o * (1.0 - lam_init)).astype(o_ref.dtype)


def _diff_attn(z_cd, lq1, lk1, lq2, lk2, subln_w, layer_idx):
    t = z_cd.shape[0]
    lam_init = 0.8 - 0.6 * math.exp(-0.3 * layer_idx)
    q_col0 = 3 * C_W // D_DV
    k_col0 = (3 * C_W + D_QW) // D_DV
    v_col0 = (3 * C_W + 2 * D_QW) // D_DV
    vec = pl.BlockSpec((1, D_DK), lambda h, i: (0, 0))
    return pl.pallas_call(
        functools.partial(_diff_kernel, lam_init=lam_init),
        grid=(D_HEADS, t // DF_TQ),
        in_specs=[pl.BlockSpec((DF_TQ, 2 * D_DK), lambda h, i: (i, q_col0 + h)),
                  _resident((t, 2 * D_DK), lambda h, i: (0, k_col0 + h)),
                  _resident((t, D_DV), lambda h, i: (0, v_col0 + h)),
                  vec, vec, vec, vec,
                  pl.BlockSpec((1, D_DV), lambda h, i: (0, 0))],
        out_specs=pl.BlockSpec((DF_TQ, D_DV), lambda h, i: (i, h)),
        out_shape=jax.ShapeDtypeStruct((t, D_VW), BF16),
        scratch_shapes=[pltpu.VMEM((2, DF_TQ, D_DK), BF16),
                        pltpu.VMEM((2, 2 * DF_TQ, DF_TK), F32),
                        pltpu.VMEM((2 * DF_TQ, LANES), F32),
                        pltpu.VMEM((2 * DF_TQ, LANES), F32),
                        pltpu.VMEM((2 * DF_TQ, D_DV), F32)],
        compiler_params=_cparams(("parallel", "arbitrary")),
    )(z_cd, z_cd, z_cd, lq1.reshape(1, D_DK), lk1.reshape(1, D_DK), lq2.reshape(1, D_DK),
      lk2.reshape(1, D_DK), subln_w.reshape(1, D_DV))


def _gate_weight(w_gate):
    d = w_gate.shape[0]
    per = 3 * B_HPG
    wg = w_gate.reshape(d, B_KV_GROUPS, per)
    wg = jnp.pad(wg, ((0, 0), (0, 0), (0, HEAD_DIM - per)))
    return wg.reshape(d, B_KV_GROUPS * HEAD_DIM)


def kernel(x, ln_w, ln_b, ffn_w_in, ffn_w_out, ab_w_in, ab_w_out, hgrn_lower_bounds, hgrn_norm_w,
           nsa_cmp_pos_k, nsa_cmp_k_w1, nsa_cmp_k_w2, nsa_cmp_pos_v, nsa_cmp_v_w1, nsa_cmp_v_w2,
           cd_w_in, cd_w_out, diff_lambda_q1, diff_lambda_k1, diff_lambda_q2, diff_lambda_k2, diff_subln_w):
    bsz, t, d = x.shape
    lb_all = jnp.cumsum(jax.nn.softmax(hgrn_lower_bounds.astype(F32), axis=0), axis=0)
    ffn_w_out_bf = ffn_w_out.astype(BF16)
    outs = []
    for bi in range(bsz):
        h = x[bi]
        h_in = h.astype(BF16)
        for layer in range(DEPTH):
            if layer % 2 == 0:
                e = layer // 2
                n_a = 4 * A_W
                n_b = B_QW + 6 * B_KVW
                z_a = _matmul(h_in, ab_w_in, e, 0, n_a, F32)
                z_b = _matmul(h_in, ab_w_in, e, n_a, n_b, BF16)
                w_gate = _gate_weight(ab_w_in[e, :, n_a + n_b:])
                z_g = _matmul(h_in, w_gate[None], 0, 0, w_gate.shape[1], F32)
                o_a = _hgrn2(z_a, lb_all[layer], hgrn_norm_w[e])
                o_b = _nsa(z_b, z_g, nsa_cmp_pos_k[e], nsa_cmp_k_w1[e], nsa_cmp_k_w2[e],
                           nsa_cmp_pos_v[e], nsa_cmp_v_w1[e], nsa_cmp_v_w2[e])
                mix_in = (o_a, o_b)
                w_out, w_out_idx = ab_w_out, e
            else:
                oi = layer // 2
                z_cd = _matmul(h_in, cd_w_in, oi, 0, cd_w_in.shape[2], BF16)
                o_c = _moba(z_cd)
                o_d = _diff_attn(z_cd, diff_lambda_q1[oi], diff_lambda_k1[oi], diff_lambda_q2[oi],
                                 diff_lambda_k2[oi], diff_subln_w[oi], layer)
                mix_in = (o_c, o_d)
                w_out, w_out_idx = cd_w_out, oi
            h, h_in = _mm_res_ln(mix_in, w_out.astype(BF16), w_out_idx, h, ln_w[layer, 0], ln_b[layer, 0])
            act = _ffn_in(h_in, ffn_w_in, layer)
            h, h_in = _mm_res_ln((act,), ffn_w_out_bf, layer, h, ln_w[layer, 1], ln_b[layer, 1])
        outs.append(h.reshape(1, t, d))
    return outs[0] if bsz == 1 else jnp.concatenate(outs, axis=0)
```

```python
import functools
import math

import numpy as np
import jax
import jax.numpy as jnp
from jax import lax
from jax.experimental import pallas as pl
from jax.experimental.pallas import tpu as pltpu

F32 = jnp.float32
BF16 = jnp.bfloat16

DEPTH = 2
HEAD_DIM = 128
A_HEADS = 8
A_CHUNK = 64
B_HEADS = 8
B_KV_GROUPS = 2
B_HPG = B_HEADS // B_KV_GROUPS
CMP_LEN = 32
CMP_STRIDE = 16
SLC_LEN = 64
N_SLC = 16
N_FORCED = 3
WIN = 512
C_HEADS = 8
MOBA_BLOCK = 256
MOBA_TOPK = 3
D_HEADS = 4
D_DK = 128
D_DV = 2 * D_DK

DEEPNORM_ALPHA = (2 * DEPTH) ** 0.25
NEG_INF = -1e30
LN_EPS = 1e-5
RMS_EPS = 1e-6
ATT_SCALE = HEAD_DIM ** -0.5

A_W = A_HEADS * HEAD_DIM
B_QW = B_HEADS * HEAD_DIM
B_KVW = B_KV_GROUPS * HEAD_DIM
C_W = C_HEADS * HEAD_DIM
D_QW = D_HEADS * 2 * D_DK
D_VW = D_HEADS * D_DV

VMEM_LIMIT = 56 * 1024 * 1024
MM_SMALL_WEIGHT_BYTES = 8 * 1024 * 1024


def _cparams(sem):
    return pltpu.CompilerParams(dimension_semantics=sem, vmem_limit_bytes=VMEM_LIMIT)


def _dot(a, b):
    return jnp.dot(a, b, preferred_element_type=F32)


def _dot_nt(a, b):
    return lax.dot_general(a, b, (((1,), (1,)), ((), ())), preferred_element_type=F32)


def _dot_tn(a, b):
    return lax.dot_general(a, b, (((0,), (0,)), ((), ())), preferred_element_type=F32)


def _sigmoid(x):
    return 1.0 / (1.0 + jnp.exp(-x))


def _silu(x):
    return x * _sigmoid(x)


MM_TM = 1024
MM_TN = 512


def _mm_kernel(x_ref, w_ref, o_ref, wb_ref):
    @pl.when(pl.program_id(1) == 0)
    def _():
        wb_ref[...] = w_ref[0].astype(BF16)

    o_ref[...] = _dot(x_ref[...], wb_ref[...]).astype(o_ref.dtype)


def _matmul(x, w, widx, col0, n, out_dtype):
    m, k = x.shape
    tn = min(MM_TN, n)
    j0 = col0 // tn
    return pl.pallas_call(
        _mm_kernel,
        grid=(n // tn, m // MM_TM),
        in_specs=[pl.BlockSpec((MM_TM, k), lambda j, i: (i, 0)),
                  pl.BlockSpec((1, k, tn), lambda j, i: (widx, 0, j0 + j))],
        out_specs=pl.BlockSpec((MM_TM, tn), lambda j, i: (i, j)),
        out_shape=jax.ShapeDtypeStruct((m, n), out_dtype),
        scratch_shapes=[pltpu.VMEM((k, tn), BF16)],
        compiler_params=_cparams(("arbitrary", "arbitrary")),
    )(x, w)


def _ffn_in_kernel(x_ref, wg_ref, wu_ref, o_ref, wgb_ref, wub_ref):
    @pl.when(pl.program_id(1) == 0)
    def _():
        wgb_ref[...] = wg_ref[0].astype(BF16)
        wub_ref[...] = wu_ref[0].astype(BF16)

    x = x_ref[...]
    g = _dot(x, wgb_ref[...])
    u = _dot(x, wub_ref[...])
    o_ref[...] = (_silu(g) * u).astype(o_ref.dtype)


def _ffn_in(x, w_in, layer):
    m, k = x.shape
    dff = w_in.shape[2] // 2
    nj = dff // MM_TN
    return pl.pallas_call(
        _ffn_in_kernel,
        grid=(nj, m // MM_TM),
        in_specs=[pl.BlockSpec((MM_TM, k), lambda j, i: (i, 0)),
                  pl.BlockSpec((1, k, MM_TN), lambda j, i: (layer, 0, j)),
                  pl.BlockSpec((1, k, MM_TN), lambda j, i: (layer, 0, j + nj))],
        out_specs=pl.BlockSpec((MM_TM, MM_TN), lambda j, i: (i, j)),
        out_shape=jax.ShapeDtypeStruct((m, dff), BF16),
        scratch_shapes=[pltpu.VMEM((k, MM_TN), BF16), pltpu.VMEM((k, MM_TN), BF16)],
        compiler_params=_cparams(("arbitrary", "arbitrary")),
    )(x, w_in, w_in)


def _layer_norm(y, w, b):
    mu = jnp.mean(y, axis=-1, keepdims=True)
    d = y - mu
    var = jnp.mean(d * d, axis=-1, keepdims=True)
    return d * lax.rsqrt(var + LN_EPS) * w + b


def _mm_res_ln_kernel(*refs, widths):
    part_refs = refs[:len(widths)]
    w_ref, h_ref, lnw_ref, lnb_ref, o_ref, obf_ref = refs[len(widths):]
    y = DEEPNORM_ALPHA * h_ref[...]
    k0 = 0
    for a_ref, width in zip(part_refs, widths):
        y = y + _dot(a_ref[...], w_ref[0, k0:k0 + width, :])
        k0 += width
    out = _layer_norm(y, lnw_ref[...], lnb_ref[...])
    o_ref[...] = out
    obf_ref[...] = out.astype(BF16)


def _mm_res_ln(parts, w, widx, h, ln_w, ln_b):
    m = parts[0].shape[0]
    widths = tuple(a.shape[1] for a in parts)
    _, kdim, n = w.shape
    tm = 512 if kdim * n * 2 <= MM_SMALL_WEIGHT_BYTES else 256
    return pl.pallas_call(
        functools.partial(_mm_res_ln_kernel, widths=widths),
        grid=(m // tm,),
        in_specs=[pl.BlockSpec((tm, width), lambda i: (i, 0)) for width in widths]
        + [_resident((1, kdim, n), lambda i: (widx, 0, 0)),
           pl.BlockSpec((tm, n), lambda i: (i, 0)),
           pl.BlockSpec((1, n), lambda i: (0, 0)),
           pl.BlockSpec((1, n), lambda i: (0, 0))],
        out_specs=[pl.BlockSpec((tm, n), lambda i: (i, 0)),
                   pl.BlockSpec((tm, n), lambda i: (i, 0))],
        out_shape=[jax.ShapeDtypeStruct((m, n), F32), jax.ShapeDtypeStruct((m, n), BF16)],
        compiler_params=_cparams(("parallel",)),
    )(*parts, w, h, ln_w.reshape(1, n), ln_b.reshape(1, n))


LOG2E = math.log2(math.e)
LANES = 128
Q_SCALE = ATT_SCALE * LOG2E


def _flash_init(m_ref, l_ref, acc_ref):
    m_ref[...] = jnp.full(m_ref.shape, -jnp.inf, F32)
    if l_ref is not None:
        l_ref[...] = jnp.zeros(l_ref.shape, F32)
    acc_ref[...] = jnp.zeros(acc_ref.shape, F32)


def _flash_update(s_ref, v, m_ref, l_ref, acc_ref):
    m_prev = m_ref[...]
    m_new = jnp.maximum(m_prev, jnp.max(s_ref[...], axis=1, keepdims=True))
    m_ref[...] = m_new
    alpha = jnp.exp2(m_prev - m_new)
    p = jnp.exp2(s_ref[...] - _lane_tile(m_new, s_ref.shape[1]))
    if l_ref is not None:
        l_ref[...] = alpha * l_ref[...] + jnp.sum(p, axis=1, keepdims=True)
    acc_ref[...] = _lane_tile(alpha, acc_ref.shape[1]) * acc_ref[...] + _dot(p.astype(BF16), v)


def _mask_scores(s_ref, visible):
    s_ref[...] = jnp.where(visible, s_ref[...], NEG_INF)


def _lane_tile(x, width):
    reps = width // x.shape[1]
    return x if reps == 1 else jnp.concatenate([x] * reps, axis=1)


SWEEP_UNROLL = 2


def _flash_sweep(n_tiles, qk, process, unroll=SWEEP_UNROLL):
    n_pairs = (n_tiles + 1) // 2
    qk(0, 0)

    def pair(jj, carry):
        j = 2 * jj
        qk(1, j + 1)
        process(0, j, False)
        qk(0, j + 2)
        process(1, j + 1, False)
        return carry

    def pair_group(gg, carry):
        for u in range(unroll):
            pair(unroll * gg + u, carry)
        return carry

    n_loop = n_pairs - 1
    lax.fori_loop(0, n_loop // unroll, pair_group, 0)
    lax.fori_loop(n_loop - n_loop % unroll, n_loop, pair, 0)
    j_last = 2 * (n_pairs - 1)
    qk(1, j_last + 1)
    process(0, j_last, True)
    process(1, j_last + 1, True)


AUG_FILL_ROWS = 512


def _fill_augmented(k_ref, v_ref, ka_ref, va_ref, block_len):
    lane = lax.broadcasted_iota(jnp.int32, (AUG_FILL_ROWS, LANES), 1)
    ones_col = jnp.where(lane == 0, 1.0, 0.0).astype(BF16)

    def fill(c, carry):
        r0 = pl.multiple_of(c * AUG_FILL_ROWS, AUG_FILL_ROWS)
        rows = pl.ds(r0, AUG_FILL_ROWS)
        blk = (r0 + lax.broadcasted_iota(jnp.int32, (AUG_FILL_ROWS, LANES), 0)) // block_len
        ka_ref[rows, 0:HEAD_DIM] = k_ref[rows, :]
        ka_ref[rows, HEAD_DIM:] = jnp.where((blk & (LANES - 1)) == lane, 1.0, 0.0).astype(BF16)
        va_ref[rows, 0:HEAD_DIM] = v_ref[rows, :]
        va_ref[rows, HEAD_DIM:] = ones_col
        return carry

    lax.fori_loop(0, k_ref.shape[0] // AUG_FILL_ROWS, fill, 0)


def _resident(block_shape, index_map):
    return pl.BlockSpec(block_shape, index_map, pipeline_mode=pl.Buffered(1))


HG_TILE = 512
HG_HEADS = 8


def _hgrn_kernel(q_ref, f_ref, i_ref, g_ref, lb_ref, nw_ref, o_ref, st_ref):
    @pl.when(pl.program_id(1) == 0)
    def _():
        st_ref[...] = jnp.zeros_like(st_ref)

    lb = lb_ref[...]
    f = lb + (1.0 - lb) * _sigmoid(f_ref[...])
    logf = jnp.log(f)
    kk = 1.0 - f
    qf = _silu(q_ref[...])
    r64 = lax.broadcasted_iota(jnp.int32, logf.shape, 0) & (A_CHUNK - 1)
    b = logf
    step = 1
    while step < A_CHUNK:
        b = b + jnp.where(r64 >= step, pltpu.roll(b, step, 0), 0.0)
        step *= 2
    causal = (lax.broadcasted_iota(jnp.int32, (A_CHUNK, A_CHUNK), 0)
              >= lax.broadcasted_iota(jnp.int32, (A_CHUNK, A_CHUNK), 1))
    nw = nw_ref[...]
    for c in range(HG_TILE // A_CHUNK):
        sl = slice(c * A_CHUNK, (c + 1) * A_CHUNK)
        for hd in range(HG_HEADS):
            hs = slice(hd * HEAD_DIM, (hd + 1) * HEAD_DIM)
            bc = b[sl, hs]
            b_last = bc[A_CHUNK - 1:A_CHUNK, :]
            q_t = (qf[sl, hs] * jnp.exp(bc)).astype(BF16)
            k_t = (kk[sl, hs] * jnp.exp(-bc)).astype(BF16)
            vc = i_ref[sl, hs].astype(BF16)
            att = jnp.where(causal, _dot_nt(q_t, k_t), 0.0)
            st = st_ref[hd]
            o = _dot(att.astype(BF16), vc) + _dot_nt(q_t, st.astype(BF16))
            kdec = (kk[sl, hs] * jnp.exp(b_last - bc)).astype(BF16)
            st_ref[hd] = st * jnp.exp(b_last) + _dot_tn(vc, kdec)
            o = o * lax.rsqrt(jnp.mean(o * o, axis=-1, keepdims=True) + RMS_EPS) * nw
            o_ref[sl, hs] = (o * _silu(g_ref[sl, hs])).astype(o_ref.dtype)


def _hgrn2(z_a, lb, norm_w):
    t = z_a.shape[0]
    ngrp = A_HEADS // HG_HEADS
    gw = HG_HEADS * HEAD_DIM
    col = lambda base: (lambda h, i: (i, base + h))
    return pl.pallas_call(
        _hgrn_kernel,
        grid=(ngrp, t // HG_TILE),
        in_specs=[pl.BlockSpec((HG_TILE, gw), col(0)),
                  pl.BlockSpec((HG_TILE, gw), col(ngrp)),
                  pl.BlockSpec((HG_TILE, gw), col(2 * ngrp)),
                  pl.BlockSpec((HG_TILE, gw), col(3 * ngrp)),
                  pl.BlockSpec((1, gw), lambda h, i: (0, h)),
                  pl.BlockSpec((1, HEAD_DIM), lambda h, i: (0, 0))],
        out_specs=pl.BlockSpec((HG_TILE, gw), lambda h, i: (i, h)),
        out_shape=jax.ShapeDtypeStruct((t, A_W), BF16),
        scratch_shapes=[pltpu.VMEM((HG_HEADS, HEAD_DIM, HEAD_DIM), F32)],
        compiler_params=_cparams(("parallel", "arbitrary")),
    )(z_a, z_a, z_a, z_a, lb.reshape(1, A_W), norm_w.reshape(1, HEAD_DIM))


SEG_W = CMP_STRIDE * HEAD_DIM


def _nsa_compress_kernel(seg_ref, pos_ref, w1_ref, w2_ref, o_ref):
    seg = seg_ref[0, 0]
    nseg = seg.shape[0]
    a = _dot(seg, w1_ref[0, :SEG_W, :])
    b = _dot(seg, w1_ref[0, SEG_W:, :])
    b_next = pltpu.roll(b, nseg - 1, 0)
    pos = jnp.broadcast_to(pos_ref[0], (8, CMP_LEN * HEAD_DIM))
    c = _dot(pos, w1_ref[0])[0:1, :]
    hid = _silu(a + b_next + c)
    o_ref[0, 0] = _dot(hid.astype(BF16), w2_ref[0]).astype(o_ref.dtype)


def _nsa_compress(seg, pos, w1, w2):
    _, ng, nseg, _ = seg.shape
    return pl.pallas_call(
        _nsa_compress_kernel,
        grid=(2, ng),
        in_specs=[pl.BlockSpec((1, 1, nseg, SEG_W), lambda a, g: (a, g, 0, 0)),
                  pl.BlockSpec((1, 1, CMP_LEN * HEAD_DIM), lambda a, g: (a, 0, 0)),
                  pl.BlockSpec((1, CMP_LEN * HEAD_DIM, HEAD_DIM), lambda a, g: (a, 0, 0)),
                  pl.BlockSpec((1, HEAD_DIM, HEAD_DIM), lambda a, g: (a, 0, 0))],
        out_specs=pl.BlockSpec((1, 1, nseg, HEAD_DIM), lambda a, g: (a, g, 0, 0)),
        out_shape=jax.ShapeDtypeStruct((2, ng, nseg, HEAD_DIM), BF16),
        compiler_params=_cparams(("parallel", "parallel")),
    )(seg, pos, w1, w2)


CS_TQ = 256
CS_WIDTH_STEP = 256
SL_NBP = 128


def _split3(x):
    hi = x.astype(BF16)
    r = x - hi.astype(F32)
    mid = r.astype(BF16)
    lo = (r - mid.astype(F32)).astype(BF16)
    return hi, mid, lo


def _topk_mask(score, lane, k):
    sel = jnp.zeros(score.shape, F32)
    width = float(score.shape[1])
    for _ in range(k):
        mx = jnp.max(score, axis=1, keepdims=True)
        idx = jnp.min(jnp.where(score == mx, lane, width), axis=1, keepdims=True)
        hit = lane == idx
        sel = jnp.where(hit, 1.0, sel)
        score = jnp.where(hit, -jnp.inf, score)
    return sel


def _nsa_cmp_kernel(q_ref, kc_ref, vc_ref, map_ref, o_ref, sel_ref, *, n_sel):
    i = pl.program_id(1)
    tq = q_ref.shape[0]
    ncmp = kc_ref.shape[2]
    nbp = map_ref.shape[1]
    t_pos = i * tq + lax.broadcasted_iota(jnp.int32, (tq, 1), 0)
    row_has_keys = t_pos >= CMP_LEN - 1
    jb = lax.broadcasted_iota(jnp.int32, (1, nbp), 1)
    cur = t_pos // SLC_LEN
    valid = jb <= cur
    forced = (jb == 0) | (jb == cur) | (jb == cur - 1)

    def body(width):
        cmp_end = lax.broadcasted_iota(jnp.int32, (1, width), 1) * CMP_STRIDE + (CMP_LEN - 1)
        mask = cmp_end <= t_pos
        kc = kc_ref[0, 0, 0:width, :]
        vc = vc_ref[0, 0, 0:width, :]
        psum = jnp.zeros((tq, width), F32)
        for hh in range(B_HPG):
            hs = slice(hh * HEAD_DIM, (hh + 1) * HEAD_DIM)
            s = jnp.where(mask, _dot_nt(q_ref[:, hs], kc) * Q_SCALE, NEG_INF)
            e = jnp.exp2(s - jnp.max(s, axis=1, keepdims=True))
            p = e * jnp.where(row_has_keys, 1.0 / jnp.sum(e, axis=1, keepdims=True), 0.0)
            o_ref[:, hs] = _dot(p.astype(BF16), vc).astype(o_ref.dtype)
            psum = psum + p
        hi, mid, lo = _split3(psum)
        cmap = map_ref[0:width, :]
        p_slc = _dot(hi, cmap) + _dot(mid, cmap) + _dot(lo, cmap)
        score = jnp.where(valid & jnp.logical_not(forced), p_slc, NEG_INF)
        sel = _topk_mask(score, jb.astype(F32), n_sel - N_FORCED)
        sel_ref[0] = jnp.where(valid, jnp.where(forced, 1.0, sel), 0.0).astype(sel_ref.dtype)

    widths = list(range(CS_WIDTH_STEP, ncmp + 1, CS_WIDTH_STEP)) if ncmp % CS_WIDTH_STEP == 0 else [ncmp]
    if len(widths) == 1:
        body(widths[0])
    else:
        n_vis = ((i + 1) * tq - CMP_LEN) // CMP_STRIDE + 1
        case = jnp.clip((n_vis + CS_WIDTH_STEP - 1) // CS_WIDTH_STEP - 1, 0, len(widths) - 1)
        for idx, width in enumerate(widths):
            pl.when(case == idx)(functools.partial(body, width))


def _cmp_to_slc_matrix(ncmp_pad, n_cmp, nb, nbp):
    ratio = SLC_LEN // CMP_STRIDE
    n_over = CMP_LEN // CMP_STRIDE
    mat = np.zeros((ncmp_pad, nbp), np.float32)
    for j in range(nb):
        for m in range(ratio):
            for n in range(n_over):
                c = ratio * j + m - n
                if 0 <= c < n_cmp:
                    mat[c, j] += 1.0
    return mat


def _nsa_cmp_select(z_b, cmp_kv):
    t = z_b.shape[0]
    ng = B_KV_GROUPS
    nseg = cmp_kv.shape[2]
    nb = t // SLC_LEN
    n_cmp = (t - CMP_LEN) // CMP_STRIDE + 1
    nbp = -(-nb // SL_NBP) * SL_NBP
    cmap = jnp.asarray(_cmp_to_slc_matrix(nseg, n_cmp, nb, nbp), BF16)
    gw = B_HPG * HEAD_DIM
    return pl.pallas_call(
        functools.partial(_nsa_cmp_kernel, n_sel=min(N_SLC, nb)),
        grid=(ng, t // CS_TQ),
        in_specs=[pl.BlockSpec((CS_TQ, gw), lambda g, i: (i, g)),
                  pl.BlockSpec((1, 1, nseg, HEAD_DIM), lambda g, i: (0, g, 0, 0)),
                  pl.BlockSpec((1, 1, nseg, HEAD_DIM), lambda g, i: (1, g, 0, 0)),
                  pl.BlockSpec((nseg, nbp), lambda g, i: (0, 0))],
        out_specs=[pl.BlockSpec((CS_TQ, gw), lambda g, i: (i, g)),
                   pl.BlockSpec((1, CS_TQ, nbp), lambda g, i: (g, i, 0))],
        out_shape=[jax.ShapeDtypeStruct((t, B_QW), BF16),
                   jax.ShapeDtypeStruct((ng, t, nbp), BF16)],
        compiler_params=_cparams(("parallel", "parallel")),
    )(z_b, cmp_kv, cmp_kv, cmap)


SL_TQ = 512
SL_TK = SL_TQ
SL_PHASE_TILES = SL_NBP * SLC_LEN // SL_TK


def _nsa_slc_kernel(q_ref, sel_ref, k_ref, v_ref, o_ref, ka_ref, va_ref, qa_ref, s_ref, m_ref, acc_ref, *, n_phase):
    i = pl.program_id(1)
    tq = SL_TQ

    @pl.when(i == 0)
    def _():
        _fill_augmented(k_ref, v_ref, ka_ref, va_ref, SLC_LEN)

    bias = ((sel_ref[0].astype(F32) - 1.0) * (-NEG_INF)).astype(BF16)
    for hh in range(B_HPG):
        rows = slice(hh * tq, (hh + 1) * tq)
        q = (q_ref[:, hh * HEAD_DIM:(hh + 1) * HEAD_DIM].astype(F32) * Q_SCALE).astype(BF16)
        for ph in range(n_phase):
            qa_ref[ph, rows, 0:HEAD_DIM] = q
            qa_ref[ph, rows, HEAD_DIM:] = bias[:, ph * SL_NBP:(ph + 1) * SL_NBP]
    _flash_init(m_ref, None, acc_ref)
    t_pos = i * tq + (lax.broadcasted_iota(jnp.int32, (B_HPG * tq, 1), 0) & (tq - 1))

    def qk(slot, j):
        off = pl.multiple_of(j * SL_TK, SL_TK)
        qa = qa_ref[0] if n_phase == 1 else qa_ref[j // SL_PHASE_TILES]
        s_ref[slot] = _dot_nt(qa, ka_ref[pl.ds(off, SL_TK), :])

    def process(slot, j, causal):
        off = pl.multiple_of(j * SL_TK, SL_TK)
        if causal:
            kpos = off + lax.broadcasted_iota(jnp.int32, (1, SL_TK), 1)
            _mask_scores(s_ref.at[slot], kpos <= t_pos)
        _flash_update(s_ref.at[slot], va_ref[pl.ds(off, SL_TK), :], m_ref, None, acc_ref)

    _flash_sweep((i * tq + tq + SL_TK - 1) // SL_TK, qk, process)
    acc = acc_ref[...]
    o = acc[:, 0:HEAD_DIM] / acc[:, HEAD_DIM:HEAD_DIM + 1]
    for hh in range(B_HPG):
        o_ref[:, hh * HEAD_DIM:(hh + 1) * HEAD_DIM] = o[hh * tq:(hh + 1) * tq].astype(o_ref.dtype)


def _nsa_slc(z_b, sel):
    t = z_b.shape[0]
    nbp = sel.shape[2]
    n_phase = nbp // SL_NBP
    gw = B_HPG * HEAD_DIM
    rows = B_HPG * SL_TQ
    k_col0 = (B_QW + 2 * B_KVW) // HEAD_DIM
    v_col0 = (B_QW + 3 * B_KVW) // HEAD_DIM
    return pl.pallas_call(
        functools.partial(_nsa_slc_kernel, n_phase=n_phase),
        grid=(B_KV_GROUPS, t // SL_TQ),
        in_specs=[pl.BlockSpec((SL_TQ, gw), lambda g, i: (i, g)),
                  pl.BlockSpec((1, SL_TQ, nbp), lambda g, i: (g, i, 0)),
                  _resident((t, HEAD_DIM), lambda g, i: (0, k_col0 + g)),
                  _resident((t, HEAD_DIM), lambda g, i: (0, v_col0 + g))],
        out_specs=pl.BlockSpec((SL_TQ, gw), lambda g, i: (i, g)),
        out_shape=jax.ShapeDtypeStruct((t, B_QW), BF16),
        scratch_shapes=[pltpu.VMEM((t, HEAD_DIM + SL_NBP), BF16),
                        pltpu.VMEM((t, 2 * HEAD_DIM), BF16),
                        pltpu.VMEM((n_phase, rows, HEAD_DIM + SL_NBP), BF16),
                        pltpu.VMEM((2, rows, SL_TK), F32),
                        pltpu.VMEM((rows, LANES), F32),
                        pltpu.VMEM((rows, 2 * HEAD_DIM), F32)],
        compiler_params=_cparams(("arbitrary", "arbitrary")),
    )(z_b, sel, z_b, z_b)


WN_TQ = WIN


def _nsa_win_kernel(q_ref, ka_ref, kb_ref, va_ref, vb_ref, oc_ref, os_ref, gate_ref, o_ref, qs_ref):
    i = pl.program_id(1)
    tq = WN_TQ
    for hh in range(B_HPG):
        q = q_ref[:, hh * HEAD_DIM:(hh + 1) * HEAD_DIM].astype(F32) * Q_SCALE
        qs_ref[hh * tq:(hh + 1) * tq, :] = q.astype(BF16)
    qs = qs_ref[...]
    t_loc = lax.broadcasted_iota(jnp.int32, (B_HPG * tq, 1), 0) & (tq - 1)
    c_loc = lax.broadcasted_iota(jnp.int32, (1, tq), 1)
    t_prev = t_loc + jnp.where(i > 0, 0, tq)
    s_a = jnp.where(c_loc > t_prev, _dot_nt(qs, ka_ref[...]), NEG_INF)
    s_b = jnp.where(c_loc <= t_loc, _dot_nt(qs, kb_ref[...]), NEG_INF)
    m = jnp.maximum(jnp.max(s_a, axis=1, keepdims=True), jnp.max(s_b, axis=1, keepdims=True))
    p_a = jnp.exp2(s_a - m)
    p_b = jnp.exp2(s_b - m)
    l = jnp.sum(p_a, axis=1, keepdims=True) + jnp.sum(p_b, axis=1, keepdims=True)
    o_w = (_dot(p_a.astype(BF16), va_ref[...]) + _dot(p_b.astype(BF16), vb_ref[...])) / l
    gates = _sigmoid(gate_ref[...])
    for hh in range(B_HPG):
        hs = slice(hh * HEAD_DIM, (hh + 1) * HEAD_DIM)
        g_c, g_s, g_w = (gates[:, 3 * hh + c:3 * hh + c + 1] for c in range(3))
        o = (g_c * oc_ref[:, hs].astype(F32) + g_s * os_ref[:, hs].astype(F32)
             + g_w * o_w[hh * tq:(hh + 1) * tq])
        o_ref[:, hs] = o.astype(o_ref.dtype)


def _nsa_win_combine(z_b, o_c, o_s, z_gate):
    t = z_b.shape[0]
    gw = B_HPG * HEAD_DIM
    k_col0 = (B_QW + 4 * B_KVW) // HEAD_DIM
    v_col0 = (B_QW + 5 * B_KVW) // HEAD_DIM
    prev_tile = lambda col0: (lambda g, i: (jnp.maximum(i - 1, 0), col0 + g))
    this_tile = lambda col0: (lambda g, i: (i, col0 + g))
    kv_block = (WN_TQ, HEAD_DIM)
    return pl.pallas_call(
        _nsa_win_kernel,
        grid=(B_KV_GROUPS, t // WN_TQ),
        in_specs=[pl.BlockSpec((WN_TQ, gw), lambda g, i: (i, g)),
                  pl.BlockSpec(kv_block, prev_tile(k_col0)),
                  pl.BlockSpec(kv_block, this_tile(k_col0)),
                  pl.BlockSpec(kv_block, prev_tile(v_col0)),
                  pl.BlockSpec(kv_block, this_tile(v_col0)),
                  pl.BlockSpec((WN_TQ, gw), lambda g, i: (i, g)),
                  pl.BlockSpec((WN_TQ, gw), lambda g, i: (i, g)),
                  pl.BlockSpec((WN_TQ, HEAD_DIM), lambda g, i: (i, g))],
        out_specs=pl.BlockSpec((WN_TQ, gw), lambda g, i: (i, g)),
        out_shape=jax.ShapeDtypeStruct((t, B_QW), BF16),
        scratch_shapes=[pltpu.VMEM((B_HPG * WN_TQ, HEAD_DIM), BF16)],
        compiler_params=_cparams(("parallel", "parallel")),
    )(z_b, z_b, z_b, z_b, z_b, o_c, o_s, z_gate)


def _nsa(z_b, z_gate, pos_k, w1k, w2k, pos_v, w1v, w2v):
    t = z_b.shape[0]
    ng = B_KV_GROUPS

    def segs(col0):
        z = z_b[:, col0:col0 + B_KVW].reshape(t, ng, HEAD_DIM).transpose(1, 0, 2)
        return z.reshape(ng, t // CMP_STRIDE, SEG_W)

    seg = jnp.stack([segs(B_QW), segs(B_QW + B_KVW)])
    pos = jnp.stack([pos_k, pos_v]).reshape(2, 1, CMP_LEN * HEAD_DIM).astype(BF16)
    w1 = jnp.stack([w1k, w1v]).astype(BF16)
    w2 = jnp.stack([w2k, w2v]).astype(BF16)
    cmp_kv = _nsa_compress(seg, pos, w1, w2)
    o_c, sel = _nsa_cmp_select(z_b, cmp_kv)
    o_s = _nsa_slc(z_b, sel)
    return _nsa_win_combine(z_b, o_c, o_s, z_gate)


KM_ROWS = 8


def _kmean_kernel(k_ref, o_ref):
    x = k_ref[...].astype(F32)
    o_ref[...] = jnp.mean(x.reshape(KM_ROWS, MOBA_BLOCK, x.shape[1]), axis=1)


def _moba_kmean(z_cd):
    t = z_cd.shape[0]
    nbm = t // MOBA_BLOCK
    return pl.pallas_call(
        _kmean_kernel,
        grid=(nbm // KM_ROWS,),
        in_specs=[pl.BlockSpec((KM_ROWS * MOBA_BLOCK, C_W), lambda i: (i, 1))],
        out_specs=pl.BlockSpec((KM_ROWS, C_W), lambda i: (i, 0)),
        out_shape=jax.ShapeDtypeStruct((nbm, C_W), F32),
        compiler_params=_cparams(("parallel",)),
    )(z_cd)


MB_TQ = 1024
MB_TK = MB_TQ // 2
MB_UNROLL = 4
MB_NBP = 128


def _moba_kernel(q_ref, km_ref, k_ref, v_ref, o_ref, ka_ref, va_ref, qa_ref, s_ref, m_ref, acc_ref):
    i = pl.program_id(1)
    tq = MB_TQ

    @pl.when(i == 0)
    def _():
        _fill_augmented(k_ref, v_ref, ka_ref, va_ref, MOBA_BLOCK)

    q = q_ref[...]
    km = km_ref[...]
    km_hi = km.astype(BF16)
    km_lo = (km - km_hi.astype(F32)).astype(BF16)
    gate = _dot_nt(q, km_hi) + _dot_nt(q, km_lo)
    jb = lax.broadcasted_iota(jnp.int32, (1, MB_NBP), 1)
    t_pos = i * tq + lax.broadcasted_iota(jnp.int32, (tq, 1), 0)
    cur = t_pos // MOBA_BLOCK
    earlier = jb < cur
    sel = _topk_mask(jnp.where(earlier, gate, NEG_INF), jb.astype(F32), MOBA_TOPK)
    sel = jnp.where(jb == cur, 1.0, jnp.where(earlier, sel, 0.0))
    qa_ref[:, 0:HEAD_DIM] = (q.astype(F32) * Q_SCALE).astype(BF16)
    qa_ref[:, HEAD_DIM:] = ((sel - 1.0) * (-NEG_INF)).astype(BF16)
    _flash_init(m_ref, None, acc_ref)

    def qk(slot, j):
        off = pl.multiple_of(j * MB_TK, MB_TK)
        s_ref[slot] = _dot_nt(qa_ref[...], ka_ref[pl.ds(off, MB_TK), :])

    def process(slot, j, causal):
        off = pl.multiple_of(j * MB_TK, MB_TK)
        if causal:
            kpos = off + lax.broadcasted_iota(jnp.int32, (1, MB_TK), 1)
            _mask_scores(s_ref.at[slot], kpos <= t_pos)
        _flash_update(s_ref.at[slot], va_ref[pl.ds(off, MB_TK), :], m_ref, None, acc_ref)

    _flash_sweep((i * tq + tq + MB_TK - 1) // MB_TK, qk, process, MB_UNROLL)
    acc = acc_ref[...]
    o_ref[...] = (acc[:, 0:HEAD_DIM] / acc[:, HEAD_DIM:HEAD_DIM + 1]).astype(o_ref.dtype)


def _moba(z_cd):
    t = z_cd.shape[0]
    nbm = t // MOBA_BLOCK
    nh = C_HEADS
    k_mean = _moba_kmean(z_cd)
    k_mean = jnp.pad(k_mean, ((0, MB_NBP - nbm), (0, 0)))
    return pl.pallas_call(
        _moba_kernel,
        grid=(nh, t // MB_TQ),
        in_specs=[pl.BlockSpec((MB_TQ, HEAD_DIM), lambda h, i: (i, h)),
                  pl.BlockSpec((MB_NBP, HEAD_DIM), lambda h, i: (0, h)),
                  _resident((t, HEAD_DIM), lambda h, i: (0, nh + h)),
                  _resident((t, HEAD_DIM), lambda h, i: (0, 2 * nh + h))],
        out_specs=pl.BlockSpec((MB_TQ, HEAD_DIM), lambda h, i: (i, h)),
        out_shape=jax.ShapeDtypeStruct((t, C_W), BF16),
        scratch_shapes=[pltpu.VMEM((t, HEAD_DIM + MB_NBP), BF16),
                        pltpu.VMEM((t, 2 * HEAD_DIM), BF16),
                        pltpu.VMEM((MB_TQ, HEAD_DIM + MB_NBP), BF16),
                        pltpu.VMEM((2, MB_TQ, MB_TK), F32),
                        pltpu.VMEM((MB_TQ, LANES), F32),
                        pltpu.VMEM((MB_TQ, 2 * HEAD_DIM), F32)],
        compiler_params=_cparams(("arbitrary", "arbitrary")),
    )(z_cd, k_mean, z_cd, z_cd)


DF_TQ = 1024
DF_TK = DF_TQ // 2


def _diff_kernel(q_ref, k_ref, v_ref, lq1_ref, lk1_ref, lq2_ref, lk2_ref, nw_ref, o_ref,
                 qs_ref, s_ref, m_ref, l_ref, acc_ref, *, lam_init):
    i = pl.program_id(1)
    tq = DF_TQ
    for mp in range(2):
        q = q_ref[:, mp * D_DK:(mp + 1) * D_DK].astype(F32) * (D_DK ** -0.5 * LOG2E)
        qs_ref[mp] = q.astype(BF16)
    _flash_init(m_ref, l_ref, acc_ref)
    t_pos = i * tq + (lax.broadcasted_iota(jnp.int32, (2 * tq, 1), 0) & (tq - 1))

    def qk(slot, j):
        off = pl.multiple_of(j * DF_TK, DF_TK)
        for mp in range(2):
            s_ref[slot, mp * tq:(mp + 1) * tq, :] = _dot_nt(
                qs_ref[mp], k_ref[pl.ds(off, DF_TK), mp * D_DK:(mp + 1) * D_DK])

    def process(slot, j, causal):
        off = pl.multiple_of(j * DF_TK, DF_TK)
        if causal:
            kpos = off + lax.broadcasted_iota(jnp.int32, (1, DF_TK), 1)
            _mask_scores(s_ref.at[slot], kpos <= t_pos)
        _flash_update(s_ref.at[slot], v_ref[pl.ds(off, DF_TK), :], m_ref, l_ref, acc_ref)

    _flash_sweep((i * tq + tq + DF_TK - 1) // DF_TK, qk, process)
    lam = (jnp.exp(jnp.sum(lq1_ref[...] * lk1_ref[...], axis=1, keepdims=True))
           - jnp.exp(jnp.sum(lq2_ref[...] * lk2_ref[...], axis=1, keepdims=True)) + lam_init)
    o = (acc_ref[0:tq, :] / _lane_tile(l_ref[0:tq, :], D_DV)
         - lam * (acc_ref[tq:, :] / _lane_tile(l_ref[tq:, :], D_DV)))
    o = o * lax.rsqrt(jnp.mean(o * o, axis=-1, keepdims=True) + RMS_EPS) * nw_ref[...]
    o_ref[...] = (o * (1.0 - lam_init)).astype(o_ref.dtype)


def _diff_attn(z_cd, lq1, lk1, lq2, lk2, subln_w, layer_idx):
    t = z_cd.shape[0]
    lam_init = 0.8 - 0.6 * math.exp(-0.3 * layer_idx)
    q_col0 = 3 * C_W // D_DV
    k_col0 = (3 * C_W + D_QW) // D_DV
    v_col0 = (3 * C_W + 2 * D_QW) // D_DV
    vec = pl.BlockSpec((1, D_DK), lambda h, i: (0, 0))
    return pl.pallas_call(
        functools.partial(_diff_kernel, lam_init=lam_init),
        grid=(D_HEADS, t // DF_TQ),
        in_specs=[pl.BlockSpec((DF_TQ, 2 * D_DK), lambda h, i: (i, q_col0 + h)),
                  _resident((t, 2 * D_DK), lambda h, i: (0, k_col0 + h)),
                  _resident((t, D_DV), lambda h, i: (0, v_col0 + h)),
                  vec, vec, vec, vec,
                  pl.BlockSpec((1, D_DV), lambda h, i: (0, 0))],
        out_specs=pl.BlockSpec((DF_TQ, D_DV), lambda h, i: (i, h)),
        out_shape=jax.ShapeDtypeStruct((t, D_VW), BF16),
        scratch_shapes=[pltpu.VMEM((2, DF_TQ, D_DK), BF16),
                        pltpu.VMEM((2, 2 * DF_TQ, DF_TK), F32),
                        pltpu.VMEM((2 * DF_TQ, LANES), F32),
                        pltpu.VMEM((2 * DF_TQ, LANES), F32),
                        pltpu.VMEM((2 * DF_TQ, D_DV), F32)],
        compiler_params=_cparams(("parallel", "arbitrary")),
    )(z_cd, z_cd, z_cd, lq1.reshape(1, D_DK), lk1.reshape(1, D_DK), lq2.reshape(1, D_DK),
      lk2.reshape(1, D_DK), subln_w.reshape(1, D_DV))


def _gate_weight(w_gate):
    d = w_gate.shape[0]
    per = 3 * B_HPG
    wg = w_gate.reshape(d, B_KV_GROUPS, per)
    wg = jnp.pad(wg, ((0, 0), (0, 0), (0, HEAD_DIM - per)))
    return wg.reshape(d, B_KV_GROUPS * HEAD_DIM)


def kernel(x, ln_w, ln_b, ffn_w_in, ffn_w_out, ab_w_in, ab_w_out, hgrn_lower_bounds, hgrn_norm_w,
           nsa_cmp_pos_k, nsa_cmp_k_w1, nsa_cmp_k_w2, nsa_cmp_pos_v, nsa_cmp_v_w1, nsa_cmp_v_w2,
           cd_w_in, cd_w_out, diff_lambda_q1, diff_lambda_k1, diff_lambda_q2, diff_lambda_k2, diff_subln_w):
    bsz, t, d = x.shape
    lb_all = jnp.cumsum(jax.nn.softmax(hgrn_lower_bounds.astype(F32), axis=0), axis=0)
    ffn_w_out_bf = ffn_w_out.astype(BF16)
    outs = []
    for bi in range(bsz):
        h = x[bi]
        h_in = h.astype(BF16)
        for layer in range(DEPTH):
            if layer % 2 == 0:
                e = layer // 2
                n_a = 4 * A_W
                n_b = B_QW + 6 * B_KVW
                z_a = _matmul(h_in, ab_w_in, e, 0, n_a, F32)
                z_b = _matmul(h_in, ab_w_in, e, n_a, n_b, BF16)
                w_gate = _gate_weight(ab_w_in[e, :, n_a + n_b:])
                z_g = _matmul(h_in, w_gate[None], 0, 0, w_gate.shape[1], F32)
                o_a = _hgrn2(z_a, lb_all[layer], hgrn_norm_w[e])
                o_b = _nsa(z_b, z_g, nsa_cmp_pos_k[e], nsa_cmp_k_w1[e], nsa_cmp_k_w2[e],
                           nsa_cmp_pos_v[e], nsa_cmp_v_w1[e], nsa_cmp_v_w2[e])
                mix_in = (o_a, o_b)
                w_out, w_out_idx = ab_w_out, e
            else:
                oi = layer // 2
                z_cd = _matmul(h_in, cd_w_in, oi, 0, cd_w_in.shape[2], BF16)
                o_c = _moba(z_cd)
                o_d = _diff_attn(z_cd, diff_lambda_q1[oi], diff_lambda_k1[oi], diff_lambda_q2[oi],
                                 diff_lambda_k2[oi], diff_subln_w[oi], layer)
                mix_in = (o_c, o_d)
                w_out, w_out_idx = cd_w_out, oi
            h, h_in = _mm_res_ln(mix_in, w_out.astype(BF16), w_out_idx, h, ln_w[layer, 0], ln_b[layer, 0])
            act = _ffn_in(h_in, ffn_w_in, layer)
            h, h_in = _mm_res_ln((act,), ffn_w_out_bf, layer, h, ln_w[layer, 1], ln_b[layer, 1])
        outs.append(h.reshape(1, t, d))
    return outs[0] if bsz == 1 else jnp.concatenate(outs, axis=0)
```
